```python
import jax, jax.numpy as jnp
from jax import lax
import numpy as np

D_MODEL = 1024
BATCH = 32
SEQ = 2048
DEPTH = 1

CHUNK = 64
N_META = 16
META_PAD = CHUNK - N_META
D_MIX = D_MODEL
GLA_HEADS = 4
GLA_VAL = D_MIX // 2
GLA_DV = GLA_VAL // GLA_HEADS
GLA_KEY = GLA_VAL // 2
GLA_DK = GLA_KEY // GLA_HEADS
GLA_GATE_RANK = 16
GLA_TAU = 16.0
SSD_INNER = D_MIX - GLA_VAL
SSD_HEADDIM = 64
SSD_HEADS = SSD_INNER // SSD_HEADDIM
SSD_GROUPS = 2
SSD_HEADS_PER_GROUP = SSD_HEADS // SSD_GROUPS
SSD_STATE = 128
SSD_CONV = 4
SSD_CONV_CH = SSD_INNER + 2 * SSD_GROUPS * SSD_STATE
IN_SPLITS = (GLA_KEY, GLA_KEY, GLA_VAL, GLA_VAL, GLA_GATE_RANK, SSD_INNER, SSD_CONV_CH, SSD_HEADS)
D_IN_PROJ = sum(IN_SPLITS)
N_EXPERTS = 32
TOP_K = 4
D_FF = D_MODEL
SWIGLU_LIMIT = 7.0
SWIGLU_ALPHA = 1.702
MOE_BLOCK = 256
DEEPNORM_ALPHA = (2.0 * DEPTH) ** 0.25
DEEPNORM_BETA = (8.0 * DEPTH) ** -0.25
LN_EPS = 1e-5
RMS_EPS = 1e-6

kernel_name = "hymba_gla_ssd_deepnorm_moe"


def layer_norm(t, g, b):
    tf = t.astype(jnp.float32)
    mu = jnp.mean(tf, axis=-1, keepdims=True)
    var = jnp.mean(jnp.square(tf - mu), axis=-1, keepdims=True)
    return ((tf - mu) * lax.rsqrt(var + LN_EPS) * g + b).astype(t.dtype)


def rms_norm(t, g):
    tf = t.astype(jnp.float32)
    return tf * lax.rsqrt(jnp.mean(jnp.square(tf), axis=-1, keepdims=True) + RMS_EPS) * g


def split_last(t, sizes):
    idx = [int(i) for i in np.cumsum(sizes)[:-1]]
    return jnp.split(t, idx, axis=-1)


def causal_depthwise_conv(u, w, b):
    out = lax.conv_general_dilated(
        u, w[:, None, :].astype(u.dtype), window_strides=(1,), padding=[(SSD_CONV - 1, 0)],
        dimension_numbers=("NWC", "WIO", "NWC"), feature_group_count=u.shape[-1])
    return out + b


def gla_chunk_causal(q, k, v, log_a):
    b, lp, _ = q.shape
    nc = lp // CHUNK
    qc = q.reshape(b, nc, CHUNK, GLA_HEADS, GLA_DK).astype(jnp.float32)
    kc = k.reshape(b, nc, CHUNK, GLA_HEADS, GLA_DK).astype(jnp.float32)
    vc = v.reshape(b, nc, CHUNK, GLA_HEADS, GLA_DV).astype(jnp.float32)
    cum = jnp.cumsum(log_a.reshape(b, nc, CHUNK, GLA_HEADS, GLA_DK), axis=2)
    total = cum[:, :, -1]
    k_dec = kc * jnp.exp(total[:, :, None] - cum)
    u = jnp.einsum("bcqhk,bcqhv->bchkv", k_dec, vc)

    def step(s, inp):
        dec, uc = inp
        s = dec[..., None] * s + uc
        return s, s

    _, states = lax.scan(step, jnp.zeros_like(u[:, 0]),
                         (jnp.moveaxis(jnp.exp(total), 1, 0), jnp.moveaxis(u, 1, 0)))
    states = jnp.moveaxis(states, 0, 1)
    o = jnp.einsum("bcqhk,bchkv->bcqhv", qc, states)
    return o.reshape(b, lp, GLA_VAL)


def ssd_chunk_causal(xs, dt, a, bm, cm):
    b, lp, _ = xs.shape
    nc = lp // CHUNK
    g, hg = SSD_GROUPS, SSD_HEADS_PER_GROUP
    xh = xs.reshape(b, nc, CHUNK, g, hg, SSD_HEADDIM).astype(jnp.float32)
    dtc = dt.reshape(b, nc, CHUNK, g, hg)
    bc = bm.reshape(b, nc, CHUNK, g, SSD_STATE).astype(jnp.float32)
    cc = cm.reshape(b, nc, CHUNK, g, SSD_STATE).astype(jnp.float32)
    cum = jnp.cumsum(dtc * a.reshape(g, hg), axis=2)
    total = cum[:, :, -1]
    w = jnp.exp(total[:, :, None] - cum) * dtc
    u = jnp.einsum("bcqgn,bcqgh,bcqghp->bcghnp", bc, w, xh)

    def step(s, inp):
        dec, uc = inp
        s = dec[..., None, None] * s + uc
        return s, s

    _, states = lax.scan(step, jnp.zeros_like(u[:, 0]),
                         (jnp.moveaxis(jnp.exp(total), 1, 0), jnp.moveaxis(u, 1, 0)))
    states = jnp.moveaxis(states, 0, 1)
    y = jnp.einsum("bcqgn,bcghnp->bcqghp", cc, states)
    return y.reshape(b, lp, SSD_INNER)


def hybrid_mixer(h, w_in, gla_w_a2, gla_b_a, gla_norm_g, ssd_conv_w, ssd_conv_b,
                 ssd_dt_bias, ssd_a_log, ssd_d, ssd_norm_g, w_out):
    b, l, _ = h.shape
    proj = h @ w_in
    proj = jnp.pad(proj, ((0, 0), (META_PAD, 0), (0, 0)))
    lp = l + META_PAD
    valid = (jnp.arange(lp) >= META_PAD)[None, :, None]
    q, k, v, og, a1, z, xbc, dt_raw = split_last(proj, IN_SPLITS)

    log_a = jax.nn.log_sigmoid((a1 @ gla_w_a2 + gla_b_a).astype(jnp.float32)) / GLA_TAU
    log_a = jnp.where(valid, log_a, 0.0)
    o_gla = gla_chunk_causal(q * (GLA_DK ** -0.5), k, v, log_a)[:, META_PAD:]
    o_gla = rms_norm(o_gla.reshape(b, l, GLA_HEADS, GLA_DV), gla_norm_g.reshape(GLA_HEADS, GLA_DV))
    o_gla = o_gla.reshape(b, l, GLA_VAL) * jax.nn.silu(og[:, META_PAD:].astype(jnp.float32))

    xbc = jax.nn.silu(causal_depthwise_conv(xbc, ssd_conv_w, ssd_conv_b))
    xs, bm, cm = split_last(xbc, (SSD_INNER, SSD_GROUPS * SSD_STATE, SSD_GROUPS * SSD_STATE))
    dt = jax.nn.softplus(dt_raw.astype(jnp.float32) + ssd_dt_bias)
    dt = jnp.where(valid, dt, 0.0)
    a = -jnp.exp(ssd_a_log.astype(jnp.float32))
    y = ssd_chunk_causal(xs, dt, a, bm, cm)
    y = y + (xs.reshape(b, lp, SSD_HEADS, SSD_HEADDIM) * ssd_d[:, None]).reshape(b, lp, SSD_INNER)
    y = y[:, META_PAD:] * jax.nn.silu(z[:, META_PAD:].astype(jnp.float32))
    gsz = SSD_INNER // SSD_GROUPS
    y = rms_norm(y.reshape(b, l, SSD_GROUPS, gsz), ssd_norm_g.reshape(SSD_GROUPS, gsz))
    y = y.reshape(b, l, SSD_INNER)

    mix = jnp.concatenate([o_gla, y], axis=-1).astype(h.dtype)
    return mix @ w_out


def clamped_swiglu(a, u):
    a = jnp.minimum(a, SWIGLU_LIMIT)
    u = jnp.clip(u, -SWIGLU_LIMIT, SWIGLU_LIMIT)
    return a * jax.nn.sigmoid(SWIGLU_ALPHA * a) * (u + 1.0)


def moe_ffn(t, router_w, router_b, w_gate, w_up, b_gate, b_up, w_down, b_down):
    n, d = t.shape
    logits = (t @ router_w + router_b).astype(jnp.float32)
    top_v, top_i = lax.top_k(logits, TOP_K)
    gates = jax.nn.softmax(top_v, axis=-1)
    flat_e = top_i.reshape(-1)
    flat_tok = jnp.arange(n * TOP_K, dtype=jnp.int32) // TOP_K
    flat_g = gates.reshape(-1)
    order = jnp.argsort(flat_e)
    sorted_e = flat_e[order]
    counts = jnp.zeros((N_EXPERTS,), jnp.int32).at[flat_e].add(1)
    starts = jnp.cumsum(counts) - counts
    padded = (counts + MOE_BLOCK - 1) // MOE_BLOCK * MOE_BLOCK
    pad_end = jnp.cumsum(padded)
    pad_start = pad_end - padded
    dest = pad_start[sorted_e] + (jnp.arange(n * TOP_K, dtype=jnp.int32) - starts[sorted_e])
    n_blocks = -(-(n * TOP_K) // MOE_BLOCK) + N_EXPERTS
    rows = n_blocks * MOE_BLOCK
    row_tok = jnp.full((rows,), n, jnp.int32).at[dest].set(flat_tok[order])
    row_gate = jnp.zeros((rows,), jnp.float32).at[dest].set(flat_g[order])
    block_e = jnp.minimum(
        jnp.searchsorted(pad_end, jnp.arange(n_blocks, dtype=jnp.int32) * MOE_BLOCK, side="right"),
        N_EXPERTS - 1)
    t_pad = jnp.concatenate([t, jnp.zeros((1, d), t.dtype)], axis=0)

    def expert_block(inp):
        tok, gt, e = inp
        xb = t_pad[tok]
        hid = clamped_swiglu(xb @ w_gate[e] + b_gate[e], xb @ w_up[e] + b_up[e])
        yb = hid @ w_down[e] + b_down[e]
        return (yb * gt[:, None].astype(yb.dtype)).astype(t.dtype)

    yb = lax.map(expert_block, (row_tok.reshape(n_blocks, MOE_BLOCK),
                                row_gate.reshape(n_blocks, MOE_BLOCK), block_e))
    out = jax.ops.segment_sum(yb.reshape(rows, d), row_tok, num_segments=n + 1)
    return out[:n]


def setup_inputs(seed: int = 0) -> dict:
    key = jax.random.key(seed)
    ks = jax.random.split(key, 32)
    f32 = jnp.float32

    def nrm(k, shape, scale):
        return jax.random.normal(k, shape, f32) * scale

    dt0 = jnp.exp(jax.random.uniform(ks[10], (DEPTH, SSD_HEADS), f32, np.log(1e-3), np.log(1e-1)))
    return {
        "x": nrm(ks[0], (BATCH, SEQ, D_MODEL), 1.0),
        "meta_tokens": nrm(ks[1], (N_META, D_MODEL), 1.0),
        "ln_in_g": 1.0 + nrm(ks[2], (D_MODEL,), 0.02),
        "ln_in_b": nrm(ks[3], (D_MODEL,), 0.02),
        "w_in": nrm(ks[4], (DEPTH, D_MODEL, D_IN_PROJ), D_MODEL ** -0.5),
        "gla_w_a2": nrm(ks[5], (DEPTH, GLA_GATE_RANK, GLA_KEY), GLA_GATE_RANK ** -0.5),
        "gla_b_a": nrm(ks[6], (DEPTH, GLA_KEY), 0.1),
        "gla_norm_g": 1.0 + nrm(ks[7], (DEPTH, GLA_VAL), 0.02),
        "ssd_conv_w": nrm(ks[8], (DEPTH, SSD_CONV, SSD_CONV_CH), SSD_CONV ** -0.5),
        "ssd_conv_b": nrm(ks[9], (DEPTH, SSD_CONV_CH), 0.02),
        "ssd_dt_bias": dt0 + jnp.log(-jnp.expm1(-dt0)),
        "ssd_a_log": jnp.log(jax.random.uniform(ks[11], (DEPTH, SSD_HEADS), f32, 1.0, 16.0)),
        "ssd_d": 1.0 + nrm(ks[12], (DEPTH, SSD_HEADS), 0.1),
        "ssd_norm_g": 1.0 + nrm(ks[13], (DEPTH, SSD_INNER), 0.02),
        "w_out": nrm(ks[14], (DEPTH, D_MIX, D_MODEL), D_MIX ** -0.5 * DEEPNORM_BETA),
        "ln1_g": 1.0 + nrm(ks[15], (DEPTH, D_MODEL), 0.02),
        "ln1_b": nrm(ks[16], (DEPTH, D_MODEL), 0.02),
        "router_w": nrm(ks[17], (DEPTH, D_MODEL, N_EXPERTS), D_MODEL ** -0.5),
        "router_b": nrm(ks[18], (DEPTH, N_EXPERTS), 0.01),
        "moe_w_gate": nrm(ks[19], (DEPTH, N_EXPERTS, D_MODEL, D_FF), D_MODEL ** -0.5),
        "moe_w_up": nrm(ks[20], (DEPTH, N_EXPERTS, D_MODEL, D_FF), D_MODEL ** -0.5),
        "moe_b_gate": nrm(ks[21], (DEPTH, N_EXPERTS, D_FF), 0.02),
        "moe_b_up": nrm(ks[22], (DEPTH, N_EXPERTS, D_FF), 0.02),
        "moe_w_down": nrm(ks[23], (DEPTH, N_EXPERTS, D_FF, D_MODEL), D_FF ** -0.5 * DEEPNORM_BETA),
        "moe_b_down": nrm(ks[24], (DEPTH, N_EXPERTS, D_MODEL), 0.02 * DEEPNORM_BETA),
        "ln2_g": 1.0 + nrm(ks[25], (DEPTH, D_MODEL), 0.02),
        "ln2_b": nrm(ks[26], (DEPTH, D_MODEL), 0.02),
    }


def reference(x, meta_tokens, ln_in_g, ln_in_b, w_in, gla_w_a2, gla_b_a, gla_norm_g,
              ssd_conv_w, ssd_conv_b, ssd_dt_bias, ssd_a_log, ssd_d, ssd_norm_g, w_out,
              ln1_g, ln1_b, router_w, router_b, moe_w_gate, moe_w_up, moe_b_gate, moe_b_up,
              moe_w_down, moe_b_down, ln2_g, ln2_b):
    b = x.shape[0]
    meta = jnp.broadcast_to(meta_tokens[None].astype(x.dtype), (b, N_META, D_MODEL))
    h = jnp.concatenate([meta, x], axis=1)
    h = layer_norm(h, ln_in_g, ln_in_b)
    for l in range(DEPTH):
        mix = hybrid_mixer(h, w_in[l], gla_w_a2[l], gla_b_a[l], gla_norm_g[l], ssd_conv_w[l],
                           ssd_conv_b[l], ssd_dt_bias[l], ssd_a_log[l], ssd_d[l], ssd_norm_g[l],
                           w_out[l])
        h = layer_norm(DEEPNORM_ALPHA * h + mix, ln1_g[l], ln1_b[l])
        ff = moe_ffn(h.reshape(-1, D_MODEL), router_w[l], router_b[l], moe_w_gate[l], moe_w_up[l],
                     moe_b_gate[l], moe_b_up[l], moe_w_down[l], moe_b_down[l]).reshape(h.shape)
        h = layer_norm(DEEPNORM_ALPHA * h + ff, ln2_g[l], ln2_b[l])
    return h[:, N_META:]
```

```python
import functools

import numpy as np
import jax
import jax.numpy as jnp
from jax import lax
from jax.experimental import pallas as pl
from jax.experimental.pallas import tpu as pltpu

F32 = jnp.float32
BF16 = jnp.bfloat16

CHUNK = 64
N_META = 16
PAIR = 2 * CHUNK
GLA_HEADS = 4
GLA_DK = 64
GLA_DV = 128
GLA_KEY = GLA_HEADS * GLA_DK
GLA_VAL = GLA_HEADS * GLA_DV
GLA_RANK = 16
GLA_TAU = 16.0
SSD_INNER = 512
SSD_HEADS = 8
SSD_HEADDIM = 64
SSD_GROUPS = 2
SSD_STATE = 128
SSD_GROUP_W = SSD_INNER // SSD_GROUPS
SSD_CONV = 4
SSD_CONV_CH = SSD_INNER + 2 * SSD_GROUPS * SSD_STATE
TOP_K = 4
SWIGLU_LIMIT = 7.0
SWIGLU_ALPHA = 1.702
MOE_BLOCK = 256
LN_EPS = 1e-5
RMS_EPS = 1e-6
LANES = 128

C_Q, C_K, C_V, C_OG, C_Z, C_XBC, C_MISC, C_END = 0, 256, 512, 1024, 1536, 2048, 3072, 3200
MISC_A1 = 0
MISC_DT = 16
CONV_PAD = 8

MIXER_TILE = 256
VMEM_LIMIT = 56 * 1024 * 1024


def _dot(a, b):
    return jnp.dot(a, b, preferred_element_type=F32)


def _split_hi_lo(x):
    hi = x.astype(BF16)
    lo = (x - hi.astype(F32)).astype(BF16)
    return hi, lo


def _log_sigmoid(x):
    return jnp.minimum(x, 0.0) - jnp.log1p(jnp.exp(-jnp.abs(x)))


def _softplus(x):
    return jnp.maximum(x, 0.0) + jnp.log1p(jnp.exp(-jnp.abs(x)))


def _silu(x):
    return x * jax.nn.sigmoid(x)


def _layer_norm(t, g, b):
    mu = jnp.mean(t, axis=-1, keepdims=True)
    tc = t - mu
    var = jnp.mean(tc * tc, axis=-1, keepdims=True)
    return tc * lax.rsqrt(var + LN_EPS) * g + b


def _group_rms(t, g, width):
    outs = []
    for s in range(0, t.shape[-1], width):
        seg = t[:, s:s + width]
        ms = jnp.mean(seg * seg, axis=-1, keepdims=True)
        outs.append(seg * lax.rsqrt(ms + RMS_EPS) * g[:, s:s + width])
    return jnp.concatenate(outs, axis=-1)


def _mixer_tile(h, tile, valid_col, valid_row, p, s_gla, s_ssd, xbc_buf, mix_buf, need_out):
    hb = h.astype(BF16)
    w_in = p["w_in"]

    def proj(lo, hi):
        r = _dot(hb, w_in[:, lo:hi])
        if valid_col is not None:
            r = jnp.where(valid_col, r, 0.0)
        return r

    misc = proj(C_MISC, C_END)

    la = _log_sigmoid(_dot(misc.astype(BF16), p["w_a2"][...]) + p["b_a"][...]) * (1.0 / GLA_TAU)
    if valid_col is not None:
        la = jnp.where(valid_col, la, 0.0)
    la_hi, la_lo = _split_hi_lo(la)
    r = _dot(p["uo"][...], jnp.concatenate([la_hi, la_lo], axis=1))
    dec_exp = r[0:tile, 0:GLA_KEY] + r[0:tile, GLA_KEY:]
    tot_rows = r[tile:, 0:GLA_KEY] + r[tile:, GLA_KEY:]
    gla_dec_t = jnp.exp(tot_rows.T)
    kd = proj(C_K, C_V) * jnp.exp(dec_exp)
    v_bf = proj(C_V, C_OG).astype(BF16)

    misc_t = misc.T
    dt = _softplus(misc_t[MISC_DT:MISC_DT + SSD_HEADS, :] + p["dt_bias"][...])
    if valid_row is not None:
        dt = jnp.where(valid_row, dt, 0.0)
    dta = dt * (-jnp.exp(p["a_log"][...]))
    d_hi, d_lo = _split_hi_lo(dta)
    r2 = _dot(jnp.concatenate([d_hi, d_lo], axis=0), p["uto"][...])
    w = jnp.exp(r2[0:8, 0:tile] + r2[8:16, 0:tile]) * dt
    ssd_dec = jnp.exp(r2[0:8, tile:] + r2[8:16, tile:])
    w_hi, w_lo = _split_hi_lo(w)
    c_hi, c_lo = _split_hi_lo(ssd_dec)
    packed = jnp.concatenate(
        [w_hi.astype(F32), w_lo.astype(F32), c_hi.astype(F32), c_lo.astype(F32),
         jnp.zeros((LANES - 4 * SSD_HEADS, tile), F32)], axis=0)
    expanded = _dot(packed.T.astype(BF16), p["eexp"][...])
    w_exp = expanded[:, 0:SSD_INNER]
    ssd_dec_exp = expanded[:, SSD_INNER:]

    xbc_buf[CONV_PAD:CONV_PAD + tile, :] = proj(C_XBC, C_MISC)
    acc = p["conv_b"][...] + p["conv_w"][0:1, :] * xbc_buf[pl.ds(CONV_PAD - 3, tile), :]
    for j in range(1, SSD_CONV):
        acc = acc + p["conv_w"][j:j + 1, :] * xbc_buf[pl.ds(CONV_PAD - 3 + j, tile), :]
    xbc_buf[0:CONV_PAD, :] = xbc_buf[tile:tile + CONV_PAD, :]
    xa = _silu(acc)
    xs = xa[:, 0:SSD_INNER]
    bm = xa[:, SSD_INNER:SSD_INNER + SSD_GROUPS * SSD_STATE]
    cm_bf = xa[:, SSD_INNER + SSD_GROUPS * SSD_STATE:].astype(BF16)
    xw_bf = (xs * w_exp).astype(BF16)

    if need_out:
        q = proj(C_Q, C_K) * (GLA_DK ** -0.5)
        head_of_lane = lax.broadcasted_iota(jnp.int32, (CHUNK, GLA_KEY), 1) // GLA_DK

    lane = lax.broadcasted_iota(jnp.int32, (1, PAIR), 1)
    for pr in range(tile // PAIR):
        rows = slice(pr * PAIR, (pr + 1) * PAIR)
        kd_t = kd[rows].T
        bm_t = bm[rows].T
        for half in range(2):
            c = 2 * pr + half
            r0 = c * CHUNK
            sel = (lane >= half * CHUNK) & (lane < (half + 1) * CHUNK)
            kd_m = jnp.where(sel, kd_t, 0.0).astype(BF16)
            upd = jnp.concatenate(
                [_dot(kd_m[hd * GLA_DK:(hd + 1) * GLA_DK], v_bf[rows, hd * GLA_DV:(hd + 1) * GLA_DV])
                 for hd in range(GLA_HEADS)], axis=0)
            s_new = gla_dec_t[:, c:c + 1] * s_gla[...] + upd
            s_gla[...] = s_new
            bm_m = jnp.where(sel, bm_t, 0.0).astype(BF16)
            ssd_new = []
            for g in range(SSD_GROUPS):
                gl = slice(g * SSD_GROUP_W, (g + 1) * SSD_GROUP_W)
                gr = slice(g * SSD_STATE, (g + 1) * SSD_STATE)
                u = _dot(bm_m[gr], xw_bf[rows, gl])
                sg = ssd_dec_exp[r0:r0 + 1, gl] * s_ssd[gr, :] + u
                s_ssd[gr, :] = sg
                ssd_new.append(sg)
            if need_out:
                qc = q[r0:r0 + CHUNK]
                qm = jnp.concatenate(
                    [jnp.where(head_of_lane == hd, qc, 0.0) for hd in range(GLA_HEADS)],
                    axis=0).astype(BF16)
                o = _dot(qm, s_new.astype(BF16))
                for hd in range(GLA_HEADS):
                    mix_buf[r0:r0 + CHUNK, hd * GLA_DV:(hd + 1) * GLA_DV] = o[hd * CHUNK:(hd + 1) * CHUNK]
                for g in range(SSD_GROUPS):
                    y = _dot(cm_bf[r0:r0 + CHUNK, g * SSD_STATE:(g + 1) * SSD_STATE],
                             ssd_new[g].astype(BF16))
                    mix_buf[r0:r0 + CHUNK, GLA_VAL + g * SSD_GROUP_W:GLA_VAL + (g + 1) * SSD_GROUP_W] = y

    if not need_out:
        return None
    o_gla = _group_rms(mix_buf[:, 0:GLA_VAL], p["gla_norm_g"][...], GLA_DV) * _silu(proj(C_OG, C_Z))
    y = (mix_buf[:, GLA_VAL:] + xs * p["ssd_d"][...]) * _silu(proj(C_Z, C_XBC))
    y = _group_rms(y, p["ssd_norm_g"][...], SSD_GROUP_W)
    return o_gla, y


_MIX_PARAMS = ("w_in", "w_a2", "b_a", "conv_w", "conv_b", "dt_bias", "a_log", "uo", "uto", "eexp")
_OUT_PARAMS = ("gla_norm_g", "ssd_d", "ssd_norm_g")


def _meta_kernel(x_ref, ln_g, ln_b, *rest):
    np_ = len(_MIX_PARAMS)
    p = dict(zip(_MIX_PARAMS, rest[:np_]))
    s_gla_out, s_ssd_out, tail_out, xbc_buf = rest[np_:]
    tile = x_ref.shape[0]
    s_gla_out[...] = jnp.zeros_like(s_gla_out)
    s_ssd_out[...] = jnp.zeros_like(s_ssd_out)
    xbc_buf[0:CONV_PAD, :] = jnp.zeros((CONV_PAD, xbc_buf.shape[1]), F32)
    first_valid = tile - N_META
    valid_col = lax.broadcasted_iota(jnp.int32, (tile, 1), 0) >= first_valid
    valid_row = lax.broadcasted_iota(jnp.int32, (1, tile), 1) >= first_valid
    h = _layer_norm(x_ref[...], ln_g[...], ln_b[...])
    _mixer_tile(h, tile, valid_col, valid_row, p, s_gla_out, s_ssd_out, xbc_buf, None, False)
    tail_out[...] = xbc_buf[0:CONV_PAD, :]


def _mixer_kernel(alpha, n_exp, x_ref, ln_g, ln_b, *rest):
    np_ = len(_MIX_PARAMS)
    p = dict(zip(_MIX_PARAMS, rest[:np_]))
    rest = rest[np_:]
    p.update(zip(_OUT_PARAMS, rest[:3]))
    (w_out, ln1_g, ln1_b, rw_hi, rw_lo, rb, su, s_gla0, s_ssd0, tail0,
     h1_out, topi_out, gate_out, rank_out, cnt_out,
     s_gla, s_ssd, xbc_buf, mix_buf, run_cnt) = rest[3:]
    tile = x_ref.shape[0]
    b, j = pl.program_id(0), pl.program_id(1)

    @pl.when(j == 0)
    def _():
        s_gla[...] = s_gla0[...]
        s_ssd[...] = s_ssd0[...]
        xbc_buf[0:CONV_PAD, :] = tail0[...]

    @pl.when((b == 0) & (j == 0))
    def _():
        run_cnt[...] = jnp.zeros_like(run_cnt)

    h = _layer_norm(x_ref[...], ln_g[...], ln_b[...])
    o_gla, y = _mixer_tile(h, tile, None, None, p, s_gla, s_ssd, xbc_buf, mix_buf, True)
    mix = _dot(o_gla.astype(BF16), w_out[0:GLA_VAL, :]) + _dot(y.astype(BF16), w_out[GLA_VAL:, :])
    h1 = _layer_norm(alpha * h + mix, ln1_g[...], ln1_b[...])
    h1_out[...] = h1

    h_hi, h_lo = _split_hi_lo(h1)
    logits = _dot(h_hi, rw_hi[...]) + (_dot(h_hi, rw_lo[...]) + _dot(h_lo, rw_hi[...])) + rb[...]
    lt = logits.T[0:n_exp, :]
    e_iota = lax.broadcasted_iota(jnp.int32, (n_exp, tile), 0).astype(F32)
    work = lt
    vals, hots = [], []
    for k in range(TOP_K):
        m = jnp.max(work, axis=0, keepdims=True)
        idx = jnp.min(jnp.where(work == m, e_iota, float(n_exp)), axis=0, keepdims=True)
        hot = e_iota == idx
        work = jnp.where(hot, -jnp.inf, work)
        vals.append(m)
        hots.append(hot)
        topi_out[k:k + 1, :] = idx.astype(jnp.int32)
    exps = [jnp.exp(vk - vals[0]) for vk in vals]
    denom = exps[0] + exps[1] + exps[2] + exps[3]
    for k in range(TOP_K):
        gate_out[k:k + 1, :] = exps[k] / denom
    member = (hots[0] | hots[1] | hots[2] | hots[3]).astype(F32).astype(BF16)
    before = _dot(member, su[...]) + run_cnt[:, 0:1]
    for k in range(TOP_K):
        rk = jnp.sum(jnp.where(hots[k], before, 0.0), axis=0, keepdims=True)
        rank_out[k:k + 1, :] = rk.astype(jnp.int32)
    pad_rows = topi_out.shape[0] - TOP_K
    topi_out[TOP_K:, :] = jnp.zeros((pad_rows, tile), jnp.int32)
    gate_out[TOP_K:, :] = jnp.zeros((pad_rows, tile), F32)
    rank_out[TOP_K:, :] = jnp.zeros((pad_rows, tile), jnp.int32)
    run_cnt[...] = run_cnt[...] + _dot(member, jnp.ones((tile, LANES), BF16))
    cnt_out[...] = run_cnt[...]


def _clamped_swiglu(a, u):
    a = jnp.minimum(a, SWIGLU_LIMIT)
    u = jnp.clip(u, -SWIGLU_LIMIT, SWIGLU_LIMIT)
    return a * jax.nn.sigmoid(SWIGLU_ALPHA * a) * (u + 1.0)


def _row_copy(src_hbm, row, dst, slot, r, sem):
    return pltpu.make_async_copy(src_hbm.at[pl.ds(row, 1), :], dst.at[slot, pl.ds(r, 1), :], sem.at[slot])


def _experts_kernel(blk_e, n_act, tok_cur, tok_nxt, h1_hbm, wg, wu, bg, bu, wd, bd, y_out, xbuf, sem):
    i = pl.program_id(0)
    rows = xbuf.shape[1]
    slot = i % 2

    def start_rows(tok_ref, s):
        def body(r, _):
            _row_copy(h1_hbm, tok_ref[0, 0, r], xbuf, s, r, sem).start()
            return 0
        lax.fori_loop(0, rows, body, 0, unroll=8)

    @pl.when(i == 0)
    def _():
        start_rows(tok_cur, 0)

    @pl.when(i + 1 < n_act[0])
    def _():
        start_rows(tok_nxt, 1 - slot)

    @pl.when(i < n_act[0])
    def _():
        def wait_body(r, _):
            _row_copy(h1_hbm, 0, xbuf, slot, r, sem).wait()
            return 0
        lax.fori_loop(0, rows, wait_body, 0, unroll=8)
        xb = xbuf[slot].astype(BF16)
        hid = _clamped_swiglu(_dot(xb, wg[...]) + bg[...], _dot(xb, wu[...]) + bu[...])
        y_out[...] = _dot(hid.astype(BF16), wd[...]) + bd[...]

    @pl.when(i >= n_act[0])
    def _():
        y_out[...] = jnp.zeros_like(y_out)


def _combine_kernel(alpha, dest_cur, dest_nxt, gates, h1, ln_g, ln_b, y_hbm, out, ybuf, sem):
    i = pl.program_id(0)
    n = pl.num_programs(0)
    tile = h1.shape[0]
    slot = i % 2
    n_rows = TOP_K * tile

    def start_rows(dest_ref, s):
        def body(r, _):
            _row_copy(y_hbm, dest_ref[0, 0, r], ybuf, s, r, sem).start()
            return 0
        lax.fori_loop(0, n_rows, body, 0, unroll=8)

    @pl.when(i == 0)
    def _():
        start_rows(dest_cur, 0)

    @pl.when(i + 1 < n)
    def _():
        start_rows(dest_nxt, 1 - slot)

    def wait_body(r, _):
        _row_copy(y_hbm, 0, ybuf, slot, r, sem).wait()
        return 0
    lax.fori_loop(0, n_rows, wait_body, 0, unroll=8)

    g = gates[...]
    ff = g[:, 0:1] * ybuf[slot, 0:tile, :]
    for k in range(1, TOP_K):
        ff = ff + g[:, k:k + 1] * ybuf[slot, k * tile:(k + 1) * tile, :]
    out[...] = _layer_norm(alpha * h1[...] + ff, ln_g[...], ln_b[...])


def _block_constants(tile):
    t = np.arange(tile)
    same = (t[:, None] // CHUNK) == (t[None, :] // CHUNK)
    later = same & (t[None, :] > t[:, None])
    chunk_rows = np.zeros((LANES, tile), np.float32)
    chunk_rows[t // CHUNK, t] = 1.0
    uo = np.concatenate([later.astype(np.float32), chunk_rows], axis=0)
    uto = np.concatenate([later.T.astype(np.float32), same.astype(np.float32)], axis=1)
    return jnp.asarray(uo, BF16), jnp.asarray(uto, BF16)


def _expand_constant():
    e = np.zeros((LANES, 2 * SSD_INNER), np.float32)
    for part in range(4):
        for hd in range(SSD_HEADS):
            base = (part // 2) * SSD_INNER + hd * SSD_HEADDIM
            e[part * SSD_HEADS + hd, base:base + SSD_HEADDIM] = 1.0
    return jnp.asarray(e, BF16)


def _full(shape):
    return pl.BlockSpec(shape, lambda *_: (0,) * len(shape))


def kernel(x, meta_tokens, ln_in_g, ln_in_b, w_in, gla_w_a2, gla_b_a, gla_norm_g, ssd_conv_w, ssd_conv_b,
           ssd_dt_bias, ssd_a_log, ssd_d, ssd_norm_g, w_out, ln1_g, ln1_b, router_w, router_b, moe_w_gate,
           moe_w_up, moe_b_gate, moe_b_up, moe_w_down, moe_b_down, ln2_g, ln2_b):
    batch, seq, d = x.shape
    depth = w_in.shape[0]
    assert depth == 1, "single-layer stack"
    n_exp = router_w.shape[-1]
    d_ff = moe_w_gate.shape[-1]
    alpha = (2.0 * depth) ** 0.25
    tile = MIXER_TILE
    assert seq % tile == 0 and d == 1024
    n_tok = batch * seq
    row = lambda a: a.reshape(1, -1).astype(F32)

    wi = w_in[0]
    o_a1 = 1536
    o_z = o_a1 + GLA_RANK
    o_xbc = o_z + SSD_INNER
    o_dt = o_xbc + SSD_CONV_CH
    misc_w = jnp.zeros((d, LANES), F32)
    misc_w = misc_w.at[:, MISC_A1:MISC_A1 + GLA_RANK].set(wi[:, o_a1:o_z])
    misc_w = misc_w.at[:, MISC_DT:MISC_DT + SSD_HEADS].set(wi[:, o_dt:o_dt + SSD_HEADS])
    w_in_r = jnp.concatenate([wi[:, 0:o_a1], wi[:, o_z:o_xbc], wi[:, o_xbc:o_dt], misc_w], axis=1).astype(BF16)
    w_a2 = jnp.zeros((LANES, GLA_KEY), F32).at[MISC_A1:MISC_A1 + GLA_RANK].set(gla_w_a2[0]).astype(BF16)
    mix_params = dict(
        w_in=w_in_r, w_a2=w_a2, b_a=row(gla_b_a[0]), conv_w=ssd_conv_w[0].astype(F32),
        conv_b=row(ssd_conv_b[0]), dt_bias=ssd_dt_bias[0].reshape(-1, 1).astype(F32),
        a_log=ssd_a_log[0].reshape(-1, 1).astype(F32), eexp=_expand_constant())
    out_params = dict(gla_norm_g=row(gla_norm_g[0]), ssd_d=row(jnp.repeat(ssd_d[0], SSD_HEADDIM)),
                      ssd_norm_g=row(ssd_norm_g[0]))

    def mix_args(t):
        uo, uto = _block_constants(t)
        vals = dict(mix_params, uo=uo, uto=uto)
        return [vals[k] for k in _MIX_PARAMS]

    m_tile = PAIR
    x_meta = jnp.concatenate([jnp.zeros((m_tile - N_META, d), F32), meta_tokens.astype(F32)], axis=0)
    meta_in = [x_meta, row(ln_in_g), row(ln_in_b)] + mix_args(m_tile)
    s_gla0, s_ssd0, tail0 = pl.pallas_call(
        _meta_kernel,
        out_shape=(jax.ShapeDtypeStruct((GLA_KEY, GLA_DV), F32),
                   jax.ShapeDtypeStruct((SSD_GROUPS * SSD_STATE, SSD_GROUP_W), F32),
                   jax.ShapeDtypeStruct((CONV_PAD, SSD_CONV_CH), F32)),
        scratch_shapes=[pltpu.VMEM((CONV_PAD + m_tile, SSD_CONV_CH), F32)],
        compiler_params=pltpu.CompilerParams(vmem_limit_bytes=VMEM_LIMIT),
        name="meta_state",
    )(*meta_in)

    n_j = seq // tile
    rw = jnp.zeros((d, LANES), F32).at[:, 0:n_exp].set(router_w[0])
    rw_hi, rw_lo = _split_hi_lo(rw)
    rb = jnp.zeros((1, LANES), F32).at[:, 0:n_exp].set(router_b[0][None])
    su = jnp.asarray(np.triu(np.ones((tile, tile), np.float32), 1), BF16)
    args = ([x, row(ln_in_g), row(ln_in_b)] + mix_args(tile) + [out_params[k] for k in _OUT_PARAMS]
            + [w_out[0].astype(BF16), row(ln1_g[0]), row(ln1_b[0]), rw_hi, rw_lo, rb, su, s_gla0, s_ssd0, tail0])
    in_specs = [pl.BlockSpec((None, tile, d), lambda b, j: (b, j, 0))] + [_full(a.shape) for a in args[1:]]
    tok_blk = lambda b, j: (0, b * n_j + j)
    h1, topi, gates, rank, cnt = pl.pallas_call(
        functools.partial(_mixer_kernel, alpha, n_exp),
        grid=(batch, n_j),
        in_specs=in_specs,
        out_specs=(pl.BlockSpec((tile, d), lambda b, j: (b * n_j + j, 0)),
                   pl.BlockSpec((8, tile), tok_blk), pl.BlockSpec((8, tile), tok_blk),
                   pl.BlockSpec((8, tile), tok_blk), _full((n_exp, LANES))),
        out_shape=(jax.ShapeDtypeStruct((n_tok, d), F32),
                   jax.ShapeDtypeStruct((8, n_tok), jnp.int32),
                   jax.ShapeDtypeStruct((8, n_tok), F32),
                   jax.ShapeDtypeStruct((8, n_tok), jnp.int32),
                   jax.ShapeDtypeStruct((n_exp, LANES), F32)),
        scratch_shapes=[pltpu.VMEM((GLA_KEY, GLA_DV), F32),
                        pltpu.VMEM((SSD_GROUPS * SSD_STATE, SSD_GROUP_W), F32),
                        pltpu.VMEM((CONV_PAD + tile, SSD_CONV_CH), F32),
                        pltpu.VMEM((tile, d), F32),
                        pltpu.VMEM((n_exp, LANES), F32)],
        compiler_params=pltpu.CompilerParams(dimension_semantics=("arbitrary", "arbitrary"),
                                             vmem_limit_bytes=VMEM_LIMIT),
        name="mixer",
    )(*args)

    counts = cnt[:, 0].astype(jnp.int32)
    padded = (counts + MOE_BLOCK - 1) // MOE_BLOCK * MOE_BLOCK
    pad_end = jnp.cumsum(padded)
    pad_start = pad_end - padded
    top_e = topi[:TOP_K]
    dest = pad_start[top_e] + rank[:TOP_K]
    n_blocks = -(-(n_tok * TOP_K) // MOE_BLOCK) + n_exp
    n_rows = n_blocks * MOE_BLOCK
    tok_ids = jnp.broadcast_to(jnp.arange(n_tok, dtype=jnp.int32)[None], (TOP_K, n_tok))
    row_tok = jnp.zeros((n_rows,), jnp.int32).at[dest.reshape(-1)].set(tok_ids.reshape(-1))
    block_e = jnp.minimum(
        jnp.searchsorted(pad_end, jnp.arange(n_blocks, dtype=jnp.int32) * MOE_BLOCK, side="right"),
        n_exp - 1).astype(jnp.int32)
    n_act = (pad_end[-1] // MOE_BLOCK).astype(jnp.int32).reshape(1)

    row_tok3 = row_tok.reshape(n_blocks, 1, MOE_BLOCK)
    smem_blk = lambda f: pl.BlockSpec((1, 1, MOE_BLOCK), f, memory_space=pltpu.SMEM)
    e_mat = lambda shape: pl.BlockSpec((None,) + shape, lambda i, be, na: (be[i], 0, 0))
    y_sorted = pl.pallas_call(
        _experts_kernel,
        grid_spec=pltpu.PrefetchScalarGridSpec(
            num_scalar_prefetch=2,
            grid=(n_blocks,),
            in_specs=[smem_blk(lambda i, be, na: (i, 0, 0)),
                      smem_blk(lambda i, be, na: (jnp.minimum(i + 1, n_blocks - 1), 0, 0)),
                      pl.BlockSpec(memory_space=pl.ANY),
                      e_mat((d, d_ff)), e_mat((d, d_ff)), e_mat((1, d_ff)), e_mat((1, d_ff)),
                      e_mat((d_ff, d)), e_mat((1, d))],
            out_specs=pl.BlockSpec((MOE_BLOCK, d), lambda i, be, na: (i, 0)),
            scratch_shapes=[pltpu.VMEM((2, MOE_BLOCK, d), F32), pltpu.SemaphoreType.DMA((2,))]),
        out_shape=jax.ShapeDtypeStruct((n_rows, d), F32),
        compiler_params=pltpu.CompilerParams(dimension_semantics=("arbitrary",), vmem_limit_bytes=VMEM_LIMIT),
        name="moe_experts",
    )(block_e, n_act, row_tok3, row_tok3, h1,
      moe_w_gate[0].astype(BF16), moe_w_up[0].astype(BF16),
      moe_b_gate[0].reshape(n_exp, 1, d_ff).astype(F32), moe_b_up[0].reshape(n_exp, 1, d_ff).astype(F32),
      moe_w_down[0].astype(BF16), moe_b_down[0].reshape(n_exp, 1, d).astype(F32))

    c_tile = MOE_BLOCK
    n_ct = n_tok // c_tile
    dest3 = dest.reshape(TOP_K, n_ct, c_tile).transpose(1, 0, 2).reshape(n_ct, 1, TOP_K * c_tile)
    dest_blk = lambda f: pl.BlockSpec((1, 1, TOP_K * c_tile), f, memory_space=pltpu.SMEM)
    out = pl.pallas_call(
        functools.partial(_combine_kernel, alpha),
        grid=(n_ct,),
        in_specs=[dest_blk(lambda i: (i, 0, 0)),
                  dest_blk(lambda i: (jnp.minimum(i + 1, n_ct - 1), 0, 0)),
                  pl.BlockSpec((c_tile, TOP_K), lambda i: (i, 0)),
                  pl.BlockSpec((c_tile, d), lambda i: (i, 0)),
                  _full((1, d)), _full((1, d)),
                  pl.BlockSpec(memory_space=pl.ANY)],
        out_specs=pl.BlockSpec((c_tile, d), lambda i: (i, 0)),
        out_shape=jax.ShapeDtypeStruct((n_tok, d), F32),
        scratch_shapes=[pltpu.VMEM((2, TOP_K * c_tile, d), F32), pltpu.SemaphoreType.DMA((2,))],
        compiler_params=pltpu.CompilerParams(dimension_semantics=("arbitrary",), vmem_limit_bytes=VMEM_LIMIT),
        name="moe_combine",
    )(dest3, dest3, gates[:TOP_K].T, h1, row(ln2_g[0]), row(ln2_b[0]), y_sorted)
    return out.reshape(batch, seq, d).astype(x.dtype)
```

```python
import functools

import numpy as np
import jax
import jax.numpy as jnp
from jax import lax
from jax.experimental import pallas as pl
from jax.experimental.pallas import tpu as pltpu

F32 = jnp.float32
BF16 = jnp.bfloat16

CHUNK = 64
N_META = 16
PAIR = 2 * CHUNK
GLA_HEADS = 4
GLA_DK = 64
GLA_DV = 128
GLA_KEY = GLA_HEADS * GLA_DK
GLA_VAL = GLA_HEADS * GLA_DV
GLA_RANK = 16
GLA_TAU = 16.0
SSD_INNER = 512
SSD_HEADS = 8
SSD_HEADDIM = 64
SSD_GROUPS = 2
SSD_STATE = 128
SSD_GROUP_W = SSD_INNER // SSD_GROUPS
SSD_CONV = 4
SSD_CONV_CH = SSD_INNER + 2 * SSD_GROUPS * SSD_STATE
TOP_K = 4
SWIGLU_LIMIT = 7.0
SWIGLU_ALPHA = 1.702
MOE_BLOCK = 512
LN_EPS = 1e-5
RMS_EPS = 1e-6
LANES = 128

C_Q, C_K, C_V, C_OG, C_Z, C_XBC, C_MISC, C_END = 0, 256, 512, 1024, 1536, 2048, 3072, 3200
MISC_A1 = 0
MISC_DT = 16
CONV_PAD = 8

MIXER_TILE = 256
VMEM_LIMIT = 56 * 1024 * 1024


def _dot(a, b):
    return jnp.dot(a, b, preferred_element_type=F32)


def _split_hi_lo(x):
    hi = x.astype(BF16)
    lo = (x - hi.astype(F32)).astype(BF16)
    return hi, lo


def _log_sigmoid(x):
    return jnp.minimum(x, 0.0) - jnp.log1p(jnp.exp(-jnp.abs(x)))


def _softplus(x):
    return jnp.maximum(x, 0.0) + jnp.log1p(jnp.exp(-jnp.abs(x)))


def _silu(x):
    return x * jax.nn.sigmoid(x)


def _layer_norm(t, g, b):
    mu = jnp.mean(t, axis=-1, keepdims=True)
    tc = t - mu
    var = jnp.mean(tc * tc, axis=-1, keepdims=True)
    return tc * lax.rsqrt(var + LN_EPS) * g + b


def _group_rms(t, g, width):
    outs = []
    for s in range(0, t.shape[-1], width):
        seg = t[:, s:s + width]
        ms = jnp.mean(seg * seg, axis=-1, keepdims=True)
        outs.append(seg * lax.rsqrt(ms + RMS_EPS) * g[:, s:s + width])
    return jnp.concatenate(outs, axis=-1)


def _mixer_tile(h, tile, valid_col, valid_row, p, s_gla, s_ssd, xbc_buf, mix_buf, need_out):
    hb = h.astype(BF16)
    w_in = p["w_in"]

    def proj(lo, hi):
        r = _dot(hb, w_in[:, lo:hi])
        if valid_col is not None:
            r = jnp.where(valid_col, r, 0.0)
        return r

    misc = proj(C_MISC, C_END)

    la = _log_sigmoid(_dot(misc.astype(BF16), p["w_a2"][...]) + p["b_a"][...]) * (1.0 / GLA_TAU)
    if valid_col is not None:
        la = jnp.where(valid_col, la, 0.0)
    la_hi, la_lo = _split_hi_lo(la)
    r = _dot(p["uo"][...], jnp.concatenate([la_hi, la_lo], axis=1))
    dec_exp = r[0:tile, 0:GLA_KEY] + r[0:tile, GLA_KEY:]
    tot_rows = r[tile:, 0:GLA_KEY] + r[tile:, GLA_KEY:]
    gla_dec_t = jnp.exp(tot_rows.T)
    kd = proj(C_K, C_V) * jnp.exp(dec_exp)
    v_bf = proj(C_V, C_OG).astype(BF16)

    misc_t = misc.T
    dt = _softplus(misc_t[MISC_DT:MISC_DT + SSD_HEADS, :] + p["dt_bias"][...])
    if valid_row is not None:
        dt = jnp.where(valid_row, dt, 0.0)
    dta = dt * (-jnp.exp(p["a_log"][...]))
    d_hi, d_lo = _split_hi_lo(dta)
    r2 = _dot(jnp.concatenate([d_hi, d_lo], axis=0), p["uto"][...])
    w = jnp.exp(r2[0:8, 0:tile] + r2[8:16, 0:tile]) * dt
    ssd_dec = jnp.exp(r2[0:8, tile:] + r2[8:16, tile:])
    w_hi, w_lo = _split_hi_lo(w)
    c_hi, c_lo = _split_hi_lo(ssd_dec)
    packed = jnp.concatenate(
        [w_hi.astype(F32), w_lo.astype(F32), c_hi.astype(F32), c_lo.astype(F32),
         jnp.zeros((LANES - 4 * SSD_HEADS, tile), F32)], axis=0)
    expanded = _dot(packed.T.astype(BF16), p["eexp"][...])
    w_exp = expanded[:, 0:SSD_INNER]
    ssd_dec_exp = expanded[:, SSD_INNER:]

    xbc_buf[CONV_PAD:CONV_PAD + tile, :] = proj(C_XBC, C_MISC)
    acc = p["conv_b"][...] + p["conv_w"][0:1, :] * xbc_buf[pl.ds(CONV_PAD - 3, tile), :]
    for j in range(1, SSD_CONV):
        acc = acc + p["conv_w"][j:j + 1, :] * xbc_buf[pl.ds(CONV_PAD - 3 + j, tile), :]
    xbc_buf[0:CONV_PAD, :] = xbc_buf[tile:tile + CONV_PAD, :]
    xa = _silu(acc)
    xs = xa[:, 0:SSD_INNER]
    bm = xa[:, SSD_INNER:SSD_INNER + SSD_GROUPS * SSD_STATE]
    cm_bf = xa[:, SSD_INNER + SSD_GROUPS * SSD_STATE:].astype(BF16)
    xw_bf = (xs * w_exp).astype(BF16)

    if need_out:
        q = proj(C_Q, C_K) * (GLA_DK ** -0.5)
        head_of_lane = lax.broadcasted_iota(jnp.int32, (CHUNK, GLA_KEY), 1) // GLA_DK

    lane = lax.broadcasted_iota(jnp.int32, (1, PAIR), 1)
    for pr in range(tile // PAIR):
        rows = slice(pr * PAIR, (pr + 1) * PAIR)
        kd_t = kd[rows].T
        bm_t = bm[rows].T
        for half in range(2):
            c = 2 * pr + half
            r0 = c * CHUNK
            sel = (lane >= half * CHUNK) & (lane < (half + 1) * CHUNK)
            kd_m = jnp.where(sel, kd_t, 0.0).astype(BF16)
            upd = jnp.concatenate(
                [_dot(kd_m[hd * GLA_DK:(hd + 1) * GLA_DK], v_bf[rows, hd * GLA_DV:(hd + 1) * GLA_DV])
                 for hd in range(GLA_HEADS)], axis=0)
            s_new = gla_dec_t[:, c:c + 1] * s_gla[...] + upd
            s_gla[...] = s_new
            bm_m = jnp.where(sel, bm_t, 0.0).astype(BF16)
            ssd_new = []
            for g in range(SSD_GROUPS):
                gl = slice(g * SSD_GROUP_W, (g + 1) * SSD_GROUP_W)
                gr = slice(g * SSD_STATE, (g + 1) * SSD_STATE)
                u = _dot(bm_m[gr], xw_bf[rows, gl])
                sg = ssd_dec_exp[r0:r0 + 1, gl] * s_ssd[gr, :] + u
                s_ssd[gr, :] = sg
                ssd_new.append(sg)
            if need_out:
                qc = q[r0:r0 + CHUNK]
                qm = jnp.concatenate(
                    [jnp.where(head_of_lane == hd, qc, 0.0) for hd in range(GLA_HEADS)],
                    axis=0).astype(BF16)
                o = _dot(qm, s_new.astype(BF16))
                for hd in range(GLA_HEADS):
                    mix_buf[r0:r0 + CHUNK, hd * GLA_DV:(hd + 1) * GLA_DV] = o[hd * CHUNK:(hd + 1) * CHUNK]
                for g in range(SSD_GROUPS):
                    y = _dot(cm_bf[r0:r0 + CHUNK, g * SSD_STATE:(g + 1) * SSD_STATE],
                             ssd_new[g].astype(BF16))
                    mix_buf[r0:r0 + CHUNK, GLA_VAL + g * SSD_GROUP_W:GLA_VAL + (g + 1) * SSD_GROUP_W] = y

    if not need_out:
        return None
    o_gla = _group_rms(mix_buf[:, 0:GLA_VAL], p["gla_norm_g"][...], GLA_DV) * _silu(proj(C_OG, C_Z))
    y = (mix_buf[:, GLA_VAL:] + xs * p["ssd_d"][...]) * _silu(proj(C_Z, C_XBC))
    y = _group_rms(y, p["ssd_norm_g"][...], SSD_GROUP_W)
    return o_gla, y


_MIX_PARAMS = ("w_in", "w_a2", "b_a", "conv_w", "conv_b", "dt_bias", "a_log", "uo", "uto", "eexp")
_OUT_PARAMS = ("gla_norm_g", "ssd_d", "ssd_norm_g")


def _meta_kernel(x_ref, ln_g, ln_b, *rest):
    np_ = len(_MIX_PARAMS)
    p = dict(zip(_MIX_PARAMS, rest[:np_]))
    s_gla_out, s_ssd_out, tail_out, xbc_buf = rest[np_:]
    tile = x_ref.shape[0]
    s_gla_out[...] = jnp.zeros_like(s_gla_out)
    s_ssd_out[...] = jnp.zeros_like(s_ssd_out)
    xbc_buf[0:CONV_PAD, :] = jnp.zeros((CONV_PAD, xbc_buf.shape[1]), F32)
    first_valid = tile - N_META
    valid_col = lax.broadcasted_iota(jnp.int32, (tile, 1), 0) >= first_valid
    valid_row = lax.broadcasted_iota(jnp.int32, (1, tile), 1) >= first_valid
    h = _layer_norm(x_ref[...], ln_g[...], ln_b[...])
    _mixer_tile(h, tile, valid_col, valid_row, p, s_gla_out, s_ssd_out, xbc_buf, None, False)
    tail_out[...] = xbc_buf[0:CONV_PAD, :]


def _mixer_kernel(alpha, n_exp, x_ref, ln_g, ln_b, *rest):
    np_ = len(_MIX_PARAMS)
    p = dict(zip(_MIX_PARAMS, rest[:np_]))
    rest = rest[np_:]
    p.update(zip(_OUT_PARAMS, rest[:3]))
    (w_out, ln1_g, ln1_b, rw_hi, rw_lo, rb, su, s_gla0, s_ssd0, tail0,
     h1_out, topi_out, gate_out, rank_out, cnt_out,
     s_gla, s_ssd, xbc_buf, mix_buf, run_cnt) = rest[3:]
    tile = x_ref.shape[0]
    b, j = pl.program_id(0), pl.program_id(1)

    @pl.when(j == 0)
    def _():
        s_gla[...] = s_gla0[...]
        s_ssd[...] = s_ssd0[...]
        xbc_buf[0:CONV_PAD, :] = tail0[...]

    @pl.when((b == 0) & (j == 0))
    def _():
        run_cnt[...] = jnp.zeros_like(run_cnt)

    h = _layer_norm(x_ref[...], ln_g[...], ln_b[...])
    o_gla, y = _mixer_tile(h, tile, None, None, p, s_gla, s_ssd, xbc_buf, mix_buf, True)
    mix = _dot(o_gla.astype(BF16), w_out[0:GLA_VAL, :]) + _dot(y.astype(BF16), w_out[GLA_VAL:, :])
    h1 = _layer_norm(alpha * h + mix, ln1_g[...], ln1_b[...])
    h1_out[...] = h1

    h_hi, h_lo = _split_hi_lo(h1)
    logits = _dot(h_hi, rw_hi[...]) + (_dot(h_hi, rw_lo[...]) + _dot(h_lo, rw_hi[...])) + rb[...]
    lt = logits.T[0:n_exp, :]
    e_iota = lax.broadcasted_iota(jnp.int32, (n_exp, tile), 0).astype(F32)
    work = lt
    vals, hots = [], []
    for k in range(TOP_K):
        m = jnp.max(work, axis=0, keepdims=True)
        idx = jnp.min(jnp.where(work == m, e_iota, float(n_exp)), axis=0, keepdims=True)
        hot = e_iota == idx
        work = jnp.where(hot, -jnp.inf, work)
        vals.append(m)
        hots.append(hot)
        topi_out[k:k + 1, :] = idx.astype(jnp.int32)
    exps = [jnp.exp(vk - vals[0]) for vk in vals]
    denom = exps[0] + exps[1] + exps[2] + exps[3]
    for k in range(TOP_K):
        gate_out[k:k + 1, :] = exps[k] / denom
    member = (hots[0] | hots[1] | hots[2] | hots[3]).astype(F32).astype(BF16)
    before = _dot(member, su[...]) + run_cnt[:, 0:1]
    for k in range(TOP_K):
        rk = jnp.sum(jnp.where(hots[k], before, 0.0), axis=0, keepdims=True)
        rank_out[k:k + 1, :] = rk.astype(jnp.int32)
    pad_rows = topi_out.shape[0] - TOP_K
    topi_out[TOP_K:, :] = jnp.zeros((pad_rows, tile), jnp.int32)
    gate_out[TOP_K:, :] = jnp.zeros((pad_rows, tile), F32)
    rank_out[TOP_K:, :] = jnp.zeros((pad_rows, tile), jnp.int32)
    run_cnt[...] = run_cnt[...] + _dot(member, jnp.ones((tile, LANES), BF16))
    cnt_out[...] = run_cnt[...]


def _clamped_swiglu(a, u):
    a = jnp.minimum(a, SWIGLU_LIMIT)
    u = jnp.clip(u, -SWIGLU_LIMIT, SWIGLU_LIMIT)
    return a * jax.nn.sigmoid(SWIGLU_ALPHA * a) * (u + 1.0)


SUB = 8


def _dispatch_kernel(fill_lo, fill_hi, dest, h1_hbm, xs_hbm, tbuf, zrow, lsem, rsem, zsem):
    i = pl.program_id(0)
    n = pl.num_programs(0)
    groups = tbuf.shape[1]
    tile = groups * SUB
    slot = i % 2
    n_exp = fill_lo.shape[0]

    def tile_load(j, s):
        return pltpu.make_async_copy(h1_hbm.at[pl.ds(j * groups, groups)], tbuf.at[s], lsem.at[s])

    def row_copy(s, g, r, dst_row):
        return pltpu.make_async_copy(tbuf.at[s, g, pl.ds(r, 1), :], xs_hbm.at[pl.ds(dst_row, 1), :], rsem.at[s])

    def zero_copy(dst_row):
        return pltpu.make_async_copy(zrow.at[pl.ds(0, 1), :], xs_hbm.at[pl.ds(dst_row, 1), :], zsem.at[0])

    def wait_rows(s):
        def body(g, _):
            for _k in range(TOP_K):
                for r in range(SUB):
                    row_copy(s, g, r, 0).wait()
            return 0
        lax.fori_loop(0, groups, body, 0)

    @pl.when(i == 0)
    def _():
        tile_load(0, 0).start()
        zrow[...] = jnp.zeros_like(zrow)
        for wait in (False, True):
            def per_expert(e, _, wait=wait):
                def body(row, _):
                    if wait:
                        zero_copy(0).wait()
                    else:
                        zero_copy(row).start()
                    return 0
                lax.fori_loop(fill_lo[e], fill_hi[e], body, 0)
                return 0
            lax.fori_loop(0, n_exp, per_expert, 0)

    @pl.when(i > 0)
    def _():
        wait_rows(1 - slot)

    @pl.when(i + 1 < n)
    def _():
        tile_load(i + 1, 1 - slot).start()

    tile_load(i, slot).wait()

    def issue(g, _):
        for k in range(TOP_K):
            for r in range(SUB):
                row_copy(slot, g, r, dest[0, 0, k * tile + g * SUB + r]).start()
        return 0
    lax.fori_loop(0, groups, issue, 0)

    @pl.when(i == n - 1)
    def _():
        wait_rows(slot)


def _experts_kernel(blk_e, n_act, x_ref, wg, wu, bg, bu, wd, bd, y_out):
    i = pl.program_id(0)

    @pl.when(i < n_act[0])
    def _():
        xb = x_ref[...].astype(BF16)
        hid = _clamped_swiglu(_dot(xb, wg[...]) + bg[...], _dot(xb, wu[...]) + bu[...])
        y_out[...] = _dot(hid.astype(BF16), wd[...]) + bd[...]

    @pl.when(i >= n_act[0])
    def _():
        y_out[...] = jnp.zeros_like(y_out)


def _combine_kernel(alpha, dest_cur, dest_nxt, gates, h1, ln_g, ln_b, y_hbm, out, ybuf, sem):
    i = pl.program_id(0)
    n = pl.num_programs(0)
    tile = h1.shape[0]
    groups = tile // SUB
    slot = i % 2

    def row_copy(s, k, g, r, src_row):
        return pltpu.make_async_copy(y_hbm.at[pl.ds(src_row, 1), :], ybuf.at[s, k, g, pl.ds(r, 1), :], sem.at[s])

    def start_rows(dest_ref, s):
        def body(g, _):
            for k in range(TOP_K):
                for r in range(SUB):
                    row_copy(s, k, g, r, dest_ref[0, 0, k * tile + g * SUB + r]).start()
            return 0
        lax.fori_loop(0, groups, body, 0)

    @pl.when(i == 0)
    def _():
        start_rows(dest_cur, 0)

    @pl.when(i + 1 < n)
    def _():
        start_rows(dest_nxt, 1 - slot)

    def wait_body(g, _):
        for k in range(TOP_K):
            for r in range(SUB):
                row_copy(slot, k, g, r, 0).wait()
        return 0
    lax.fori_loop(0, groups, wait_body, 0)

    g = gates[...]
    ff = g[:, 0:1] * ybuf[slot, 0].reshape(tile, -1)
    for k in range(1, TOP_K):
        ff = ff + g[:, k:k + 1] * ybuf[slot, k].reshape(tile, -1)
    out[...] = _layer_norm(alpha * h1[...] + ff, ln_g[...], ln_b[...])


def _block_constants(tile):
    t = np.arange(tile)
    same = (t[:, None] // CHUNK) == (t[None, :] // CHUNK)
    later = same & (t[None, :] > t[:, None])
    chunk_rows = np.zeros((LANES, tile), np.float32)
    chunk_rows[t // CHUNK, t] = 1.0
    uo = np.concatenate([later.astype(np.float32), chunk_rows], axis=0)
    uto = np.concatenate([later.T.astype(np.float32), same.astype(np.float32)], axis=1)
    return jnp.asarray(uo, BF16), jnp.asarray(uto, BF16)


def _expand_constant():
    e = np.zeros((LANES, 2 * SSD_INNER), np.float32)
    for part in range(4):
        for hd in range(SSD_HEADS):
            base = (part // 2) * SSD_INNER + hd * SSD_HEADDIM
            e[part * SSD_HEADS + hd, base:base + SSD_HEADDIM] = 1.0
    return jnp.asarray(e, BF16)


def _full(shape):
    return pl.BlockSpec(shape, lambda *_: (0,) * len(shape))


def kernel(x, meta_tokens, ln_in_g, ln_in_b, w_in, gla_w_a2, gla_b_a, gla_norm_g, ssd_conv_w, ssd_conv_b,
           ssd_dt_bias, ssd_a_log, ssd_d, ssd_norm_g, w_out, ln1_g, ln1_b, router_w, router_b, moe_w_gate,
           moe_w_up, moe_b_gate, moe_b_up, moe_w_down, moe_b_down, ln2_g, ln2_b):
    batch, seq, d = x.shape
    depth = w_in.shape[0]
    assert depth == 1, "single-layer stack"
    n_exp = router_w.shape[-1]
    d_ff = moe_w_gate.shape[-1]
    alpha = (2.0 * depth) ** 0.25
    tile = MIXER_TILE
    assert seq % tile == 0 and d == 1024
    n_tok = batch * seq
    row = lambda a: a.reshape(1, -1).astype(F32)

    wi = w_in[0]
    o_a1 = 1536
    o_z = o_a1 + GLA_RANK
    o_xbc = o_z + SSD_INNER
    o_dt = o_xbc + SSD_CONV_CH
    misc_w = jnp.zeros((d, LANES), F32)
    misc_w = misc_w.at[:, MISC_A1:MISC_A1 + GLA_RANK].set(wi[:, o_a1:o_z])
    misc_w = misc_w.at[:, MISC_DT:MISC_DT + SSD_HEADS].set(wi[:, o_dt:o_dt + SSD_HEADS])
    w_in_r = jnp.concatenate([wi[:, 0:o_a1], wi[:, o_z:o_xbc], wi[:, o_xbc:o_dt], misc_w], axis=1).astype(BF16)
    w_a2 = jnp.zeros((LANES, GLA_KEY), F32).at[MISC_A1:MISC_A1 + GLA_RANK].set(gla_w_a2[0]).astype(BF16)
    mix_params = dict(
        w_in=w_in_r, w_a2=w_a2, b_a=row(gla_b_a[0]), conv_w=ssd_conv_w[0].astype(F32),
        conv_b=row(ssd_conv_b[0]), dt_bias=ssd_dt_bias[0].reshape(-1, 1).astype(F32),
        a_log=ssd_a_log[0].reshape(-1, 1).astype(F32), eexp=_expand_constant())
    out_params = dict(gla_norm_g=row(gla_norm_g[0]), ssd_d=row(jnp.repeat(ssd_d[0], SSD_HEADDIM)),
                      ssd_norm_g=row(ssd_norm_g[0]))

    def mix_args(t):
        uo, uto = _block_constants(t)
        vals = dict(mix_params, uo=uo, uto=uto)
        return [vals[k] for k in _MIX_PARAMS]

    m_tile = PAIR
    x_meta = jnp.concatenate([jnp.zeros((m_tile - N_META, d), F32), meta_tokens.astype(F32)], axis=0)
    meta_in = [x_meta, row(ln_in_g), row(ln_in_b)] + mix_args(m_tile)
    s_gla0, s_ssd0, tail0 = pl.pallas_call(
        _meta_kernel,
        out_shape=(jax.ShapeDtypeStruct((GLA_KEY, GLA_DV), F32),
                   jax.ShapeDtypeStruct((SSD_GROUPS * SSD_STATE, SSD_GROUP_W), F32),
                   jax.ShapeDtypeStruct((CONV_PAD, SSD_CONV_CH), F32)),
        scratch_shapes=[pltpu.VMEM((CONV_PAD + m_tile, SSD_CONV_CH), F32)],
        compiler_params=pltpu.CompilerParams(vmem_limit_bytes=VMEM_LIMIT),
        name="meta_state",
    )(*meta_in)

    n_j = seq // tile
    rw = jnp.zeros((d, LANES), F32).at[:, 0:n_exp].set(router_w[0])
    rw_hi, rw_lo = _split_hi_lo(rw)
    rb = jnp.zeros((1, LANES), F32).at[:, 0:n_exp].set(router_b[0][None])
    su = jnp.asarray(np.triu(np.ones((tile, tile), np.float32), 1), BF16)
    args = ([x, row(ln_in_g), row(ln_in_b)] + mix_args(tile) + [out_params[k] for k in _OUT_PARAMS]
            + [w_out[0].astype(BF16), row(ln1_g[0]), row(ln1_b[0]), rw_hi, rw_lo, rb, su, s_gla0, s_ssd0, tail0])
    in_specs = [pl.BlockSpec((None, tile, d), lambda b, j: (b, j, 0))] + [_full(a.shape) for a in args[1:]]
    tok_blk = lambda b, j: (0, b * n_j + j)
    h1, topi, gates, rank, cnt = pl.pallas_call(
        functools.partial(_mixer_kernel, alpha, n_exp),
        grid=(batch, n_j),
        in_specs=in_specs,
        out_specs=(pl.BlockSpec((tile, d), lambda b, j: (b * n_j + j, 0)),
                   pl.BlockSpec((8, tile), tok_blk), pl.BlockSpec((8, tile), tok_blk),
                   pl.BlockSpec((8, tile), tok_blk), _full((n_exp, LANES))),
        out_shape=(jax.ShapeDtypeStruct((n_tok, d), F32),
                   jax.ShapeDtypeStruct((8, n_tok), jnp.int32),
                   jax.ShapeDtypeStruct((8, n_tok), F32),
                   jax.ShapeDtypeStruct((8, n_tok), jnp.int32),
                   jax.ShapeDtypeStruct((n_exp, LANES), F32)),
        scratch_shapes=[pltpu.VMEM((GLA_KEY, GLA_DV), F32),
                        pltpu.VMEM((SSD_GROUPS * SSD_STATE, SSD_GROUP_W), F32),
                        pltpu.VMEM((CONV_PAD + tile, SSD_CONV_CH), F32),
                        pltpu.VMEM((tile, d), F32),
                        pltpu.VMEM((n_exp, LANES), F32)],
        compiler_params=pltpu.CompilerParams(dimension_semantics=("arbitrary", "arbitrary"),
                                             vmem_limit_bytes=VMEM_LIMIT),
        name="mixer",
    )(*args)

    counts = cnt[:, 0].astype(jnp.int32)
    padded = (counts + MOE_BLOCK - 1) // MOE_BLOCK * MOE_BLOCK
    pad_end = jnp.cumsum(padded)
    pad_start = pad_end - padded
    top_e = topi[:TOP_K]
    e_ids = jnp.arange(n_exp, dtype=jnp.int32)[:, None, None]
    dest = rank[:TOP_K] + jnp.sum(jnp.where(top_e[None] == e_ids, pad_start[:, None, None], 0), axis=0)
    n_blocks = -(-(n_tok * TOP_K) // MOE_BLOCK) + n_exp
    n_rows = n_blocks * MOE_BLOCK
    blk_lo = jnp.arange(n_blocks, dtype=jnp.int32) * MOE_BLOCK
    block_e = jnp.minimum(jnp.sum((blk_lo[:, None] >= pad_end[None, :]).astype(jnp.int32), axis=1), n_exp - 1)
    n_act = (pad_end[-1] // MOE_BLOCK).astype(jnp.int32).reshape(1)

    c_tile = 256
    n_ct = n_tok // c_tile
    dest3 = dest.reshape(TOP_K, n_ct, c_tile).transpose(1, 0, 2).reshape(n_ct, 1, TOP_K * c_tile)
    dest_blk = lambda f: pl.BlockSpec((1, 1, TOP_K * c_tile), f, memory_space=pltpu.SMEM)
    x_sorted = pl.pallas_call(
        _dispatch_kernel,
        grid_spec=pltpu.PrefetchScalarGridSpec(
            num_scalar_prefetch=2,
            grid=(n_ct,),
            in_specs=[dest_blk(lambda i, lo, hi: (i, 0, 0)), pl.BlockSpec(memory_space=pl.ANY)],
            out_specs=pl.BlockSpec(memory_space=pl.ANY),
            scratch_shapes=[pltpu.VMEM((2, c_tile // SUB, SUB, d), F32), pltpu.VMEM((SUB, d), F32),
                            pltpu.SemaphoreType.DMA((2,)), pltpu.SemaphoreType.DMA((2,)),
                            pltpu.SemaphoreType.DMA((1,))]),
        out_shape=jax.ShapeDtypeStruct((n_rows, d), F32),
        compiler_params=pltpu.CompilerParams(dimension_semantics=("arbitrary",), vmem_limit_bytes=VMEM_LIMIT),
        name="moe_dispatch",
    )(pad_start + counts, pad_end, dest3, h1.reshape(n_tok // SUB, SUB, d))

    last_act = lambda i, na: jnp.minimum(i, na[0] - 1)
    e_mat = lambda shape: pl.BlockSpec((None,) + shape, lambda i, be, na: (be[i], 0, 0))
    y_sorted = pl.pallas_call(
        _experts_kernel,
        grid_spec=pltpu.PrefetchScalarGridSpec(
            num_scalar_prefetch=2,
            grid=(n_blocks,),
            in_specs=[pl.BlockSpec((MOE_BLOCK, d), lambda i, be, na: (last_act(i, na), 0)),
                      e_mat((d, d_ff)), e_mat((d, d_ff)), e_mat((1, d_ff)), e_mat((1, d_ff)),
                      e_mat((d_ff, d)), e_mat((1, d))],
            out_specs=pl.BlockSpec((MOE_BLOCK, d), lambda i, be, na: (i, 0))),
        out_shape=jax.ShapeDtypeStruct((n_rows, d), F32),
        compiler_params=pltpu.CompilerParams(dimension_semantics=("arbitrary",), vmem_limit_bytes=VMEM_LIMIT),
        name="moe_experts",
    )(block_e, n_act, x_sorted,
      moe_w_gate[0].astype(BF16), moe_w_up[0].astype(BF16),
      moe_b_gate[0].reshape(n_exp, 1, d_ff).astype(F32), moe_b_up[0].reshape(n_exp, 1, d_ff).astype(F32),
      moe_w_down[0].astype(BF16), moe_b_down[0].reshape(n_exp, 1, d).astype(F32))

    out = pl.pallas_call(
        functools.partial(_combine_kernel, alpha),
        grid=(n_ct,),
        in_specs=[dest_blk(lambda i: (i, 0, 0)),
                  dest_blk(lambda i: (jnp.minimum(i + 1, n_ct - 1), 0, 0)),
                  pl.BlockSpec((c_tile, TOP_K), lambda i: (i, 0)),
                  pl.BlockSpec((c_tile, d), lambda i: (i, 0)),
                  _full((1, d)), _full((1, d)),
                  pl.BlockSpec(memory_space=pl.ANY)],
        out_specs=pl.BlockSpec((c_tile, d), lambda i: (i, 0)),
        out_shape=jax.ShapeDtypeStruct((n_tok, d), F32),
        scratch_shapes=[pltpu.VMEM((2, TOP_K, c_tile // SUB, SUB, d), F32), pltpu.SemaphoreType.DMA((2,))],
        compiler_params=pltpu.CompilerParams(dimension_semantics=("arbitrary",), vmem_limit_bytes=VMEM_LIMIT),
        name="moe_combine",
    )(dest3, dest3, gates[:TOP_K].T, h1, row(ln2_g[0]), row(ln2_b[0]), y_sorted)
    return out.reshape(batch, seq, d).astype(x.dtype)
```

```python
import functools

import numpy as np
import jax
import jax.numpy as jnp
from jax import lax
from jax.experimental import pallas as pl
from jax.experimental.pallas import tpu as pltpu

F32 = jnp.float32
BF16 = jnp.bfloat16

CHUNK = 64
N_META = 16
PAIR = 2 * CHUNK
GLA_HEADS = 4
GLA_DK = 64
GLA_DV = 128
GLA_KEY = GLA_HEADS * GLA_DK
GLA_VAL = GLA_HEADS * GLA_DV
GLA_RANK = 16
GLA_TAU = 16.0
SSD_INNER = 512
SSD_HEADS = 8
SSD_HEADDIM = 64
SSD_GROUPS = 2
SSD_STATE = 128
SSD_GROUP_W = SSD_INNER // SSD_GROUPS
SSD_CONV = 4
SSD_CONV_CH = SSD_INNER + 2 * SSD_GROUPS * SSD_STATE
TOP_K = 4
SWIGLU_LIMIT = 7.0
SWIGLU_ALPHA = 1.702
MOE_BLOCK = 512
LN_EPS = 1e-5
RMS_EPS = 1e-6
LANES = 128

C_Q, C_K, C_V, C_OG, C_Z, C_XBC, C_MISC, C_END = 0, 256, 512, 1024, 1536, 2048, 3072, 3200
MISC_A1 = 0
MISC_DT = 16
CONV_PAD = 8

MIXER_TILE = 256
MIXER_SEQS = 1
VMEM_LIMIT = 56 * 1024 * 1024


def _dot(a, b):
    return jnp.dot(a, b, preferred_element_type=F32)


def _split_hi_lo(x):
    hi = x.astype(BF16)
    lo = (x - hi.astype(F32)).astype(BF16)
    return hi, lo


def _log_sigmoid(x):
    return jnp.minimum(x, 0.0) - jnp.log1p(jnp.exp(-jnp.abs(x)))


def _softplus(x):
    return jnp.maximum(x, 0.0) + jnp.log1p(jnp.exp(-jnp.abs(x)))


def _silu(x):
    return x * jax.nn.sigmoid(x)


def _layer_norm(t, g, b):
    mu = jnp.mean(t, axis=-1, keepdims=True)
    tc = t - mu
    var = jnp.mean(tc * tc, axis=-1, keepdims=True)
    return tc * lax.rsqrt(var + LN_EPS) * g + b


def _group_rms(t, g, width):
    outs = []
    for s in range(0, t.shape[-1], width):
        seg = t[:, s:s + width]
        ms = jnp.mean(seg * seg, axis=-1, keepdims=True)
        outs.append(seg * lax.rsqrt(ms + RMS_EPS) * g[:, s:s + width])
    return jnp.concatenate(outs, axis=-1)


def _mixer_tile(h, tile, valid_col, valid_row, p, s_gla, s_ssd, xbc_buf, mix_buf, need_out):
    hb = h.astype(BF16)
    w_in = p["w_in"]

    def proj(lo, hi):
        r = _dot(hb, w_in[:, lo:hi])
        if valid_col is not None:
            r = jnp.where(valid_col, r, 0.0)
        return r

    misc = proj(C_MISC, C_END)

    la = _log_sigmoid(_dot(misc.astype(BF16), p["w_a2"][...]) + p["b_a"][...]) * (1.0 / GLA_TAU)
    if valid_col is not None:
        la = jnp.where(valid_col, la, 0.0)
    la_hi, la_lo = _split_hi_lo(la)
    r = _dot(p["uo"][...], jnp.concatenate([la_hi, la_lo], axis=1))
    dec_exp = r[0:tile, 0:GLA_KEY] + r[0:tile, GLA_KEY:]
    tot_rows = r[tile:, 0:GLA_KEY] + r[tile:, GLA_KEY:]
    gla_dec_t = jnp.exp(tot_rows.T)
    kd = proj(C_K, C_V) * jnp.exp(dec_exp)
    v_bf = proj(C_V, C_OG).astype(BF16)

    misc_t = misc.T
    dt = _softplus(misc_t[MISC_DT:MISC_DT + SSD_HEADS, :] + p["dt_bias"][...])
    if valid_row is not None:
        dt = jnp.where(valid_row, dt, 0.0)
    dta = dt * (-jnp.exp(p["a_log"][...]))
    d_hi, d_lo = _split_hi_lo(dta)
    r2 = _dot(jnp.concatenate([d_hi, d_lo], axis=0), p["uto"][...])
    w = jnp.exp(r2[0:8, 0:tile] + r2[8:16, 0:tile]) * dt
    ssd_dec = jnp.exp(r2[0:8, tile:] + r2[8:16, tile:])
    w_hi, w_lo = _split_hi_lo(w)
    c_hi, c_lo = _split_hi_lo(ssd_dec)
    packed = jnp.concatenate(
        [w_hi.astype(F32), w_lo.astype(F32), c_hi.astype(F32), c_lo.astype(F32),
         jnp.zeros((LANES - 4 * SSD_HEADS, tile), F32)], axis=0)
    expanded = _dot(packed.T.astype(BF16), p["eexp"][...])
    w_exp = expanded[:, 0:SSD_INNER]
    ssd_dec_exp = expanded[:, SSD_INNER:]

    xbc_buf[CONV_PAD:CONV_PAD + tile, :] = proj(C_XBC, C_MISC)
    acc = p["conv_b"][...] + p["conv_w"][0:1, :] * xbc_buf[pl.ds(CONV_PAD - 3, tile), :]
    for j in range(1, SSD_CONV):
        acc = acc + p["conv_w"][j:j + 1, :] * xbc_buf[pl.ds(CONV_PAD - 3 + j, tile), :]
    xbc_buf[0:CONV_PAD, :] = xbc_buf[tile:tile + CONV_PAD, :]
    xa = _silu(acc)
    xs = xa[:, 0:SSD_INNER]
    bm = xa[:, SSD_INNER:SSD_INNER + SSD_GROUPS * SSD_STATE]
    cm_bf = xa[:, SSD_INNER + SSD_GROUPS * SSD_STATE:].astype(BF16)
    xw_bf = (xs * w_exp).astype(BF16)

    if need_out:
        q = proj(C_Q, C_K) * (GLA_DK ** -0.5)
        head_of_lane = lax.broadcasted_iota(jnp.int32, (CHUNK, GLA_KEY), 1) // GLA_DK

    lane = lax.broadcasted_iota(jnp.int32, (1, PAIR), 1)
    for pr in range(tile // PAIR):
        rows = slice(pr * PAIR, (pr + 1) * PAIR)
        kd_t = kd[rows].T
        bm_t = bm[rows].T
        for half in range(2):
            c = 2 * pr + half
            r0 = c * CHUNK
            sel = (lane >= half * CHUNK) & (lane < (half + 1) * CHUNK)
            kd_m = jnp.where(sel, kd_t, 0.0).astype(BF16)
            upd = jnp.concatenate(
                [_dot(kd_m[hd * GLA_DK:(hd + 1) * GLA_DK], v_bf[rows, hd * GLA_DV:(hd + 1) * GLA_DV])
                 for hd in range(GLA_HEADS)], axis=0)
            s_new = gla_dec_t[:, c:c + 1] * s_gla[...] + upd
            s_gla[...] = s_new
            bm_m = jnp.where(sel, bm_t, 0.0).astype(BF16)
            ssd_new = []
            for g in range(SSD_GROUPS):
                gl = slice(g * SSD_GROUP_W, (g + 1) * SSD_GROUP_W)
                gr = slice(g * SSD_STATE, (g + 1) * SSD_STATE)
                u = _dot(bm_m[gr], xw_bf[rows, gl])
                sg = ssd_dec_exp[r0:r0 + 1, gl] * s_ssd[gr, :] + u
                s_ssd[gr, :] = sg
                ssd_new.append(sg)
            if need_out:
                qc = q[r0:r0 + CHUNK]
                qm = jnp.concatenate(
                    [jnp.where(head_of_lane == hd, qc, 0.0) for hd in range(GLA_HEADS)],
                    axis=0).astype(BF16)
                o = _dot(qm, s_new.astype(BF16))
                for hd in range(GLA_HEADS):
                    mix_buf[r0:r0 + CHUNK, hd * GLA_DV:(hd + 1) * GLA_DV] = o[hd * CHUNK:(hd + 1) * CHUNK]
                for g in range(SSD_GROUPS):
                    y = _dot(cm_bf[r0:r0 + CHUNK, g * SSD_STATE:(g + 1) * SSD_STATE],
                             ssd_new[g].astype(BF16))
                    mix_buf[r0:r0 + CHUNK, GLA_VAL + g * SSD_GROUP_W:GLA_VAL + (g + 1) * SSD_GROUP_W] = y

    if not need_out:
        return None
    o_gla = _group_rms(mix_buf[:, 0:GLA_VAL], p["gla_norm_g"][...], GLA_DV) * _silu(proj(C_OG, C_Z))
    y = (mix_buf[:, GLA_VAL:] + xs * p["ssd_d"][...]) * _silu(proj(C_Z, C_XBC))
    y = _group_rms(y, p["ssd_norm_g"][...], SSD_GROUP_W)
    return o_gla, y


_MIX_PARAMS = ("w_in", "w_a2", "b_a", "conv_w", "conv_b", "dt_bias", "a_log", "uo", "uto", "eexp")
_OUT_PARAMS = ("gla_norm_g", "ssd_d", "ssd_norm_g")


def _meta_kernel(x_ref, ln_g, ln_b, *rest):
    np_ = len(_MIX_PARAMS)
    p = dict(zip(_MIX_PARAMS, rest[:np_]))
    s_gla_out, s_ssd_out, tail_out, xbc_buf = rest[np_:]
    tile = x_ref.shape[0]
    s_gla_out[...] = jnp.zeros_like(s_gla_out)
    s_ssd_out[...] = jnp.zeros_like(s_ssd_out)
    xbc_buf[0:CONV_PAD, :] = jnp.zeros((CONV_PAD, xbc_buf.shape[1]), F32)
    first_valid = tile - N_META
    valid_col = lax.broadcasted_iota(jnp.int32, (tile, 1), 0) >= first_valid
    valid_row = lax.broadcasted_iota(jnp.int32, (1, tile), 1) >= first_valid
    h = _layer_norm(x_ref[...], ln_g[...], ln_b[...])
    _mixer_tile(h, tile, valid_col, valid_row, p, s_gla_out, s_ssd_out, xbc_buf, None, False)
    tail_out[...] = xbc_buf[0:CONV_PAD, :]


def _mixer_kernel(alpha, n_exp, x_ref, ln_g, ln_b, *rest):
    np_ = len(_MIX_PARAMS)
    p = dict(zip(_MIX_PARAMS, rest[:np_]))
    rest = rest[np_:]
    p.update(zip(_OUT_PARAMS, rest[:3]))
    (w_out, ln1_g, ln1_b, rw_hi, rw_lo, rb, su, s_gla0, s_ssd0, tail0,
     h1_out, topi_out, gate_out, rank_out, cnt_out,
     s_gla, s_ssd, xbc_buf, mix_buf, run_cnt) = rest[3:]
    n_sub, tile = x_ref.shape[0], x_ref.shape[1]
    b, j = pl.program_id(0), pl.program_id(1)

    @pl.when(j == 0)
    def _():
        for s in range(n_sub):
            s_gla[s] = s_gla0[...]
            s_ssd[s] = s_ssd0[...]
            xbc_buf[s, 0:CONV_PAD, :] = tail0[...]

    @pl.when((b == 0) & (j == 0))
    def _():
        run_cnt[...] = jnp.zeros_like(run_cnt)

    count = run_cnt[...]
    pad_rows = topi_out.shape[1] - TOP_K
    for s in range(n_sub):
        h = _layer_norm(x_ref[s], ln_g[...], ln_b[...])
        o_gla, y = _mixer_tile(h, tile, None, None, p, s_gla.at[s], s_ssd.at[s], xbc_buf.at[s],
                               mix_buf.at[s], True)
        mix = _dot(o_gla.astype(BF16), w_out[0:GLA_VAL, :]) + _dot(y.astype(BF16), w_out[GLA_VAL:, :])
        h1 = _layer_norm(alpha * h + mix, ln1_g[...], ln1_b[...])
        _store_rows(h1_out.at[s], h1)

        h_hi, h_lo = _split_hi_lo(h1)
        logits = _dot(h_hi, rw_hi[...]) + (_dot(h_hi, rw_lo[...]) + _dot(h_lo, rw_hi[...])) + rb[...]
        lt = logits.T[0:n_exp, :]
        e_iota = lax.broadcasted_iota(jnp.int32, (n_exp, tile), 0).astype(F32)
        work = lt
        vals, hots = [], []
        for k in range(TOP_K):
            m = jnp.max(work, axis=0, keepdims=True)
            idx = jnp.min(jnp.where(work == m, e_iota, float(n_exp)), axis=0, keepdims=True)
            hot = e_iota == idx
            work = jnp.where(hot, -jnp.inf, work)
            vals.append(m)
            hots.append(hot)
            topi_out[s, k:k + 1, :] = idx.astype(jnp.int32)
        exps = [jnp.exp(vk - vals[0]) for vk in vals]
        denom = exps[0] + exps[1] + exps[2] + exps[3]
        for k in range(TOP_K):
            gate_out[s, k:k + 1, :] = exps[k] / denom
        member = (hots[0] | hots[1] | hots[2] | hots[3]).astype(F32).astype(BF16)
        before = _dot(member, su[...]) + count[:, 0:1]
        for k in range(TOP_K):
            rk = jnp.sum(jnp.where(hots[k], before, 0.0), axis=0, keepdims=True)
            rank_out[s, k:k + 1, :] = rk.astype(jnp.int32)
        topi_out[s, TOP_K:, :] = jnp.zeros((pad_rows, tile), jnp.int32)
        gate_out[s, TOP_K:, :] = jnp.zeros((pad_rows, tile), F32)
        rank_out[s, TOP_K:, :] = jnp.zeros((pad_rows, tile), jnp.int32)
        count = count + _dot(member, jnp.ones((tile, LANES), BF16))
    run_cnt[...] = count
    cnt_out[...] = count


def _clamped_swiglu(a, u):
    a = jnp.minimum(a, SWIGLU_LIMIT)
    u = jnp.clip(u, -SWIGLU_LIMIT, SWIGLU_LIMIT)
    return a * jax.nn.sigmoid(SWIGLU_ALPHA * a) * (u + 1.0)


SUB = 8


def _load_rows(ref, n_rows):
    return jnp.concatenate([ref[pl.ds(j, n_rows, stride=SUB), :] for j in range(SUB)], axis=1)


def _store_rows(ref, val):
    for j in range(SUB):
        ref[pl.ds(j, val.shape[0], stride=SUB), :] = val[:, j * LANES:(j + 1) * LANES]


def _tile_rows(row):
    return pl.ds(pl.multiple_of(row * SUB, SUB), SUB)


def _dispatch_kernel(fill_lo, fill_hi, dest, h_hbm, xs_hbm, tbuf, zrow, lsem, rsem, zsem):
    i = pl.program_id(0)
    n = pl.num_programs(0)
    tile = dest.shape[-1] // TOP_K
    groups = tile // SUB
    slot = i % 2
    n_exp = fill_lo.shape[0]

    def tile_load(j, s):
        return pltpu.make_async_copy(h_hbm.at[pl.ds(j * tile * SUB, tile * SUB), :], tbuf.at[s], lsem.at[s])

    def row_copy(s, t, dst_row):
        return pltpu.make_async_copy(tbuf.at[s, _tile_rows(t), :], xs_hbm.at[_tile_rows(dst_row), :], rsem.at[s])

    def zero_copy(dst_row):
        return pltpu.make_async_copy(zrow, xs_hbm.at[_tile_rows(dst_row), :], zsem.at[0])

    def wait_rows(s):
        def body(g, _):
            for r in range(TOP_K * SUB):
                row_copy(s, g * SUB + r % SUB, 0).wait()
            return 0
        lax.fori_loop(0, groups, body, 0)

    @pl.when(i == 0)
    def _():
        tile_load(0, 0).start()
        zrow[...] = jnp.zeros_like(zrow)
        for wait in (False, True):
            def per_expert(e, _, wait=wait):
                def body(row, _):
                    if wait:
                        zero_copy(0).wait()
                    else:
                        zero_copy(row).start()
                    return 0
                lax.fori_loop(fill_lo[e], fill_hi[e], body, 0)
                return 0
            lax.fori_loop(0, n_exp, per_expert, 0)

    @pl.when(i > 0)
    def _():
        wait_rows(1 - slot)

    @pl.when(i + 1 < n)
    def _():
        tile_load(i + 1, 1 - slot).start()

    tile_load(i, slot).wait()

    def issue(g, _):
        for k in range(TOP_K):
            for r in range(SUB):
                t = g * SUB + r
                row_copy(slot, t, dest[0, 0, k * tile + t]).start(priority=r % 2)
        return 0
    lax.fori_loop(0, groups, issue, 0)

    @pl.when(i == n - 1)
    def _():
        wait_rows(slot)


def _experts_kernel(blk_e, n_act, x_ref, wg, wu, bg, bu, wd, bd, y_out, wg_bf, wu_bf, wd_bf):
    i = pl.program_id(0)
    active = i < n_act[0]
    new_expert = (i == 0) | (blk_e[i] != blk_e[jnp.maximum(i - 1, 0)])

    @pl.when(active & new_expert)
    def _():
        wg_bf[...] = wg[...].astype(BF16)
        wu_bf[...] = wu[...].astype(BF16)
        wd_bf[...] = wd[...].astype(BF16)

    @pl.when(active)
    def _():
        xb = _load_rows(x_ref, x_ref.shape[0] // SUB).astype(BF16)
        hid = _clamped_swiglu(_dot(xb, wg_bf[...]) + bg[...], _dot(xb, wu_bf[...]) + bu[...])
        _store_rows(y_out, _dot(hid.astype(BF16), wd_bf[...]) + bd[...])

    @pl.when(i >= n_act[0])
    def _():
        y_out[...] = jnp.zeros_like(y_out)


def _combine_kernel(alpha, dest_cur, dest_nxt, gates, h1, ln_g, ln_b, y_hbm, out, ybuf, sem):
    i = pl.program_id(0)
    n = pl.num_programs(0)
    tile = out.shape[0]
    groups = tile // SUB
    slot = i % 2

    def row_copy(s, k, t, src_row):
        return pltpu.make_async_copy(y_hbm.at[_tile_rows(src_row), :], ybuf.at[s, k, _tile_rows(t), :], sem.at[s])

    def start_rows(dest_ref, s):
        def body(g, _):
            for k in range(TOP_K):
                for r in range(SUB):
                    t = g * SUB + r
                    row_copy(s, k, t, dest_ref[0, 0, k * tile + t]).start(priority=r % 2)
            return 0
        lax.fori_loop(0, groups, body, 0)

    @pl.when(i == 0)
    def _():
        start_rows(dest_cur, 0)

    @pl.when(i + 1 < n)
    def _():
        start_rows(dest_nxt, 1 - slot)

    def wait_body(g, _):
        for k in range(TOP_K):
            for r in range(SUB):
                row_copy(slot, k, g * SUB + r, 0).wait()
        return 0
    lax.fori_loop(0, groups, wait_body, 0)

    g = gates[...]
    ff = g[:, 0:1] * _load_rows(ybuf.at[slot, 0], tile)
    for k in range(1, TOP_K):
        ff = ff + g[:, k:k + 1] * _load_rows(ybuf.at[slot, k], tile)
    out[...] = _layer_norm(alpha * _load_rows(h1, tile) + ff, ln_g[...], ln_b[...])


def _block_constants(tile):
    t = np.arange(tile)
    same = (t[:, None] // CHUNK) == (t[None, :] // CHUNK)
    later = same & (t[None, :] > t[:, None])
    chunk_rows = np.zeros((LANES, tile), np.float32)
    chunk_rows[t // CHUNK, t] = 1.0
    uo = np.concatenate([later.astype(np.float32), chunk_rows], axis=0)
    uto = np.concatenate([later.T.astype(np.float32), same.astype(np.float32)], axis=1)
    return jnp.asarray(uo, BF16), jnp.asarray(uto, BF16)


def _expand_constant():
    e = np.zeros((LANES, 2 * SSD_INNER), np.float32)
    for part in range(4):
        for hd in range(SSD_HEADS):
            base = (part // 2) * SSD_INNER + hd * SSD_HEADDIM
            e[part * SSD_HEADS + hd, base:base + SSD_HEADDIM] = 1.0
    return jnp.asarray(e, BF16)


def _full(shape):
    return pl.BlockSpec(shape, lambda *_: (0,) * len(shape))


def kernel(x, meta_tokens, ln_in_g, ln_in_b, w_in, gla_w_a2, gla_b_a, gla_norm_g, ssd_conv_w, ssd_conv_b,
           ssd_dt_bias, ssd_a_log, ssd_d, ssd_norm_g, w_out, ln1_g, ln1_b, router_w, router_b, moe_w_gate,
           moe_w_up, moe_b_gate, moe_b_up, moe_w_down, moe_b_down, ln2_g, ln2_b):
    batch, seq, d = x.shape
    depth = w_in.shape[0]
    assert depth == 1, "single-layer stack"
    n_exp = router_w.shape[-1]
    d_ff = moe_w_gate.shape[-1]
    alpha = (2.0 * depth) ** 0.25
    tile = MIXER_TILE
    assert seq % tile == 0 and d == 1024
    n_tok = batch * seq
    row = lambda a: a.reshape(1, -1).astype(F32)

    wi = w_in[0]
    o_a1 = 1536
    o_z = o_a1 + GLA_RANK
    o_xbc = o_z + SSD_INNER
    o_dt = o_xbc + SSD_CONV_CH
    misc_w = jnp.zeros((d, LANES), F32)
    misc_w = misc_w.at[:, MISC_A1:MISC_A1 + GLA_RANK].set(wi[:, o_a1:o_z])
    misc_w = misc_w.at[:, MISC_DT:MISC_DT + SSD_HEADS].set(wi[:, o_dt:o_dt + SSD_HEADS])
    w_in_r = jnp.concatenate([wi[:, 0:o_a1], wi[:, o_z:o_xbc], wi[:, o_xbc:o_dt], misc_w], axis=1).astype(BF16)
    w_a2 = jnp.zeros((LANES, GLA_KEY), F32).at[MISC_A1:MISC_A1 + GLA_RANK].set(gla_w_a2[0]).astype(BF16)
    mix_params = dict(
        w_in=w_in_r, w_a2=w_a2, b_a=row(gla_b_a[0]), conv_w=ssd_conv_w[0].astype(F32),
        conv_b=row(ssd_conv_b[0]), dt_bias=ssd_dt_bias[0].reshape(-1, 1).astype(F32),
        a_log=ssd_a_log[0].reshape(-1, 1).astype(F32), eexp=_expand_constant())
    out_params = dict(gla_norm_g=row(gla_norm_g[0]), ssd_d=row(jnp.repeat(ssd_d[0], SSD_HEADDIM)),
                      ssd_norm_g=row(ssd_norm_g[0]))

    def mix_args(t):
        uo, uto = _block_constants(t)
        vals = dict(mix_params, uo=uo, uto=uto)
        return [vals[k] for k in _MIX_PARAMS]

    m_tile = PAIR
    x_meta = jnp.concatenate([jnp.zeros((m_tile - N_META, d), F32), meta_tokens.astype(F32)], axis=0)
    meta_in = [x_meta, row(ln_in_g), row(ln_in_b)] + mix_args(m_tile)
    s_gla0, s_ssd0, tail0 = pl.pallas_call(
        _meta_kernel,
        out_shape=(jax.ShapeDtypeStruct((GLA_KEY, GLA_DV), F32),
                   jax.ShapeDtypeStruct((SSD_GROUPS * SSD_STATE, SSD_GROUP_W), F32),
                   jax.ShapeDtypeStruct((CONV_PAD, SSD_CONV_CH), F32)),
        scratch_shapes=[pltpu.VMEM((CONV_PAD + m_tile, SSD_CONV_CH), F32)],
        compiler_params=pltpu.CompilerParams(vmem_limit_bytes=VMEM_LIMIT),
        name="meta_state",
    )(*meta_in)

    n_j = seq // tile
    rw = jnp.zeros((d, LANES), F32).at[:, 0:n_exp].set(router_w[0])
    rw_hi, rw_lo = _split_hi_lo(rw)
    rb = jnp.zeros((1, LANES), F32).at[:, 0:n_exp].set(router_b[0][None])
    su = jnp.asarray(np.triu(np.ones((tile, tile), np.float32), 1), BF16)
    args = ([x, row(ln_in_g), row(ln_in_b)] + mix_args(tile) + [out_params[k] for k in _OUT_PARAMS]
            + [w_out[0].astype(BF16), row(ln1_g[0]), row(ln1_b[0]), rw_hi, rw_lo, rb, su, s_gla0, s_ssd0, tail0])
    n_sub = MIXER_SEQS
    assert batch % n_sub == 0
    in_specs = [pl.BlockSpec((n_sub, tile, d), lambda b, j: (b, j, 0))] + [_full(a.shape) for a in args[1:]]
    tok_blk = pl.BlockSpec((n_sub, 8, tile), lambda b, j: (b, 0, j))
    h1, topi, gates, rank, cnt = pl.pallas_call(
        functools.partial(_mixer_kernel, alpha, n_exp),
        grid=(batch // n_sub, n_j),
        in_specs=in_specs,
        out_specs=(pl.BlockSpec((n_sub, tile * SUB, LANES), lambda b, j: (b, j, 0)),
                   tok_blk, tok_blk, tok_blk, _full((n_exp, LANES))),
        out_shape=(jax.ShapeDtypeStruct((batch, seq * SUB, LANES), F32),
                   jax.ShapeDtypeStruct((batch, 8, seq), jnp.int32),
                   jax.ShapeDtypeStruct((batch, 8, seq), F32),
                   jax.ShapeDtypeStruct((batch, 8, seq), jnp.int32),
                   jax.ShapeDtypeStruct((n_exp, LANES), F32)),
        scratch_shapes=[pltpu.VMEM((n_sub, GLA_KEY, GLA_DV), F32),
                        pltpu.VMEM((n_sub, SSD_GROUPS * SSD_STATE, SSD_GROUP_W), F32),
                        pltpu.VMEM((n_sub, CONV_PAD + tile, SSD_CONV_CH), F32),
                        pltpu.VMEM((n_sub, tile, d), F32),
                        pltpu.VMEM((n_exp, LANES), F32)],
        compiler_params=pltpu.CompilerParams(dimension_semantics=("arbitrary", "arbitrary"),
                                             vmem_limit_bytes=VMEM_LIMIT),
        name="mixer",
    )(*args)
    h1 = h1.reshape(n_tok * SUB, LANES)
    per_tok = lambda a: a[:, :TOP_K].transpose(1, 0, 2).reshape(TOP_K, n_tok)
    top_e, gates, rank = per_tok(topi), per_tok(gates), per_tok(rank)

    counts = cnt[:, 0].astype(jnp.int32)
    padded = (counts + MOE_BLOCK - 1) // MOE_BLOCK * MOE_BLOCK
    pad_end = jnp.cumsum(padded)
    pad_start = pad_end - padded
    e_ids = jnp.arange(n_exp, dtype=jnp.int32)[:, None, None]
    dest = rank + jnp.sum(jnp.where(top_e[None] == e_ids, pad_start[:, None, None], 0), axis=0)
    n_blocks = -(-(n_tok * TOP_K) // MOE_BLOCK) + n_exp
    n_rows = n_blocks * MOE_BLOCK
    blk_lo = jnp.arange(n_blocks, dtype=jnp.int32) * MOE_BLOCK
    block_e = jnp.minimum(jnp.sum((blk_lo[:, None] >= pad_end[None, :]).astype(jnp.int32), axis=1), n_exp - 1)
    n_act = (pad_end[-1] // MOE_BLOCK).astype(jnp.int32).reshape(1)

    c_tile = 256
    n_ct = n_tok // c_tile
    dest3 = dest.reshape(TOP_K, n_ct, c_tile).transpose(1, 0, 2).reshape(n_ct, 1, TOP_K * c_tile)
    dest_blk = lambda f: pl.BlockSpec((1, 1, TOP_K * c_tile), f, memory_space=pltpu.SMEM)
    x_sorted = pl.pallas_call(
        _dispatch_kernel,
        grid_spec=pltpu.PrefetchScalarGridSpec(
            num_scalar_prefetch=2,
            grid=(n_ct,),
            in_specs=[dest_blk(lambda i, lo, hi: (i, 0, 0)), pl.BlockSpec(memory_space=pl.ANY)],
            out_specs=pl.BlockSpec(memory_space=pl.ANY),
            scratch_shapes=[pltpu.VMEM((2, c_tile * SUB, LANES), F32), pltpu.VMEM((SUB, LANES), F32),
                            pltpu.SemaphoreType.DMA((2,)), pltpu.SemaphoreType.DMA((2,)),
                            pltpu.SemaphoreType.DMA((1,))]),
        out_shape=jax.ShapeDtypeStruct((n_rows * SUB, LANES), F32),
        compiler_params=pltpu.CompilerParams(dimension_semantics=("arbitrary",), vmem_limit_bytes=VMEM_LIMIT),
        name="moe_dispatch",
    )(pad_start + counts, pad_end, dest3, h1)

    last_act = lambda i, na: jnp.minimum(i, na[0] - 1)
    e_mat = lambda shape: pl.BlockSpec((None,) + shape, lambda i, be, na: (be[i], 0, 0))
    y_sorted = pl.pallas_call(
        _experts_kernel,
        grid_spec=pltpu.PrefetchScalarGridSpec(
            num_scalar_prefetch=2,
            grid=(n_blocks,),
            in_specs=[pl.BlockSpec((MOE_BLOCK * SUB, LANES), lambda i, be, na: (last_act(i, na), 0)),
                      e_mat((d, d_ff)), e_mat((d, d_ff)), e_mat((1, d_ff)), e_mat((1, d_ff)),
                      e_mat((d_ff, d)), e_mat((1, d))],
            out_specs=pl.BlockSpec((MOE_BLOCK * SUB, LANES), lambda i, be, na: (i, 0)),
            scratch_shapes=[pltpu.VMEM((d, d_ff), BF16), pltpu.VMEM((d, d_ff), BF16),
                            pltpu.VMEM((d_ff, d), BF16)]),
        out_shape=jax.ShapeDtypeStruct((n_rows * SUB, LANES), F32),
        compiler_params=pltpu.CompilerParams(dimension_semantics=("arbitrary",), vmem_limit_bytes=VMEM_LIMIT),
        name="moe_experts",
    )(block_e, n_act, x_sorted,
      moe_w_gate[0].astype(F32), moe_w_up[0].astype(F32),
      moe_b_gate[0].reshape(n_exp, 1, d_ff).astype(F32), moe_b_up[0].reshape(n_exp, 1, d_ff).astype(F32),
      moe_w_down[0].astype(F32), moe_b_down[0].reshape(n_exp, 1, d).astype(F32))

    out = pl.pallas_call(
        functools.partial(_combine_kernel, alpha),
        grid=(n_ct,),
        in_specs=[dest_blk(lambda i: (i, 0, 0)),
                  dest_blk(lambda i: (jnp.minimum(i + 1, n_ct - 1), 0, 0)),
                  pl.BlockSpec((c_tile, TOP_K), lambda i: (i, 0)),
                  pl.BlockSpec((c_tile * SUB, LANES), lambda i: (i, 0)),
                  _full((1, d)), _full((1, d)),
                  pl.BlockSpec(memory_space=pl.ANY)],
        out_specs=pl.BlockSpec((c_tile, d), lambda i: (i, 0)),
        out_shape=jax.ShapeDtypeStruct((n_tok, d), F32),
        scratch_shapes=[pltpu.VMEM((2, TOP_K, c_tile * SUB, LANES), F32), pltpu.SemaphoreType.DMA((2,))],
        compiler_params=pltpu.CompilerParams(dimension_semantics=("arbitrary",), vmem_limit_bytes=VMEM_LIMIT),
        name="moe_combine",
    )(dest3, dest3, gates.T, h1, row(ln2_g[0]), row(ln2_b[0]), y_sorted)
    return out.reshape(batch, seq, d).astype(x.dtype)
```

```python
import functools

import numpy as np
import jax
import jax.numpy as jnp
from jax import lax
from jax.experimental import pallas as pl
from jax.experimental.pallas import tpu as pltpu

F32 = jnp.float32
BF16 = jnp.bfloat16

CHUNK = 64
N_META = 16
PAIR = 2 * CHUNK
GLA_HEADS = 4
GLA_DK = 64
GLA_DV = 128
GLA_KEY = GLA_HEADS * GLA_DK
GLA_VAL = GLA_HEADS * GLA_DV
GLA_RANK = 16
GLA_TAU = 16.0
SSD_INNER = 512
SSD_HEADS = 8
SSD_HEADDIM = 64
SSD_GROUPS = 2
SSD_STATE = 128
SSD_GROUP_W = SSD_INNER // SSD_GROUPS
SSD_CONV = 4
SSD_CONV_CH = SSD_INNER + 2 * SSD_GROUPS * SSD_STATE
TOP_K = 4
SWIGLU_LIMIT = 7.0
SWIGLU_ALPHA = 1.702
MOE_BLOCK = 512
LN_EPS = 1e-5
RMS_EPS = 1e-6
LANES = 128

C_Q, C_K, C_V, C_OG, C_Z, C_XBC, C_MISC, C_END = 0, 256, 512, 1024, 1536, 2048, 3072, 3200
MISC_A1 = 0
MISC_DT = 16
CONV_PAD = 8

MIXER_TILE = 256
MIXER_SEQS = 1
VMEM_LIMIT = 56 * 1024 * 1024


def _dot(a, b):
    return jnp.dot(a, b, preferred_element_type=F32)


def _split_hi_lo(x):
    hi = x.astype(BF16)
    lo = (x - hi.astype(F32)).astype(BF16)
    return hi, lo


def _log_sigmoid(x):
    return jnp.minimum(x, 0.0) - jnp.log1p(jnp.exp(-jnp.abs(x)))


def _softplus(x):
    return jnp.maximum(x, 0.0) + jnp.log1p(jnp.exp(-jnp.abs(x)))


def _silu(x):
    return x * jax.nn.sigmoid(x)


def _layer_norm(t, g, b):
    mu = jnp.mean(t, axis=-1, keepdims=True)
    tc = t - mu
    var = jnp.mean(tc * tc, axis=-1, keepdims=True)
    return tc * lax.rsqrt(var + LN_EPS) * g + b


def _group_rms(t, g, width):
    outs = []
    for s in range(0, t.shape[-1], width):
        seg = t[:, s:s + width]
        ms = jnp.mean(seg * seg, axis=-1, keepdims=True)
        outs.append(seg * lax.rsqrt(ms + RMS_EPS) * g[:, s:s + width])
    return jnp.concatenate(outs, axis=-1)


def _mixer_tile(h, tile, valid_col, valid_row, p, s_gla, s_ssd, xbc_buf, mix_buf, need_out):
    hb = h.astype(BF16)
    w_in = p["w_in"]

    def proj(lo, hi):
        r = _dot(hb, w_in[:, lo:hi])
        if valid_col is not None:
            r = jnp.where(valid_col, r, 0.0)
        return r

    misc = proj(C_MISC, C_END)

    la = _log_sigmoid(_dot(misc.astype(BF16), p["w_a2"][...]) + p["b_a"][...]) * (1.0 / GLA_TAU)
    if valid_col is not None:
        la = jnp.where(valid_col, la, 0.0)
    la_hi, la_lo = _split_hi_lo(la)
    r = _dot(p["uo"][...], jnp.concatenate([la_hi, la_lo], axis=1))
    dec_exp = r[0:tile, 0:GLA_KEY] + r[0:tile, GLA_KEY:]
    tot_rows = r[tile:, 0:GLA_KEY] + r[tile:, GLA_KEY:]
    gla_dec_t = jnp.exp(tot_rows.T)
    kd = proj(C_K, C_V) * jnp.exp(dec_exp)
    v_bf = proj(C_V, C_OG).astype(BF16)

    misc_t = misc.T
    dt = _softplus(misc_t[MISC_DT:MISC_DT + SSD_HEADS, :] + p["dt_bias"][...])
    if valid_row is not None:
        dt = jnp.where(valid_row, dt, 0.0)
    dta = dt * (-jnp.exp(p["a_log"][...]))
    d_hi, d_lo = _split_hi_lo(dta)
    r2 = _dot(jnp.concatenate([d_hi, d_lo], axis=0), p["uto"][...])
    w = jnp.exp(r2[0:8, 0:tile] + r2[8:16, 0:tile]) * dt
    ssd_dec = jnp.exp(r2[0:8, tile:] + r2[8:16, tile:])
    w_hi, w_lo = _split_hi_lo(w)
    c_hi, c_lo = _split_hi_lo(ssd_dec)
    packed = jnp.concatenate(
        [w_hi.astype(F32), w_lo.astype(F32), c_hi.astype(F32), c_lo.astype(F32),
         jnp.zeros((LANES - 4 * SSD_HEADS, tile), F32)], axis=0)
    expanded = _dot(packed.T.astype(BF16), p["eexp"][...])
    w_exp = expanded[:, 0:SSD_INNER]
    ssd_dec_exp = expanded[:, SSD_INNER:]

    xbc_buf[CONV_PAD:CONV_PAD + tile, :] = proj(C_XBC, C_MISC)
    acc = p["conv_b"][...] + p["conv_w"][0:1, :] * xbc_buf[pl.ds(CONV_PAD - 3, tile), :]
    for j in range(1, SSD_CONV):
        acc = acc + p["conv_w"][j:j + 1, :] * xbc_buf[pl.ds(CONV_PAD - 3 + j, tile), :]
    xbc_buf[0:CONV_PAD, :] = xbc_buf[tile:tile + CONV_PAD, :]
    xa = _silu(acc)
    xs = xa[:, 0:SSD_INNER]
    bm = xa[:, SSD_INNER:SSD_INNER + SSD_GROUPS * SSD_STATE]
    cm_bf = xa[:, SSD_INNER + SSD_GROUPS * SSD_STATE:].astype(BF16)
    xw_bf = (xs * w_exp).astype(BF16)

    if need_out:
        q = proj(C_Q, C_K) * (GLA_DK ** -0.5)
        head_of_lane = lax.broadcasted_iota(jnp.int32, (CHUNK, GLA_KEY), 1) // GLA_DK

    lane = lax.broadcasted_iota(jnp.int32, (1, PAIR), 1)
    for pr in range(tile // PAIR):
        rows = slice(pr * PAIR, (pr + 1) * PAIR)
        kd_t = kd[rows].T
        bm_t = bm[rows].T
        for half in range(2):
            c = 2 * pr + half
            r0 = c * CHUNK
            sel = (lane >= half * CHUNK) & (lane < (half + 1) * CHUNK)
            kd_m = jnp.where(sel, kd_t, 0.0).astype(BF16)
            upd = jnp.concatenate(
                [_dot(kd_m[hd * GLA_DK:(hd + 1) * GLA_DK], v_bf[rows, hd * GLA_DV:(hd + 1) * GLA_DV])
                 for hd in range(GLA_HEADS)], axis=0)
            s_new = gla_dec_t[:, c:c + 1] * s_gla[...] + upd
            s_gla[...] = s_new
            bm_m = jnp.where(sel, bm_t, 0.0).astype(BF16)
            ssd_new = []
            for g in range(SSD_GROUPS):
                gl = slice(g * SSD_GROUP_W, (g + 1) * SSD_GROUP_W)
                gr = slice(g * SSD_STATE, (g + 1) * SSD_STATE)
                u = _dot(bm_m[gr], xw_bf[rows, gl])
                sg = ssd_dec_exp[r0:r0 + 1, gl] * s_ssd[gr, :] + u
                s_ssd[gr, :] = sg
                ssd_new.append(sg)
            if need_out:
                qc = q[r0:r0 + CHUNK]
                qm = jnp.concatenate(
                    [jnp.where(head_of_lane == hd, qc, 0.0) for hd in range(GLA_HEADS)],
                    axis=0).astype(BF16)
                o = _dot(qm, s_new.astype(BF16))
                for hd in range(GLA_HEADS):
                    mix_buf[r0:r0 + CHUNK, hd * GLA_DV:(hd + 1) * GLA_DV] = o[hd * CHUNK:(hd + 1) * CHUNK]
                for g in range(SSD_GROUPS):
                    y = _dot(cm_bf[r0:r0 + CHUNK, g * SSD_STATE:(g + 1) * SSD_STATE],
                             ssd_new[g].astype(BF16))
                    mix_buf[r0:r0 + CHUNK, GLA_VAL + g * SSD_GROUP_W:GLA_VAL + (g + 1) * SSD_GROUP_W] = y

    if not need_out:
        return None
    o_gla = _group_rms(mix_buf[:, 0:GLA_VAL], p["gla_norm_g"][...], GLA_DV) * _silu(proj(C_OG, C_Z))
    y = (mix_buf[:, GLA_VAL:] + xs * p["ssd_d"][...]) * _silu(proj(C_Z, C_XBC))
    y = _group_rms(y, p["ssd_norm_g"][...], SSD_GROUP_W)
    return o_gla, y


_MIX_PARAMS = ("w_in", "w_a2", "b_a", "conv_w", "conv_b", "dt_bias", "a_log", "uo", "uto", "eexp")
_OUT_PARAMS = ("gla_norm_g", "ssd_d", "ssd_norm_g")


def _meta_kernel(x_ref, ln_g, ln_b, *rest):
    np_ = len(_MIX_PARAMS)
    p = dict(zip(_MIX_PARAMS, rest[:np_]))
    s_gla_out, s_ssd_out, tail_out, xbc_buf = rest[np_:]
    tile = x_ref.shape[0]
    s_gla_out[...] = jnp.zeros_like(s_gla_out)
    s_ssd_out[...] = jnp.zeros_like(s_ssd_out)
    xbc_buf[0:CONV_PAD, :] = jnp.zeros((CONV_PAD, xbc_buf.shape[1]), F32)
    first_valid = tile - N_META
    valid_col = lax.broadcasted_iota(jnp.int32, (tile, 1), 0) >= first_valid
    valid_row = lax.broadcasted_iota(jnp.int32, (1, tile), 1) >= first_valid
    h = _layer_norm(x_ref[...], ln_g[...], ln_b[...])
    _mixer_tile(h, tile, valid_col, valid_row, p, s_gla_out, s_ssd_out, xbc_buf, None, False)
    tail_out[...] = xbc_buf[0:CONV_PAD, :]


def _mixer_kernel(alpha, n_exp, x_ref, ln_g, ln_b, *rest):
    np_ = len(_MIX_PARAMS)
    p = dict(zip(_MIX_PARAMS, rest[:np_]))
    rest = rest[np_:]
    p.update(zip(_OUT_PARAMS, rest[:3]))
    (w_out, ln1_g, ln1_b, rw_hi, rw_lo, rb, su, s_gla0, s_ssd0, tail0,
     h1_out, topi_out, gate_out, rank_out, cnt_out, base_out, tcnt_out,
     s_gla, s_ssd, xbc_buf, mix_buf, run_cnt) = rest[3:]
    n_sub, tile = x_ref.shape[0], x_ref.shape[1]
    b, j = pl.program_id(0), pl.program_id(1)

    @pl.when(j == 0)
    def _():
        for s in range(n_sub):
            s_gla[s] = s_gla0[...]
            s_ssd[s] = s_ssd0[...]
            xbc_buf[s, 0:CONV_PAD, :] = tail0[...]

    @pl.when((b == 0) & (j == 0))
    def _():
        run_cnt[...] = jnp.zeros_like(run_cnt)

    count = run_cnt[...]
    pad_rows = topi_out.shape[1] - TOP_K
    for s in range(n_sub):
        h = _layer_norm(x_ref[s], ln_g[...], ln_b[...])
        o_gla, y = _mixer_tile(h, tile, None, None, p, s_gla.at[s], s_ssd.at[s], xbc_buf.at[s],
                               mix_buf.at[s], True)
        mix = _dot(o_gla.astype(BF16), w_out[0:GLA_VAL, :]) + _dot(y.astype(BF16), w_out[GLA_VAL:, :])
        h1 = _layer_norm(alpha * h + mix, ln1_g[...], ln1_b[...])
        _store_rows(h1_out.at[s], h1)

        h_hi, h_lo = _split_hi_lo(h1)
        logits = _dot(h_hi, rw_hi[...]) + (_dot(h_hi, rw_lo[...]) + _dot(h_lo, rw_hi[...])) + rb[...]
        lt = logits.T[0:n_exp, :]
        e_iota = lax.broadcasted_iota(jnp.int32, (n_exp, tile), 0).astype(F32)
        work = lt
        vals, hots = [], []
        for k in range(TOP_K):
            m = jnp.max(work, axis=0, keepdims=True)
            idx = jnp.min(jnp.where(work == m, e_iota, float(n_exp)), axis=0, keepdims=True)
            hot = e_iota == idx
            work = jnp.where(hot, -jnp.inf, work)
            vals.append(m)
            hots.append(hot)
            topi_out[s, k:k + 1, :] = idx.astype(jnp.int32)
        exps = [jnp.exp(vk - vals[0]) for vk in vals]
        denom = exps[0] + exps[1] + exps[2] + exps[3]
        for k in range(TOP_K):
            gate_out[s, k:k + 1, :] = exps[k] / denom
        member = (hots[0] | hots[1] | hots[2] | hots[3]).astype(F32).astype(BF16)
        before = _dot(member, su[...]) + count[:, 0:1]
        for k in range(TOP_K):
            rk = jnp.sum(jnp.where(hots[k], before, 0.0), axis=0, keepdims=True)
            rank_out[s, k:k + 1, :] = rk.astype(jnp.int32)
        topi_out[s, TOP_K:, :] = jnp.zeros((pad_rows, tile), jnp.int32)
        gate_out[s, TOP_K:, :] = jnp.zeros((pad_rows, tile), F32)
        rank_out[s, TOP_K:, :] = jnp.zeros((pad_rows, tile), jnp.int32)
        tile_count = _dot(member, jnp.ones((tile, LANES), BF16))
        base_out[s, 0] = count
        tcnt_out[s, 0] = tile_count
        count = count + tile_count
    run_cnt[...] = count
    cnt_out[...] = count


def _clamped_swiglu(a, u):
    a = jnp.minimum(a, SWIGLU_LIMIT)
    u = jnp.clip(u, -SWIGLU_LIMIT, SWIGLU_LIMIT)
    return a * jax.nn.sigmoid(SWIGLU_ALPHA * a) * (u + 1.0)


SUB = 8


def _load_rows(ref, n_rows):
    return jnp.concatenate([ref[pl.ds(j, n_rows, stride=SUB), :] for j in range(SUB)], axis=1)


def _store_rows(ref, val):
    for j in range(SUB):
        ref[pl.ds(j, val.shape[0], stride=SUB), :] = val[:, j * LANES:(j + 1) * LANES]


def _tile_rows(row):
    return pl.ds(pl.multiple_of(row * SUB, SUB), SUB)


RUN_ROWS = 8
PIECE_GROUP = 4


def _run_rows(row):
    return pl.ds(pl.multiple_of(row * SUB, SUB), RUN_ROWS * SUB)


def _stage_tile(addr):
    return pl.ds(pl.multiple_of(addr, SUB), SUB)


def _for_pieces(pieces, fn):
    def body(g, _):
        for u in range(PIECE_GROUP):
            p = g * PIECE_GROUP + u
            fn(p * RUN_ROWS, pieces[0, 0, 1 + p])
        return 0
    lax.fori_loop(0, pieces[0, 0, 0], body, 0)


def _dispatch_kernel(fill_lo, fill_hi, pos, meta_cur, meta_prv, h_hbm, xs_hbm, tbuf, stage, zrow, lsem, rsem, zsem):
    i = pl.program_id(0)
    n = pl.num_programs(0)
    tile = pos.shape[-1] // TOP_K
    slot = i % 2
    n_exp = fill_lo.shape[0]

    def tile_load(j, s):
        return pltpu.make_async_copy(h_hbm.at[pl.ds(j * tile * SUB, tile * SUB), :], tbuf.at[s], lsem.at[s])

    def run_copy(s, stage_row, sorted_row):
        return pltpu.make_async_copy(stage.at[_run_rows(s * (stage.shape[0] // (2 * SUB)) + stage_row), :],
                                     xs_hbm.at[_run_rows(sorted_row), :], rsem.at[s])

    def zero_copy(dst_row):
        return pltpu.make_async_copy(zrow, xs_hbm.at[_tile_rows(dst_row), :], zsem.at[0])

    @pl.when(i == 0)
    def _():
        tile_load(0, 0).start()
        zrow[...] = jnp.zeros_like(zrow)
        stage[...] = jnp.zeros_like(stage)
        for wait in (False, True):
            def per_expert(e, _, wait=wait):
                def body(row, _):
                    if wait:
                        zero_copy(0).wait()
                    else:
                        zero_copy(row).start()
                    return 0
                lax.fori_loop(fill_lo[e], fill_hi[e], body, 0)
                return 0
            lax.fori_loop(0, n_exp, per_expert, 0)
        tail_lo = fill_hi[n_exp - 1]
        tail_pieces = (xs_hbm.shape[0] // SUB - tail_lo) // RUN_ROWS
        for wait in (False, True):
            def tail(c, _, wait=wait):
                cp = pltpu.make_async_copy(stage.at[_run_rows(0), :], xs_hbm.at[_run_rows(tail_lo + c * RUN_ROWS), :],
                                           zsem.at[0])
                if wait:
                    cp.wait()
                else:
                    cp.start()
                return 0
            lax.fori_loop(0, tail_pieces, tail, 0)

    @pl.when(i > 0)
    def _():
        _for_pieces(meta_prv, lambda off, row: run_copy(1 - slot, 0, 0).wait())

    @pl.when(i + 1 < n)
    def _():
        tile_load(i + 1, 1 - slot).start()

    tile_load(i, slot).wait()

    def pack(t, _):
        row = tbuf[slot, _tile_rows(t), :]
        for k in range(TOP_K):
            stage[_stage_tile(pos[0, 0, TOP_K * t + k]), :] = row
        return 0
    lax.fori_loop(0, tile, pack, 0, unroll=8)

    _for_pieces(meta_cur, lambda off, row: run_copy(slot, off, row).start())

    @pl.when(i == n - 1)
    def _():
        _for_pieces(meta_cur, lambda off, row: run_copy(slot, 0, 0).wait())


def _experts_kernel(blk_e, n_act, x_ref, wg, wu, bg, bu, wd, bd, y_out, wg_bf, wu_bf, wd_bf):
    i = pl.program_id(0)
    active = i < n_act[0]
    new_expert = (i == 0) | (blk_e[i] != blk_e[jnp.maximum(i - 1, 0)])

    @pl.when(active & new_expert)
    def _():
        wg_bf[...] = wg[...].astype(BF16)
        wu_bf[...] = wu[...].astype(BF16)
        wd_bf[...] = wd[...].astype(BF16)

    @pl.when(active)
    def _():
        xb = _load_rows(x_ref, x_ref.shape[0] // SUB).astype(BF16)
        hid = _clamped_swiglu(_dot(xb, wg_bf[...]) + bg[...], _dot(xb, wu_bf[...]) + bu[...])
        _store_rows(y_out, _dot(hid.astype(BF16), wd_bf[...]) + bd[...])

    @pl.when(i >= n_act[0])
    def _():
        y_out[...] = jnp.zeros_like(y_out)


def _combine_kernel(alpha, pos, gate, meta_cur, meta_nxt, h1, ln_g, ln_b, y_hbm, out, stage, ff_rows, sem):
    i = pl.program_id(0)
    n = pl.num_programs(0)
    tile = out.shape[0]
    slot = i % 2

    def run_copy(s, stage_row, sorted_row):
        return pltpu.make_async_copy(y_hbm.at[_run_rows(sorted_row), :],
                                     stage.at[_run_rows(s * (stage.shape[0] // (2 * SUB)) + stage_row), :], sem.at[s])

    @pl.when(i == 0)
    def _():
        _for_pieces(meta_cur, lambda off, row: run_copy(0, off, row).start())

    @pl.when(i + 1 < n)
    def _():
        _for_pieces(meta_nxt, lambda off, row: run_copy(1 - slot, off, row).start())

    _for_pieces(meta_cur, lambda off, row: run_copy(slot, 0, 0).wait())

    def gather(t, _):
        acc = gate[0, 0, TOP_K * t] * stage[_stage_tile(pos[0, 0, TOP_K * t]), :]
        for k in range(1, TOP_K):
            acc = acc + gate[0, 0, TOP_K * t + k] * stage[_stage_tile(pos[0, 0, TOP_K * t + k]), :]
        ff_rows[_tile_rows(t), :] = acc
        return 0
    lax.fori_loop(0, tile, gather, 0, unroll=8)

    out[...] = _layer_norm(alpha * _load_rows(h1, tile) + _load_rows(ff_rows, tile), ln_g[...], ln_b[...])


def _block_constants(tile):
    t = np.arange(tile)
    same = (t[:, None] // CHUNK) == (t[None, :] // CHUNK)
    later = same & (t[None, :] > t[:, None])
    chunk_rows = np.zeros((LANES, tile), np.float32)
    chunk_rows[t // CHUNK, t] = 1.0
    uo = np.concatenate([later.astype(np.float32), chunk_rows], axis=0)
    uto = np.concatenate([later.T.astype(np.float32), same.astype(np.float32)], axis=1)
    return jnp.asarray(uo, BF16), jnp.asarray(uto, BF16)


def _expand_constant():
    e = np.zeros((LANES, 2 * SSD_INNER), np.float32)
    for part in range(4):
        for hd in range(SSD_HEADS):
            base = (part // 2) * SSD_INNER + hd * SSD_HEADDIM
            e[part * SSD_HEADS + hd, base:base + SSD_HEADDIM] = 1.0
    return jnp.asarray(e, BF16)


def _full(shape):
    return pl.BlockSpec(shape, lambda *_: (0,) * len(shape))


def kernel(x, meta_tokens, ln_in_g, ln_in_b, w_in, gla_w_a2, gla_b_a, gla_norm_g, ssd_conv_w, ssd_conv_b,
           ssd_dt_bias, ssd_a_log, ssd_d, ssd_norm_g, w_out, ln1_g, ln1_b, router_w, router_b, moe_w_gate,
           moe_w_up, moe_b_gate, moe_b_up, moe_w_down, moe_b_down, ln2_g, ln2_b):
    batch, seq, d = x.shape
    depth = w_in.shape[0]
    assert depth == 1, "single-layer stack"
    n_exp = router_w.shape[-1]
    d_ff = moe_w_gate.shape[-1]
    alpha = (2.0 * depth) ** 0.25
    tile = MIXER_TILE
    assert seq % tile == 0 and d == 1024
    n_tok = batch * seq
    row = lambda a: a.reshape(1, -1).astype(F32)

    wi = w_in[0]
    o_a1 = 1536
    o_z = o_a1 + GLA_RANK
    o_xbc = o_z + SSD_INNER
    o_dt = o_xbc + SSD_CONV_CH
    misc_w = jnp.zeros((d, LANES), F32)
    misc_w = misc_w.at[:, MISC_A1:MISC_A1 + GLA_RANK].set(wi[:, o_a1:o_z])
    misc_w = misc_w.at[:, MISC_DT:MISC_DT + SSD_HEADS].set(wi[:, o_dt:o_dt + SSD_HEADS])
    w_in_r = jnp.concatenate([wi[:, 0:o_a1], wi[:, o_z:o_xbc], wi[:, o_xbc:o_dt], misc_w], axis=1).astype(BF16)
    w_a2 = jnp.zeros((LANES, GLA_KEY), F32).at[MISC_A1:MISC_A1 + GLA_RANK].set(gla_w_a2[0]).astype(BF16)
    mix_params = dict(
        w_in=w_in_r, w_a2=w_a2, b_a=row(gla_b_a[0]), conv_w=ssd_conv_w[0].astype(F32),
        conv_b=row(ssd_conv_b[0]), dt_bias=ssd_dt_bias[0].reshape(-1, 1).astype(F32),
        a_log=ssd_a_log[0].reshape(-1, 1).astype(F32), eexp=_expand_constant())
    out_params = dict(gla_norm_g=row(gla_norm_g[0]), ssd_d=row(jnp.repeat(ssd_d[0], SSD_HEADDIM)),
                      ssd_norm_g=row(ssd_norm_g[0]))

    def mix_args(t):
        uo, uto = _block_constants(t)
        vals = dict(mix_params, uo=uo, uto=uto)
        return [vals[k] for k in _MIX_PARAMS]

    m_tile = PAIR
    x_meta = jnp.concatenate([jnp.zeros((m_tile - N_META, d), F32), meta_tokens.astype(F32)], axis=0)
    meta_in = [x_meta, row(ln_in_g), row(ln_in_b)] + mix_args(m_tile)
    s_gla0, s_ssd0, tail0 = pl.pallas_call(
        _meta_kernel,
        out_shape=(jax.ShapeDtypeStruct((GLA_KEY, GLA_DV), F32),
                   jax.ShapeDtypeStruct((SSD_GROUPS * SSD_STATE, SSD_GROUP_W), F32),
                   jax.ShapeDtypeStruct((CONV_PAD, SSD_CONV_CH), F32)),
        scratch_shapes=[pltpu.VMEM((CONV_PAD + m_tile, SSD_CONV_CH), F32)],
        compiler_params=pltpu.CompilerParams(vmem_limit_bytes=VMEM_LIMIT),
        name="meta_state",
    )(*meta_in)

    n_j = seq // tile
    rw = jnp.zeros((d, LANES), F32).at[:, 0:n_exp].set(router_w[0])
    rw_hi, rw_lo = _split_hi_lo(rw)
    rb = jnp.zeros((1, LANES), F32).at[:, 0:n_exp].set(router_b[0][None])
    su = jnp.asarray(np.triu(np.ones((tile, tile), np.float32), 1), BF16)
    args = ([x, row(ln_in_g), row(ln_in_b)] + mix_args(tile) + [out_params[k] for k in _OUT_PARAMS]
            + [w_out[0].astype(BF16), row(ln1_g[0]), row(ln1_b[0]), rw_hi, rw_lo, rb, su, s_gla0, s_ssd0, tail0])
    n_sub = MIXER_SEQS
    assert batch % n_sub == 0
    in_specs = [pl.BlockSpec((n_sub, tile, d), lambda b, j: (b, j, 0))] + [_full(a.shape) for a in args[1:]]
    tok_blk = pl.BlockSpec((n_sub, 8, tile), lambda b, j: (b, 0, j))
    tile_tab = pl.BlockSpec((n_sub, 1, n_exp, LANES), lambda b, j: (b, j, 0, 0))
    h1, topi, gates, rank, cnt, base, tile_cnt = pl.pallas_call(
        functools.partial(_mixer_kernel, alpha, n_exp),
        grid=(batch // n_sub, n_j),
        in_specs=in_specs,
        out_specs=(pl.BlockSpec((n_sub, tile * SUB, LANES), lambda b, j: (b, j, 0)),
                   tok_blk, tok_blk, tok_blk, _full((n_exp, LANES)), tile_tab, tile_tab),
        out_shape=(jax.ShapeDtypeStruct((batch, seq * SUB, LANES), F32),
                   jax.ShapeDtypeStruct((batch, 8, seq), jnp.int32),
                   jax.ShapeDtypeStruct((batch, 8, seq), F32),
                   jax.ShapeDtypeStruct((batch, 8, seq), jnp.int32),
                   jax.ShapeDtypeStruct((n_exp, LANES), F32),
                   jax.ShapeDtypeStruct((batch, n_j, n_exp, LANES), F32),
                   jax.ShapeDtypeStruct((batch, n_j, n_exp, LANES), F32)),
        scratch_shapes=[pltpu.VMEM((n_sub, GLA_KEY, GLA_DV), F32),
                        pltpu.VMEM((n_sub, SSD_GROUPS * SSD_STATE, SSD_GROUP_W), F32),
                        pltpu.VMEM((n_sub, CONV_PAD + tile, SSD_CONV_CH), F32),
                        pltpu.VMEM((n_sub, tile, d), F32),
                        pltpu.VMEM((n_exp, LANES), F32)],
        compiler_params=pltpu.CompilerParams(dimension_semantics=("arbitrary", "arbitrary"),
                                             vmem_limit_bytes=VMEM_LIMIT),
        name="mixer",
    )(*args)
    h1 = h1.reshape(n_tok * SUB, LANES)
    per_tok = lambda a: a[:, :TOP_K].transpose(1, 0, 2).reshape(TOP_K, n_tok)
    top_e, gates, rank = per_tok(topi), per_tok(gates), per_tok(rank)

    counts = cnt[:, 0].astype(jnp.int32)
    padded = (counts + RUN_ROWS + MOE_BLOCK - 1) // MOE_BLOCK * MOE_BLOCK
    pad_end = jnp.cumsum(padded)
    pad_start = pad_end - padded
    n_blocks = -(-(n_tok * TOP_K) // MOE_BLOCK) + n_exp + 1
    n_rows = n_blocks * MOE_BLOCK
    blk_lo = jnp.arange(n_blocks, dtype=jnp.int32) * MOE_BLOCK
    block_e = jnp.minimum(jnp.sum((blk_lo[:, None] >= pad_end[None, :]).astype(jnp.int32), axis=1), n_exp - 1)
    n_act = (pad_end[-1] // MOE_BLOCK).astype(jnp.int32).reshape(1)
    c_tile = tile
    n_ct = n_tok // c_tile
    run_base = base[..., 0].astype(jnp.int32).reshape(n_ct, n_exp)
    run_len = tile_cnt[..., 0].astype(jnp.int32).reshape(n_ct, n_exp)
    run_pieces = (run_len + RUN_ROWS - 1) // RUN_ROWS
    piece_end = jnp.cumsum(run_pieces, axis=1)
    first_piece = piece_end - run_pieces
    stage_row = first_piece * RUN_ROWS
    max_pieces = -(-(TOP_K * c_tile // RUN_ROWS + n_exp) // PIECE_GROUP) * PIECE_GROUP
    p_ids = jnp.arange(max_pieces, dtype=jnp.int32)
    in_run = (p_ids[None, :, None] >= first_piece[:, None, :]) & (p_ids[None, :, None] < piece_end[:, None, :])
    piece_row = jnp.sum(jnp.where(in_run, (pad_start[None, :] + run_base - stage_row)[:, None, :], 0), axis=2)
    piece_row = piece_row + p_ids[None, :] * RUN_ROWS
    n_pieces = piece_end[:, -1:]
    spare_row = (n_blocks - 1) * MOE_BLOCK + (p_ids[None, :] - n_pieces) * RUN_ROWS
    piece_row = jnp.where(p_ids[None, :] < n_pieces, piece_row, jnp.clip(spare_row, 0, n_rows - RUN_ROWS))
    n_groups = (n_pieces + PIECE_GROUP - 1) // PIECE_GROUP
    meta_w = -(-(max_pieces + 1) // LANES) * LANES
    meta = jnp.concatenate([n_groups, piece_row,
                            jnp.zeros((n_ct, meta_w - 1 - max_pieces), jnp.int32)], axis=1).reshape(n_ct, 1, meta_w)
    shift_tok = jnp.repeat((stage_row - run_base).T, c_tile, axis=1)
    e_ids = jnp.arange(n_exp, dtype=jnp.int32)[:, None, None]
    pos = rank + jnp.sum(jnp.where(top_e[None] == e_ids, shift_tok[:, None, :], 0), axis=0)
    stage_rows = max_pieces * RUN_ROWS
    slot_tok = (jnp.arange(n_tok, dtype=jnp.int32) // c_tile) % 2
    pos_addr = (pos + slot_tok[None, :] * stage_rows) * SUB
    per_tile = lambda a: a.reshape(TOP_K, n_ct, c_tile).transpose(1, 2, 0).reshape(n_ct, 1, TOP_K * c_tile)
    pos3, gate3 = per_tile(pos_addr), per_tile(gates)
    smem_blk = lambda width, f: pl.BlockSpec((1, 1, width), f, memory_space=pltpu.SMEM)

    x_sorted = pl.pallas_call(
        _dispatch_kernel,
        grid_spec=pltpu.PrefetchScalarGridSpec(
            num_scalar_prefetch=2,
            grid=(n_ct,),
            in_specs=[smem_blk(TOP_K * c_tile, lambda i, lo, hi: (i, 0, 0)),
                      smem_blk(meta_w,lambda i, lo, hi: (i, 0, 0)),
                      smem_blk(meta_w,lambda i, lo, hi: (jnp.maximum(i - 1, 0), 0, 0)),
                      pl.BlockSpec(memory_space=pl.ANY)],
            out_specs=pl.BlockSpec(memory_space=pl.ANY),
            scratch_shapes=[pltpu.VMEM((2, c_tile * SUB, LANES), F32),
                            pltpu.VMEM((2 * stage_rows * SUB, LANES), F32), pltpu.VMEM((SUB, LANES), F32),
                            pltpu.SemaphoreType.DMA((2,)), pltpu.SemaphoreType.DMA((2,)),
                            pltpu.SemaphoreType.DMA((1,))]),
        out_shape=jax.ShapeDtypeStruct((n_rows * SUB, LANES), F32),
        compiler_params=pltpu.CompilerParams(dimension_semantics=("arbitrary",), vmem_limit_bytes=VMEM_LIMIT),
        name="moe_dispatch",
    )(pad_start + counts, pad_end, pos3, meta, meta, h1)

    last_act = lambda i, na: jnp.minimum(i, na[0] - 1)
    e_mat = lambda shape: pl.BlockSpec((None,) + shape, lambda i, be, na: (be[i], 0, 0))
    y_sorted = pl.pallas_call(
        _experts_kernel,
        grid_spec=pltpu.PrefetchScalarGridSpec(
            num_scalar_prefetch=2,
            grid=(n_blocks,),
            in_specs=[pl.BlockSpec((MOE_BLOCK * SUB, LANES), lambda i, be, na: (last_act(i, na), 0)),
                      e_mat((d, d_ff)), e_mat((d, d_ff)), e_mat((1, d_ff)), e_mat((1, d_ff)),
                      e_mat((d_ff, d)), e_mat((1, d))],
            out_specs=pl.BlockSpec((MOE_BLOCK * SUB, LANES), lambda i, be, na: (i, 0)),
            scratch_shapes=[pltpu.VMEM((d, d_ff), BF16), pltpu.VMEM((d, d_ff), BF16),
                            pltpu.VMEM((d_ff, d), BF16)]),
        out_shape=jax.ShapeDtypeStruct((n_rows * SUB, LANES), F32),
        compiler_params=pltpu.CompilerParams(dimension_semantics=("arbitrary",), vmem_limit_bytes=VMEM_LIMIT),
        name="moe_experts",
    )(block_e, n_act, x_sorted,
      moe_w_gate[0].astype(F32), moe_w_up[0].astype(F32),
      moe_b_gate[0].reshape(n_exp, 1, d_ff).astype(F32), moe_b_up[0].reshape(n_exp, 1, d_ff).astype(F32),
      moe_w_down[0].astype(F32), moe_b_down[0].reshape(n_exp, 1, d).astype(F32))

    out = pl.pallas_call(
        functools.partial(_combine_kernel, alpha),
        grid=(n_ct,),
        in_specs=[smem_blk(TOP_K * c_tile, lambda i: (i, 0, 0)),
                  smem_blk(TOP_K * c_tile, lambda i: (i, 0, 0)),
                  smem_blk(meta_w,lambda i: (i, 0, 0)),
                  smem_blk(meta_w,lambda i: (jnp.minimum(i + 1, n_ct - 1), 0, 0)),
                  pl.BlockSpec((c_tile * SUB, LANES), lambda i: (i, 0)),
                  _full((1, d)), _full((1, d)),
                  pl.BlockSpec(memory_space=pl.ANY)],
        out_specs=pl.BlockSpec((c_tile, d), lambda i: (i, 0)),
        out_shape=jax.ShapeDtypeStruct((n_tok, d), F32),
        scratch_shapes=[pltpu.VMEM((2 * stage_rows * SUB, LANES), F32), pltpu.VMEM((c_tile * SUB, LANES), F32),
                        pltpu.SemaphoreType.DMA((2,))],
        compiler_params=pltpu.CompilerParams(dimension_semantics=("arbitrary",), vmem_limit_bytes=VMEM_LIMIT),
        name="moe_combine",
    )(pos3, gate3, meta, meta, h1, row(ln2_g[0]), row(ln2_b[0]), y_sorted)
    return out.reshape(batch, seq, d).astype(x.dtype)
```

```python
import functools

import numpy as np
import jax
import jax.numpy as jnp
from jax import lax
from jax.experimental import pallas as pl
from jax.experimental.pallas import tpu as pltpu

F32 = jnp.float32
BF16 = jnp.bfloat16

CHUNK = 64
N_META = 16
PAIR = 2 * CHUNK
GLA_HEADS = 4
GLA_DK = 64
GLA_DV = 128
GLA_KEY = GLA_HEADS * GLA_DK
GLA_VAL = GLA_HEADS * GLA_DV
GLA_RANK = 16
GLA_TAU = 16.0
SSD_INNER = 512
SSD_HEADS = 8
SSD_HEADDIM = 64
SSD_GROUPS = 2
SSD_STATE = 128
SSD_GROUP_W = SSD_INNER // SSD_GROUPS
SSD_CONV = 4
SSD_CONV_CH = SSD_INNER + 2 * SSD_GROUPS * SSD_STATE
TOP_K = 4
SWIGLU_LIMIT = 7.0
SWIGLU_ALPHA = 1.702
MOE_BLOCK = 512
LN_EPS = 1e-5
RMS_EPS = 1e-6
LANES = 128

C_Q, C_K, C_V, C_OG, C_Z, C_XBC, C_MISC, C_END = 0, 256, 512, 1024, 1536, 2048, 3072, 3200
MISC_A1 = 0
MISC_DT = 16
CONV_PAD = 8

MIXER_TILE = 256
MIXER_SEQS = 1
VMEM_LIMIT = 56 * 1024 * 1024


def _dot(a, b):
    return jnp.dot(a, b, preferred_element_type=F32)


def _split_hi_lo(x):
    hi = x.astype(BF16)
    lo = (x - hi.astype(F32)).astype(BF16)
    return hi, lo


def _log_sigmoid(x):
    return jnp.minimum(x, 0.0) - jnp.log1p(jnp.exp(-jnp.abs(x)))


def _softplus(x):
    return jnp.maximum(x, 0.0) + jnp.log1p(jnp.exp(-jnp.abs(x)))


def _silu(x):
    return x * jax.nn.sigmoid(x)


def _layer_norm(t, g, b):
    mu = jnp.mean(t, axis=-1, keepdims=True)
    tc = t - mu
    var = jnp.mean(tc * tc, axis=-1, keepdims=True)
    return tc * lax.rsqrt(var + LN_EPS) * g + b


def _group_rms(t, g, width):
    outs = []
    for s in range(0, t.shape[-1], width):
        seg = t[:, s:s + width]
        ms = jnp.mean(seg * seg, axis=-1, keepdims=True)
        outs.append(seg * lax.rsqrt(ms + RMS_EPS) * g[:, s:s + width])
    return jnp.concatenate(outs, axis=-1)


def _mixer_tile(h, tile, valid_col, valid_row, p, s_gla, s_ssd, xbc_buf, mix_buf, need_out):
    hb = h.astype(BF16)
    w_in = p["w_in"]

    def proj(lo, hi):
        r = _dot(hb, w_in[:, lo:hi])
        if valid_col is not None:
            r = jnp.where(valid_col, r, 0.0)
        return r

    misc = proj(C_MISC, C_END)

    la = _log_sigmoid(_dot(misc.astype(BF16), p["w_a2"][...]) + p["b_a"][...]) * (1.0 / GLA_TAU)
    if valid_col is not None:
        la = jnp.where(valid_col, la, 0.0)
    la_hi, la_lo = _split_hi_lo(la)
    r = _dot(p["uo"][...], jnp.concatenate([la_hi, la_lo], axis=1))
    dec_exp = r[0:tile, 0:GLA_KEY] + r[0:tile, GLA_KEY:]
    tot_rows = r[tile:, 0:GLA_KEY] + r[tile:, GLA_KEY:]
    gla_dec_t = jnp.exp(tot_rows.T)
    kd = proj(C_K, C_V) * jnp.exp(dec_exp)
    v_bf = proj(C_V, C_OG).astype(BF16)

    misc_t = misc.T
    dt = _softplus(misc_t[MISC_DT:MISC_DT + SSD_HEADS, :] + p["dt_bias"][...])
    if valid_row is not None:
        dt = jnp.where(valid_row, dt, 0.0)
    dta = dt * (-jnp.exp(p["a_log"][...]))
    d_hi, d_lo = _split_hi_lo(dta)
    r2 = _dot(jnp.concatenate([d_hi, d_lo], axis=0), p["uto"][...])
    w = jnp.exp(r2[0:8, 0:tile] + r2[8:16, 0:tile]) * dt
    ssd_dec = jnp.exp(r2[0:8, tile:] + r2[8:16, tile:])
    w_hi, w_lo = _split_hi_lo(w)
    c_hi, c_lo = _split_hi_lo(ssd_dec)
    packed = jnp.concatenate(
        [w_hi.astype(F32), w_lo.astype(F32), c_hi.astype(F32), c_lo.astype(F32),
         jnp.zeros((LANES - 4 * SSD_HEADS, tile), F32)], axis=0)
    expanded = _dot(packed.T.astype(BF16), p["eexp"][...])
    w_exp = expanded[:, 0:SSD_INNER]
    ssd_dec_exp = expanded[:, SSD_INNER:]

    xbc_buf[CONV_PAD:CONV_PAD + tile, :] = proj(C_XBC, C_MISC)
    acc = p["conv_b"][...] + p["conv_w"][0:1, :] * xbc_buf[pl.ds(CONV_PAD - 3, tile), :]
    for j in range(1, SSD_CONV):
        acc = acc + p["conv_w"][j:j + 1, :] * xbc_buf[pl.ds(CONV_PAD - 3 + j, tile), :]
    xbc_buf[0:CONV_PAD, :] = xbc_buf[tile:tile + CONV_PAD, :]
    xa = _silu(acc)
    xs = xa[:, 0:SSD_INNER]
    bm = xa[:, SSD_INNER:SSD_INNER + SSD_GROUPS * SSD_STATE]
    cm_bf = xa[:, SSD_INNER + SSD_GROUPS * SSD_STATE:].astype(BF16)
    xw_bf = (xs * w_exp).astype(BF16)

    if need_out:
        q = proj(C_Q, C_K) * (GLA_DK ** -0.5)
        head_of_lane = lax.broadcasted_iota(jnp.int32, (CHUNK, GLA_KEY), 1) // GLA_DK

    lane = lax.broadcasted_iota(jnp.int32, (1, PAIR), 1)
    for pr in range(tile // PAIR):
        rows = slice(pr * PAIR, (pr + 1) * PAIR)
        kd_t = kd[rows].T
        bm_t = bm[rows].T
        for half in range(2):
            c = 2 * pr + half
            r0 = c * CHUNK
            sel = (lane >= half * CHUNK) & (lane < (half + 1) * CHUNK)
            kd_m = jnp.where(sel, kd_t, 0.0).astype(BF16)
            upd = jnp.concatenate(
                [_dot(kd_m[hd * GLA_DK:(hd + 1) * GLA_DK], v_bf[rows, hd * GLA_DV:(hd + 1) * GLA_DV])
                 for hd in range(GLA_HEADS)], axis=0)
            s_new = gla_dec_t[:, c:c + 1] * s_gla[...] + upd
            s_gla[...] = s_new
            bm_m = jnp.where(sel, bm_t, 0.0).astype(BF16)
            ssd_new = []
            for g in range(SSD_GROUPS):
                gl = slice(g * SSD_GROUP_W, (g + 1) * SSD_GROUP_W)
                gr = slice(g * SSD_STATE, (g + 1) * SSD_STATE)
                u = _dot(bm_m[gr], xw_bf[rows, gl])
                sg = ssd_dec_exp[r0:r0 + 1, gl] * s_ssd[gr, :] + u
                s_ssd[gr, :] = sg
                ssd_new.append(sg)
            if need_out:
                qc = q[r0:r0 + CHUNK]
                qm = jnp.concatenate(
                    [jnp.where(head_of_lane == hd, qc, 0.0) for hd in range(GLA_HEADS)],
                    axis=0).astype(BF16)
                o = _dot(qm, s_new.astype(BF16))
                for hd in range(GLA_HEADS):
                    mix_buf[r0:r0 + CHUNK, hd * GLA_DV:(hd + 1) * GLA_DV] = o[hd * CHUNK:(hd + 1) * CHUNK]
                for g in range(SSD_GROUPS):
                    y = _dot(cm_bf[r0:r0 + CHUNK, g * SSD_STATE:(g + 1) * SSD_STATE],
                             ssd_new[g].astype(BF16))
                    mix_buf[r0:r0 + CHUNK, GLA_VAL + g * SSD_GROUP_W:GLA_VAL + (g + 1) * SSD_GROUP_W] = y

    if not need_out:
        return None
    o_gla = _group_rms(mix_buf[:, 0:GLA_VAL], p["gla_norm_g"][...], GLA_DV) * _silu(proj(C_OG, C_Z))
    y = (mix_buf[:, GLA_VAL:] + xs * p["ssd_d"][...]) * _silu(proj(C_Z, C_XBC))
    y = _group_rms(y, p["ssd_norm_g"][...], SSD_GROUP_W)
    return o_gla, y


_MIX_PARAMS = ("w_in", "w_a2", "b_a", "conv_w", "conv_b", "dt_bias", "a_log", "uo", "uto", "eexp")
_OUT_PARAMS = ("gla_norm_g", "ssd_d", "ssd_norm_g")


def _meta_kernel(x_ref, ln_g, ln_b, *rest):
    np_ = len(_MIX_PARAMS)
    p = dict(zip(_MIX_PARAMS, rest[:np_]))
    s_gla_out, s_ssd_out, tail_out, xbc_buf = rest[np_:]
    tile = x_ref.shape[0]
    s_gla_out[...] = jnp.zeros_like(s_gla_out)
    s_ssd_out[...] = jnp.zeros_like(s_ssd_out)
    xbc_buf[0:CONV_PAD, :] = jnp.zeros((CONV_PAD, xbc_buf.shape[1]), F32)
    first_valid = tile - N_META
    valid_col = lax.broadcasted_iota(jnp.int32, (tile, 1), 0) >= first_valid
    valid_row = lax.broadcasted_iota(jnp.int32, (1, tile), 1) >= first_valid
    h = _layer_norm(x_ref[...], ln_g[...], ln_b[...])
    _mixer_tile(h, tile, valid_col, valid_row, p, s_gla_out, s_ssd_out, xbc_buf, None, False)
    tail_out[...] = xbc_buf[0:CONV_PAD, :]


def _mixer_kernel(alpha, n_exp, x_ref, ln_g, ln_b, *rest):
    np_ = len(_MIX_PARAMS)
    p = dict(zip(_MIX_PARAMS, rest[:np_]))
    rest = rest[np_:]
    p.update(zip(_OUT_PARAMS, rest[:3]))
    (w_out, ln1_g, ln1_b, rw_hi, rw_lo, rb, su, s_gla0, s_ssd0, tail0,
     h1_out, topi_out, gate_out, rank_out, cnt_out, base_out, tcnt_out,
     s_gla, s_ssd, xbc_buf, mix_buf, run_cnt) = rest[3:]
    n_sub, tile = x_ref.shape[0], x_ref.shape[1]
    b, j = pl.program_id(0), pl.program_id(1)

    @pl.when(j == 0)
    def _():
        for s in range(n_sub):
            s_gla[s] = s_gla0[...]
            s_ssd[s] = s_ssd0[...]
            xbc_buf[s, 0:CONV_PAD, :] = tail0[...]

    @pl.when((b == 0) & (j == 0))
    def _():
        run_cnt[...] = jnp.zeros_like(run_cnt)

    count = run_cnt[...]
    pad_rows = topi_out.shape[1] - TOP_K
    for s in range(n_sub):
        h = _layer_norm(x_ref[s], ln_g[...], ln_b[...])
        o_gla, y = _mixer_tile(h, tile, None, None, p, s_gla.at[s], s_ssd.at[s], xbc_buf.at[s],
                               mix_buf.at[s], True)
        mix = _dot(o_gla.astype(BF16), w_out[0:GLA_VAL, :]) + _dot(y.astype(BF16), w_out[GLA_VAL:, :])
        h1 = _layer_norm(alpha * h + mix, ln1_g[...], ln1_b[...])
        _store_rows(h1_out.at[s], h1)

        h_hi, h_lo = _split_hi_lo(h1)
        logits = _dot(h_hi, rw_hi[...]) + (_dot(h_hi, rw_lo[...]) + _dot(h_lo, rw_hi[...])) + rb[...]
        lt = logits.T[0:n_exp, :]
        e_iota = lax.broadcasted_iota(jnp.int32, (n_exp, tile), 0).astype(F32)
        work = lt
        vals, hots = [], []
        for k in range(TOP_K):
            m = jnp.max(work, axis=0, keepdims=True)
            idx = jnp.min(jnp.where(work == m, e_iota, float(n_exp)), axis=0, keepdims=True)
            hot = e_iota == idx
            work = jnp.where(hot, -jnp.inf, work)
            vals.append(m)
            hots.append(hot)
            topi_out[s, k:k + 1, :] = idx.astype(jnp.int32)
        exps = [jnp.exp(vk - vals[0]) for vk in vals]
        denom = exps[0] + exps[1] + exps[2] + exps[3]
        for k in range(TOP_K):
            gate_out[s, k:k + 1, :] = exps[k] / denom
        member = (hots[0] | hots[1] | hots[2] | hots[3]).astype(F32).astype(BF16)
        before = _dot(member, su[...]) + count[:, 0:1]
        for k in range(TOP_K):
            rk = jnp.sum(jnp.where(hots[k], before, 0.0), axis=0, keepdims=True)
            rank_out[s, k:k + 1, :] = rk.astype(jnp.int32)
        topi_out[s, TOP_K:, :] = jnp.zeros((pad_rows, tile), jnp.int32)
        gate_out[s, TOP_K:, :] = jnp.zeros((pad_rows, tile), F32)
        rank_out[s, TOP_K:, :] = jnp.zeros((pad_rows, tile), jnp.int32)
        tile_count = _dot(member, jnp.ones((tile, LANES), BF16))
        base_out[s, 0] = count
        tcnt_out[s, 0] = tile_count
        count = count + tile_count
    run_cnt[...] = count
    cnt_out[...] = count


def _clamped_swiglu(a, u):
    a = jnp.minimum(a, SWIGLU_LIMIT)
    u = jnp.clip(u, -SWIGLU_LIMIT, SWIGLU_LIMIT)
    return a * jax.nn.sigmoid(SWIGLU_ALPHA * a) * (u + 1.0)


SUB = 8


def _load_rows(ref, n_rows):
    return jnp.concatenate([ref[pl.ds(j, n_rows, stride=SUB), :] for j in range(SUB)], axis=1)


def _store_rows(ref, val):
    for j in range(SUB):
        ref[pl.ds(j, val.shape[0], stride=SUB), :] = val[:, j * LANES:(j + 1) * LANES]


def _tile_rows(row):
    return pl.ds(pl.multiple_of(row * SUB, SUB), SUB)


RUN_ROWS = 8
PIECE_GROUP = 4
REST_SIZES = (4, 2, 1)


def _run_rows(row):
    return pl.ds(pl.multiple_of(row * SUB, SUB), RUN_ROWS * SUB)


def _stage_tile(addr):
    return pl.ds(pl.multiple_of(addr, SUB), SUB)


def _for_pieces(pieces, fn):
    def body(g, _):
        for u in range(PIECE_GROUP):
            p = g * PIECE_GROUP + u
            fn(p * RUN_ROWS, pieces[0, 0, 1 + p])
        return 0
    lax.fori_loop(0, pieces[0, 0, 0], body, 0)


def _dispatch_kernel(fill_lo, fill_hi, pos, meta_cur, meta_prv, tail_cur, tail_prv, h_hbm, xs_hbm,
                     tbuf, stage, zrow, lsem, rsem, zsem):
    i = pl.program_id(0)
    n = pl.num_programs(0)
    tile = pos.shape[-1] // TOP_K
    slot = i % 2
    n_exp = fill_lo.shape[0]
    stage_rows = stage.shape[0] // (2 * SUB)
    rest_base = stage_rows - n_exp * RUN_ROWS

    def tile_load(j, s):
        return pltpu.make_async_copy(h_hbm.at[pl.ds(j * tile * SUB, tile * SUB), :], tbuf.at[s], lsem.at[s])

    def rows_copy(s, stage_row, sorted_row, n_rows):
        src = stage.at[pl.ds(pl.multiple_of((s * stage_rows + stage_row) * SUB, SUB), n_rows * SUB), :]
        dst = xs_hbm.at[pl.ds(pl.multiple_of(sorted_row * SUB, SUB), n_rows * SUB), :]
        return pltpu.make_async_copy(src, dst, rsem.at[s])

    def for_rest(rest, s, wait):
        for c, size in enumerate(REST_SIZES):
            first = 8 + 2 * n_exp * c

            def body(q, _, first=first, size=size):
                if wait:
                    rows_copy(s, 0, 0, size).wait()
                else:
                    rows_copy(s, rest[0, 0, first + 2 * q + 1], rest[0, 0, first + 2 * q], size).start()
                return 0
            lax.fori_loop(0, rest[0, 0, c], body, 0)

    def zero_copy(dst_row):
        return pltpu.make_async_copy(zrow, xs_hbm.at[_tile_rows(dst_row), :], zsem.at[0])

    @pl.when(i == 0)
    def _():
        tile_load(0, 0).start()
        zrow[...] = jnp.zeros_like(zrow)
        stage[...] = jnp.zeros_like(stage)
        for wait in (False, True):
            def per_expert(e, _, wait=wait):
                def body(row, _):
                    if wait:
                        zero_copy(0).wait()
                    else:
                        zero_copy(row).start()
                    return 0
                lax.fori_loop(fill_lo[e], fill_hi[e], body, 0)
                return 0
            lax.fori_loop(0, n_exp, per_expert, 0)
        tail_lo = fill_hi[n_exp - 1]
        tail_pieces = (xs_hbm.shape[0] // SUB - tail_lo) // RUN_ROWS
        for wait in (False, True):
            def tail(c, _, wait=wait):
                cp = pltpu.make_async_copy(stage.at[_run_rows(0), :], xs_hbm.at[_run_rows(tail_lo + c * RUN_ROWS), :],
                                           zsem.at[0])
                if wait:
                    cp.wait()
                else:
                    cp.start()
                return 0
            lax.fori_loop(0, tail_pieces, tail, 0)

    @pl.when(i + 1 < n)
    def _():
        tile_load(i + 1, 1 - slot).start()

    tile_load(i, slot).wait()

    def pack(t, _):
        row = tbuf[slot, _tile_rows(t), :]
        for k in range(TOP_K):
            stage[_stage_tile(pos[0, 0, TOP_K * t + k]), :] = row
        return 0
    lax.fori_loop(0, tile, pack, 0, unroll=8)

    _for_pieces(meta_cur, lambda off, row: rows_copy(slot, off, row, RUN_ROWS).start())
    for_rest(tail_cur, slot, False)

    def wait_tile(rest, s):
        rows_copy(s, 0, 0, TOP_K * tile).wait()

        def body(q, _):
            rows_copy(s, 0, 0, RUN_ROWS).wait()
            return 0
        lax.fori_loop(0, rest[0, 0, len(REST_SIZES)], body, 0)

    @pl.when(i > 0)
    def _():
        wait_tile(tail_prv, 1 - slot)

    @pl.when(i == n - 1)
    def _():
        wait_tile(tail_cur, slot)


def _experts_kernel(blk_e, n_act, x_ref, wg, wu, bg, bu, wd, bd, y_out, wg_bf, wu_bf, wd_bf):
    i = pl.program_id(0)
    active = i < n_act[0]
    new_expert = (i == 0) | (blk_e[i] != blk_e[jnp.maximum(i - 1, 0)])

    @pl.when(active & new_expert)
    def _():
        wg_bf[...] = wg[...].astype(BF16)
        wu_bf[...] = wu[...].astype(BF16)
        wd_bf[...] = wd[...].astype(BF16)

    @pl.when(active)
    def _():
        xb = _load_rows(x_ref, x_ref.shape[0] // SUB).astype(BF16)
        hid = _clamped_swiglu(_dot(xb, wg_bf[...]) + bg[...], _dot(xb, wu_bf[...]) + bu[...])
        _store_rows(y_out, _dot(hid.astype(BF16), wd_bf[...]) + bd[...])

    @pl.when(i >= n_act[0])
    def _():
        y_out[...] = jnp.zeros_like(y_out)


def _combine_kernel(alpha, n_exp, pos, gate, meta_cur, meta_nxt, tail_cur, tail_nxt, h1, ln_g, ln_b, y_hbm, out,
                    stage, ff_rows, sem):
    i = pl.program_id(0)
    n = pl.num_programs(0)
    tile = out.shape[0]
    slot = i % 2
    stage_rows = stage.shape[0] // (2 * SUB)
    rest_base = stage_rows - n_exp * RUN_ROWS

    def run_copy(s, stage_row, sorted_row):
        return pltpu.make_async_copy(y_hbm.at[_run_rows(sorted_row), :],
                                     stage.at[_run_rows(s * stage_rows + stage_row), :], sem.at[s])

    def fetch(meta, tails, s, wait):
        act = (lambda cp: cp.wait()) if wait else (lambda cp: cp.start())
        _for_pieces(meta, lambda off, row: act(run_copy(s, 0 if wait else off, 0 if wait else row)))

        if wait:
            pltpu.make_async_copy(y_hbm.at[pl.ds(0, n_exp * RUN_ROWS * SUB), :],
                                  stage.at[pl.ds(0, n_exp * RUN_ROWS * SUB), :], sem.at[s]).wait()
            return

        def rest(g, _):
            for u in range(PIECE_GROUP):
                e = g * PIECE_GROUP + u
                run_copy(s, rest_base + e * RUN_ROWS, tails[0, 0, e]).start()
            return 0
        lax.fori_loop(0, n_exp // PIECE_GROUP, rest, 0)

    @pl.when(i == 0)
    def _():
        fetch(meta_cur, tail_cur, 0, False)

    @pl.when(i + 1 < n)
    def _():
        fetch(meta_nxt, tail_nxt, 1 - slot, False)

    fetch(meta_cur, tail_cur, slot, True)

    def gather(t, _):
        acc = gate[0, 0, TOP_K * t] * stage[_stage_tile(pos[0, 0, TOP_K * t]), :]
        for k in range(1, TOP_K):
            acc = acc + gate[0, 0, TOP_K * t + k] * stage[_stage_tile(pos[0, 0, TOP_K * t + k]), :]
        ff_rows[_tile_rows(t), :] = acc
        return 0
    lax.fori_loop(0, tile, gather, 0, unroll=8)

    out[...] = _layer_norm(alpha * _load_rows(h1, tile) + _load_rows(ff_rows, tile), ln_g[...], ln_b[...])


def _block_constants(tile):
    t = np.arange(tile)
    same = (t[:, None] // CHUNK) == (t[None, :] // CHUNK)
    later = same & (t[None, :] > t[:, None])
    chunk_rows = np.zeros((LANES, tile), np.float32)
    chunk_rows[t // CHUNK, t] = 1.0
    uo = np.concatenate([later.astype(np.float32), chunk_rows], axis=0)
    uto = np.concatenate([later.T.astype(np.float32), same.astype(np.float32)], axis=1)
    return jnp.asarray(uo, BF16), jnp.asarray(uto, BF16)


def _expand_constant():
    e = np.zeros((LANES, 2 * SSD_INNER), np.float32)
    for part in range(4):
        for hd in range(SSD_HEADS):
            base = (part // 2) * SSD_INNER + hd * SSD_HEADDIM
            e[part * SSD_HEADS + hd, base:base + SSD_HEADDIM] = 1.0
    return jnp.asarray(e, BF16)


def _full(shape):
    return pl.BlockSpec(shape, lambda *_: (0,) * len(shape))


def kernel(x, meta_tokens, ln_in_g, ln_in_b, w_in, gla_w_a2, gla_b_a, gla_norm_g, ssd_conv_w, ssd_conv_b,
           ssd_dt_bias, ssd_a_log, ssd_d, ssd_norm_g, w_out, ln1_g, ln1_b, router_w, router_b, moe_w_gate,
           moe_w_up, moe_b_gate, moe_b_up, moe_w_down, moe_b_down, ln2_g, ln2_b):
    batch, seq, d = x.shape
    depth = w_in.shape[0]
    assert depth == 1, "single-layer stack"
    n_exp = router_w.shape[-1]
    d_ff = moe_w_gate.shape[-1]
    alpha = (2.0 * depth) ** 0.25
    tile = MIXER_TILE
    assert seq % tile == 0 and d == 1024
    n_tok = batch * seq
    row = lambda a: a.reshape(1, -1).astype(F32)

    wi = w_in[0]
    o_a1 = 1536
    o_z = o_a1 + GLA_RANK
    o_xbc = o_z + SSD_INNER
    o_dt = o_xbc + SSD_CONV_CH
    misc_w = jnp.zeros((d, LANES), F32)
    misc_w = misc_w.at[:, MISC_A1:MISC_A1 + GLA_RANK].set(wi[:, o_a1:o_z])
    misc_w = misc_w.at[:, MISC_DT:MISC_DT + SSD_HEADS].set(wi[:, o_dt:o_dt + SSD_HEADS])
    w_in_r = jnp.concatenate([wi[:, 0:o_a1], wi[:, o_z:o_xbc], wi[:, o_xbc:o_dt], misc_w], axis=1).astype(BF16)
    w_a2 = jnp.zeros((LANES, GLA_KEY), F32).at[MISC_A1:MISC_A1 + GLA_RANK].set(gla_w_a2[0]).astype(BF16)
    mix_params = dict(
        w_in=w_in_r, w_a2=w_a2, b_a=row(gla_b_a[0]), conv_w=ssd_conv_w[0].astype(F32),
        conv_b=row(ssd_conv_b[0]), dt_bias=ssd_dt_bias[0].reshape(-1, 1).astype(F32),
        a_log=ssd_a_log[0].reshape(-1, 1).astype(F32), eexp=_expand_constant())
    out_params = dict(gla_norm_g=row(gla_norm_g[0]), ssd_d=row(jnp.repeat(ssd_d[0], SSD_HEADDIM)),
                      ssd_norm_g=row(ssd_norm_g[0]))

    def mix_args(t):
        uo, uto = _block_constants(t)
        vals = dict(mix_params, uo=uo, uto=uto)
        return [vals[k] for k in _MIX_PARAMS]

    m_tile = PAIR
    x_meta = jnp.concatenate([jnp.zeros((m_tile - N_META, d), F32), meta_tokens.astype(F32)], axis=0)
    meta_in = [x_meta, row(ln_in_g), row(ln_in_b)] + mix_args(m_tile)
    s_gla0, s_ssd0, tail0 = pl.pallas_call(
        _meta_kernel,
        out_shape=(jax.ShapeDtypeStruct((GLA_KEY, GLA_DV), F32),
                   jax.ShapeDtypeStruct((SSD_GROUPS * SSD_STATE, SSD_GROUP_W), F32),
                   jax.ShapeDtypeStruct((CONV_PAD, SSD_CONV_CH), F32)),
        scratch_shapes=[pltpu.VMEM((CONV_PAD + m_tile, SSD_CONV_CH), F32)],
        compiler_params=pltpu.CompilerParams(vmem_limit_bytes=VMEM_LIMIT),
        name="meta_state",
    )(*meta_in)

    n_j = seq // tile
    rw = jnp.zeros((d, LANES), F32).at[:, 0:n_exp].set(router_w[0])
    rw_hi, rw_lo = _split_hi_lo(rw)
    rb = jnp.zeros((1, LANES), F32).at[:, 0:n_exp].set(router_b[0][None])
    su = jnp.asarray(np.triu(np.ones((tile, tile), np.float32), 1), BF16)
    args = ([x, row(ln_in_g), row(ln_in_b)] + mix_args(tile) + [out_params[k] for k in _OUT_PARAMS]
            + [w_out[0].astype(BF16), row(ln1_g[0]), row(ln1_b[0]), rw_hi, rw_lo, rb, su, s_gla0, s_ssd0, tail0])
    n_sub = MIXER_SEQS
    assert batch % n_sub == 0
    in_specs = [pl.BlockSpec((n_sub, tile, d), lambda b, j: (b, j, 0))] + [_full(a.shape) for a in args[1:]]
    tok_blk = pl.BlockSpec((n_sub, 8, tile), lambda b, j: (b, 0, j))
    tile_tab = pl.BlockSpec((n_sub, 1, n_exp, LANES), lambda b, j: (b, j, 0, 0))
    h1, topi, gates, rank, cnt, base, tile_cnt = pl.pallas_call(
        functools.partial(_mixer_kernel, alpha, n_exp),
        grid=(batch // n_sub, n_j),
        in_specs=in_specs,
        out_specs=(pl.BlockSpec((n_sub, tile * SUB, LANES), lambda b, j: (b, j, 0)),
                   tok_blk, tok_blk, tok_blk, _full((n_exp, LANES)), tile_tab, tile_tab),
        out_shape=(jax.ShapeDtypeStruct((batch, seq * SUB, LANES), F32),
                   jax.ShapeDtypeStruct((batch, 8, seq), jnp.int32),
                   jax.ShapeDtypeStruct((batch, 8, seq), F32),
                   jax.ShapeDtypeStruct((batch, 8, seq), jnp.int32),
                   jax.ShapeDtypeStruct((n_exp, LANES), F32),
                   jax.ShapeDtypeStruct((batch, n_j, n_exp, LANES), F32),
                   jax.ShapeDtypeStruct((batch, n_j, n_exp, LANES), F32)),
        scratch_shapes=[pltpu.VMEM((n_sub, GLA_KEY, GLA_DV), F32),
                        pltpu.VMEM((n_sub, SSD_GROUPS * SSD_STATE, SSD_GROUP_W), F32),
                        pltpu.VMEM((n_sub, CONV_PAD + tile, SSD_CONV_CH), F32),
                        pltpu.VMEM((n_sub, tile, d), F32),
                        pltpu.VMEM((n_exp, LANES), F32)],
        compiler_params=pltpu.CompilerParams(dimension_semantics=("arbitrary", "arbitrary"),
                                             vmem_limit_bytes=VMEM_LIMIT),
        name="mixer",
    )(*args)
    h1 = h1.reshape(n_tok * SUB, LANES)
    per_tok = lambda a: a[:, :TOP_K].transpose(1, 0, 2).reshape(TOP_K, n_tok)
    top_e, gates, rank = per_tok(topi), per_tok(gates), per_tok(rank)

    counts = cnt[:, 0].astype(jnp.int32)
    padded = (counts + RUN_ROWS + MOE_BLOCK - 1) // MOE_BLOCK * MOE_BLOCK
    pad_end = jnp.cumsum(padded)
    pad_start = pad_end - padded
    n_blocks = -(-(n_tok * TOP_K) // MOE_BLOCK) + n_exp + 1
    n_rows = n_blocks * MOE_BLOCK
    blk_lo = jnp.arange(n_blocks, dtype=jnp.int32) * MOE_BLOCK
    block_e = jnp.minimum(jnp.sum((blk_lo[:, None] >= pad_end[None, :]).astype(jnp.int32), axis=1), n_exp - 1)
    n_act = (pad_end[-1] // MOE_BLOCK).astype(jnp.int32).reshape(1)
    c_tile = tile
    n_ct = n_tok // c_tile
    run_base = base[..., 0].astype(jnp.int32).reshape(n_ct, n_exp)
    run_len = tile_cnt[..., 0].astype(jnp.int32).reshape(n_ct, n_exp)
    src_row = pad_start[None, :] + run_base
    run_pieces = run_len // RUN_ROWS
    full_rows = run_pieces * RUN_ROWS
    piece_end = jnp.cumsum(run_pieces, axis=1)
    first_piece = piece_end - run_pieces
    stage_row = first_piece * RUN_ROWS
    max_pieces = -(-(TOP_K * c_tile // RUN_ROWS) // PIECE_GROUP) * PIECE_GROUP
    p_ids = jnp.arange(max_pieces, dtype=jnp.int32)
    in_run = (p_ids[None, :, None] >= first_piece[:, None, :]) & (p_ids[None, :, None] < piece_end[:, None, :])
    piece_row = jnp.sum(jnp.where(in_run, (src_row - stage_row)[:, None, :], 0), axis=2)
    piece_row = piece_row + p_ids[None, :] * RUN_ROWS
    n_pieces = piece_end[:, -1:]
    tile_parity = (jnp.arange(n_ct, dtype=jnp.int32) % 2)[:, None]
    spare_row = (n_blocks - 1) * MOE_BLOCK + (tile_parity * PIECE_GROUP + p_ids[None, :] - n_pieces) * RUN_ROWS
    piece_row = jnp.where(p_ids[None, :] < n_pieces, piece_row, jnp.clip(spare_row, 0, n_rows - RUN_ROWS))
    n_groups = (n_pieces + PIECE_GROUP - 1) // PIECE_GROUP
    meta_w = -(-(max_pieces + 1) // LANES) * LANES
    meta = jnp.concatenate([n_groups, piece_row,
                            jnp.zeros((n_ct, meta_w - 1 - max_pieces), jnp.int32)], axis=1).reshape(n_ct, 1, meta_w)
    rest_base = max_pieces * RUN_ROWS
    stage_rows = rest_base + n_exp * RUN_ROWS
    tails = jnp.concatenate([src_row + full_rows, run_len - full_rows,
                             jnp.zeros((n_ct, LANES - 2 * n_exp), jnp.int32)], axis=1).reshape(n_ct, 1, LANES)
    rest_len = run_len - full_rows
    e_row = jnp.arange(n_exp, dtype=jnp.int32)
    rest_cnt, rest_lists = [], []
    for size in REST_SIZES:
        has = (rest_len & size) != 0
        slot_in_list = jnp.cumsum(has, axis=1) - has
        before = rest_len & (7 & ~(2 * size - 1))
        sel = has[:, None, :] & (slot_in_list[:, None, :] == e_row[None, :, None])
        pick_run = lambda v: jnp.sum(jnp.where(sel, v[:, None, :], 0), axis=2)
        rest_lists.append(jnp.stack([pick_run(src_row + full_rows + before),
                                     pick_run(rest_base + e_row[None, :] * RUN_ROWS + before)], axis=2)
                          .reshape(n_ct, 2 * n_exp))
        rest_cnt.append(jnp.sum(has, axis=1, keepdims=True).astype(jnp.int32))
    rest_w = -(-(8 + 2 * n_exp * len(REST_SIZES)) // LANES) * LANES
    rest_cnt.append(n_groups * PIECE_GROUP - n_pieces)
    rest = jnp.concatenate(rest_cnt + [jnp.zeros((n_ct, 8 - len(rest_cnt)), jnp.int32)] + rest_lists
                           + [jnp.zeros((n_ct, rest_w - 8 - 2 * n_exp * len(REST_SIZES)), jnp.int32)],
                           axis=1).reshape(n_ct, 1, rest_w)
    e_ids = e_row[:, None, None]
    hot = top_e[None] == e_ids
    pick = lambda tab: jnp.sum(jnp.where(hot, jnp.repeat(tab.T, c_tile, axis=1)[:, None, :], 0), axis=0)
    in_full = rank < pick(run_base + full_rows)
    e_rest = rest_base + jnp.arange(n_exp, dtype=jnp.int32)[None, :] * RUN_ROWS
    pos = rank + jnp.where(in_full, pick(stage_row - run_base), pick(e_rest - full_rows - run_base))
    slot_tok = (jnp.arange(n_tok, dtype=jnp.int32) // c_tile) % 2
    pos_addr = (pos + slot_tok[None, :] * stage_rows) * SUB
    per_tile = lambda a: a.reshape(TOP_K, n_ct, c_tile).transpose(1, 2, 0).reshape(n_ct, 1, TOP_K * c_tile)
    pos3, gate3 = per_tile(pos_addr), per_tile(gates)
    smem_blk = lambda width, f: pl.BlockSpec((1, 1, width), f, memory_space=pltpu.SMEM)

    x_sorted = pl.pallas_call(
        _dispatch_kernel,
        grid_spec=pltpu.PrefetchScalarGridSpec(
            num_scalar_prefetch=2,
            grid=(n_ct,),
            in_specs=[smem_blk(TOP_K * c_tile, lambda i, lo, hi: (i, 0, 0)),
                      smem_blk(meta_w, lambda i, lo, hi: (i, 0, 0)),
                      smem_blk(meta_w, lambda i, lo, hi: (jnp.maximum(i - 1, 0), 0, 0)),
                      smem_blk(rest_w, lambda i, lo, hi: (i, 0, 0)),
                      smem_blk(rest_w, lambda i, lo, hi: (jnp.maximum(i - 1, 0), 0, 0)),
                      pl.BlockSpec(memory_space=pl.ANY)],
            out_specs=pl.BlockSpec(memory_space=pl.ANY),
            scratch_shapes=[pltpu.VMEM((2, c_tile * SUB, LANES), F32),
                            pltpu.VMEM((2 * stage_rows * SUB, LANES), F32), pltpu.VMEM((SUB, LANES), F32),
                            pltpu.SemaphoreType.DMA((2,)), pltpu.SemaphoreType.DMA((2,)),
                            pltpu.SemaphoreType.DMA((1,))]),
        out_shape=jax.ShapeDtypeStruct((n_rows * SUB, LANES), F32),
        compiler_params=pltpu.CompilerParams(dimension_semantics=("arbitrary",), vmem_limit_bytes=VMEM_LIMIT),
        name="moe_dispatch",
    )(pad_start + counts, pad_end, pos3, meta, meta, rest, rest, h1)

    last_act = lambda i, na: jnp.minimum(i, na[0] - 1)
    e_mat = lambda shape: pl.BlockSpec((None,) + shape, lambda i, be, na: (be[i], 0, 0))
    y_sorted = pl.pallas_call(
        _experts_kernel,
        grid_spec=pltpu.PrefetchScalarGridSpec(
            num_scalar_prefetch=2,
            grid=(n_blocks,),
            in_specs=[pl.BlockSpec((MOE_BLOCK * SUB, LANES), lambda i, be, na: (last_act(i, na), 0)),
                      e_mat((d, d_ff)), e_mat((d, d_ff)), e_mat((1, d_ff)), e_mat((1, d_ff)),
                      e_mat((d_ff, d)), e_mat((1, d))],
            out_specs=pl.BlockSpec((MOE_BLOCK * SUB, LANES), lambda i, be, na: (i, 0)),
            scratch_shapes=[pltpu.VMEM((d, d_ff), BF16), pltpu.VMEM((d, d_ff), BF16),
                            pltpu.VMEM((d_ff, d), BF16)]),
        out_shape=jax.ShapeDtypeStruct((n_rows * SUB, LANES), F32),
        compiler_params=pltpu.CompilerParams(dimension_semantics=("arbitrary",), vmem_limit_bytes=VMEM_LIMIT),
        name="moe_experts",
    )(block_e, n_act, x_sorted,
      moe_w_gate[0].astype(F32), moe_w_up[0].astype(F32),
      moe_b_gate[0].reshape(n_exp, 1, d_ff).astype(F32), moe_b_up[0].reshape(n_exp, 1, d_ff).astype(F32),
      moe_w_down[0].astype(F32), moe_b_down[0].reshape(n_exp, 1, d).astype(F32))

    out = pl.pallas_call(
        functools.partial(_combine_kernel, alpha, n_exp),
        grid=(n_ct,),
        in_specs=[smem_blk(TOP_K * c_tile, lambda i: (i, 0, 0)),
                  smem_blk(TOP_K * c_tile, lambda i: (i, 0, 0)),
                  smem_blk(meta_w, lambda i: (i, 0, 0)),
                  smem_blk(meta_w, lambda i: (jnp.minimum(i + 1, n_ct - 1), 0, 0)),
                  smem_blk(LANES, lambda i: (i, 0, 0)),
                  smem_blk(LANES, lambda i: (jnp.minimum(i + 1, n_ct - 1), 0, 0)),
                  pl.BlockSpec((c_tile * SUB, LANES), lambda i: (i, 0)),
                  _full((1, d)), _full((1, d)),
                  pl.BlockSpec(memory_space=pl.ANY)],
        out_specs=pl.BlockSpec((c_tile, d), lambda i: (i, 0)),
        out_shape=jax.ShapeDtypeStruct((n_tok, d), F32),
        scratch_shapes=[pltpu.VMEM((2 * stage_rows * SUB, LANES), F32), pltpu.VMEM((c_tile * SUB, LANES), F32),
                        pltpu.SemaphoreType.DMA((2,))],
        compiler_params=pltpu.CompilerParams(dimension_semantics=("arbitrary",), vmem_limit_bytes=VMEM_LIMIT),
        name="moe_combine",
    )(pos3, gate3, meta, meta, tails, tails, h1, row(ln2_g[0]), row(ln2_b[0]), y_sorted)
    return out.reshape(batch, seq, d).astype(x.dtype)
```

```python
import functools

import numpy as np
import jax
import jax.numpy as jnp
from jax import lax
from jax.experimental import pallas as pl
from jax.experimental.pallas import tpu as pltpu

F32 = jnp.float32
BF16 = jnp.bfloat16

CHUNK = 64
N_META = 16
PAIR = 2 * CHUNK
GLA_HEADS = 4
GLA_DK = 64
GLA_DV = 128
GLA_KEY = GLA_HEADS * GLA_DK
GLA_VAL = GLA_HEADS * GLA_DV
GLA_RANK = 16
GLA_TAU = 16.0
SSD_INNER = 512
SSD_HEADS = 8
SSD_HEADDIM = 64
SSD_GROUPS = 2
SSD_STATE = 128
SSD_GROUP_W = SSD_INNER // SSD_GROUPS
SSD_CONV = 4
SSD_CONV_CH = SSD_INNER + 2 * SSD_GROUPS * SSD_STATE
TOP_K = 4
SWIGLU_LIMIT = 7.0
SWIGLU_ALPHA = 1.702
MOE_BLOCK = 512
LN_EPS = 1e-5
RMS_EPS = 1e-6
LANES = 128

C_Q, C_K, C_V, C_OG, C_Z, C_XBC, C_MISC, C_END = 0, 256, 512, 1024, 1536, 2048, 3072, 3200
MISC_A1 = 0
MISC_DT = 16
CONV_PAD = 8

MIXER_TILE = 256
MIXER_SEQS = 2
SEQ_SKEW = 5
VMEM_LIMIT = 56 * 1024 * 1024


def _dot(a, b):
    return jnp.dot(a, b, preferred_element_type=F32)


def _split_hi_lo(x):
    hi = x.astype(BF16)
    lo = (x - hi.astype(F32)).astype(BF16)
    return hi, lo


def _log_sigmoid(x):
    return jnp.minimum(x, 0.0) - jnp.log1p(jnp.exp(-jnp.abs(x)))


def _softplus(x):
    return jnp.maximum(x, 0.0) + jnp.log1p(jnp.exp(-jnp.abs(x)))


def _silu(x):
    return x * jax.nn.sigmoid(x)


def _layer_norm(t, g, b):
    mu = jnp.mean(t, axis=-1, keepdims=True)
    tc = t - mu
    var = jnp.mean(tc * tc, axis=-1, keepdims=True)
    return tc * lax.rsqrt(var + LN_EPS) * g + b


def _group_rms(t, g, width):
    outs = []
    for s in range(0, t.shape[-1], width):
        seg = t[:, s:s + width]
        ms = jnp.mean(seg * seg, axis=-1, keepdims=True)
        outs.append(seg * lax.rsqrt(ms + RMS_EPS) * g[:, s:s + width])
    return jnp.concatenate(outs, axis=-1)


def _run(steps):
    try:
        while True:
            next(steps)
    except StopIteration as done:
        return done.value


def _mixer_tile(h, tile, valid_col, valid_row, p, s_gla, s_ssd, xbc_buf, mix_buf, need_out):
    hb = h.astype(BF16)
    w_in = p["w_in"]

    def proj(lo, hi):
        r = _dot(hb, w_in[:, lo:hi])
        if valid_col is not None:
            r = jnp.where(valid_col, r, 0.0)
        return r

    misc = proj(C_MISC, C_END)

    la = _log_sigmoid(_dot(misc.astype(BF16), p["w_a2"][...]) + p["b_a"][...]) * (1.0 / GLA_TAU)
    if valid_col is not None:
        la = jnp.where(valid_col, la, 0.0)
    la_hi, la_lo = _split_hi_lo(la)
    r = _dot(p["uo"][...], jnp.concatenate([la_hi, la_lo], axis=1))
    dec_exp = r[0:tile, 0:GLA_KEY] + r[0:tile, GLA_KEY:]
    tot_rows = r[tile:, 0:GLA_KEY] + r[tile:, GLA_KEY:]
    gla_dec_t = jnp.exp(tot_rows.T)
    kd = proj(C_K, C_V) * jnp.exp(dec_exp)
    v_bf = proj(C_V, C_OG).astype(BF16)

    yield
    misc_t = misc.T
    dt = _softplus(misc_t[MISC_DT:MISC_DT + SSD_HEADS, :] + p["dt_bias"][...])
    if valid_row is not None:
        dt = jnp.where(valid_row, dt, 0.0)
    dta = dt * (-jnp.exp(p["a_log"][...]))
    d_hi, d_lo = _split_hi_lo(dta)
    r2 = _dot(jnp.concatenate([d_hi, d_lo], axis=0), p["uto"][...])
    w = jnp.exp(r2[0:8, 0:tile] + r2[8:16, 0:tile]) * dt
    ssd_dec = jnp.exp(r2[0:8, tile:] + r2[8:16, tile:])
    w_hi, w_lo = _split_hi_lo(w)
    c_hi, c_lo = _split_hi_lo(ssd_dec)
    packed = jnp.concatenate(
        [w_hi.astype(F32), w_lo.astype(F32), c_hi.astype(F32), c_lo.astype(F32),
         jnp.zeros((LANES - 4 * SSD_HEADS, tile), F32)], axis=0)
    expanded = _dot(packed.T.astype(BF16), p["eexp"][...])
    w_exp = expanded[:, 0:SSD_INNER]
    ssd_dec_exp = expanded[:, SSD_INNER:]

    yield
    xbc_buf[CONV_PAD:CONV_PAD + tile, :] = proj(C_XBC, C_MISC)
    acc = p["conv_b"][...] + p["conv_w"][0:1, :] * xbc_buf[pl.ds(CONV_PAD - 3, tile), :]
    for j in range(1, SSD_CONV):
        acc = acc + p["conv_w"][j:j + 1, :] * xbc_buf[pl.ds(CONV_PAD - 3 + j, tile), :]
    xbc_buf[0:CONV_PAD, :] = xbc_buf[tile:tile + CONV_PAD, :]
    xa = _silu(acc)
    xs = xa[:, 0:SSD_INNER]
    bm = xa[:, SSD_INNER:SSD_INNER + SSD_GROUPS * SSD_STATE]
    cm_bf = xa[:, SSD_INNER + SSD_GROUPS * SSD_STATE:].astype(BF16)
    xw_bf = (xs * w_exp).astype(BF16)
    yield

    if need_out:
        q = proj(C_Q, C_K) * (GLA_DK ** -0.5)
        head_of_lane = lax.broadcasted_iota(jnp.int32, (CHUNK, GLA_KEY), 1) // GLA_DK

    lane = lax.broadcasted_iota(jnp.int32, (1, PAIR), 1)
    for pr in range(tile // PAIR):
        rows = slice(pr * PAIR, (pr + 1) * PAIR)
        kd_t = kd[rows].T
        bm_t = bm[rows].T
        for half in range(2):
            c = 2 * pr + half
            r0 = c * CHUNK
            sel = (lane >= half * CHUNK) & (lane < (half + 1) * CHUNK)
            kd_m = jnp.where(sel, kd_t, 0.0).astype(BF16)
            upd = jnp.concatenate(
                [_dot(kd_m[hd * GLA_DK:(hd + 1) * GLA_DK], v_bf[rows, hd * GLA_DV:(hd + 1) * GLA_DV])
                 for hd in range(GLA_HEADS)], axis=0)
            s_new = gla_dec_t[:, c:c + 1] * s_gla[...] + upd
            s_gla[...] = s_new
            bm_m = jnp.where(sel, bm_t, 0.0).astype(BF16)
            ssd_new = []
            for g in range(SSD_GROUPS):
                gl = slice(g * SSD_GROUP_W, (g + 1) * SSD_GROUP_W)
                gr = slice(g * SSD_STATE, (g + 1) * SSD_STATE)
                u = _dot(bm_m[gr], xw_bf[rows, gl])
                sg = ssd_dec_exp[r0:r0 + 1, gl] * s_ssd[gr, :] + u
                s_ssd[gr, :] = sg
                ssd_new.append(sg)
            if need_out:
                qc = q[r0:r0 + CHUNK]
                qm = jnp.concatenate(
                    [jnp.where(head_of_lane == hd, qc, 0.0) for hd in range(GLA_HEADS)],
                    axis=0).astype(BF16)
                o = _dot(qm, s_new.astype(BF16))
                for hd in range(GLA_HEADS):
                    mix_buf[r0:r0 + CHUNK, hd * GLA_DV:(hd + 1) * GLA_DV] = o[hd * CHUNK:(hd + 1) * CHUNK]
                for g in range(SSD_GROUPS):
                    y = _dot(cm_bf[r0:r0 + CHUNK, g * SSD_STATE:(g + 1) * SSD_STATE],
                             ssd_new[g].astype(BF16))
                    mix_buf[r0:r0 + CHUNK, GLA_VAL + g * SSD_GROUP_W:GLA_VAL + (g + 1) * SSD_GROUP_W] = y
        yield

    if not need_out:
        return None
    o_gla = _group_rms(mix_buf[:, 0:GLA_VAL], p["gla_norm_g"][...], GLA_DV) * _silu(proj(C_OG, C_Z))
    yield
    y = (mix_buf[:, GLA_VAL:] + xs * p["ssd_d"][...]) * _silu(proj(C_Z, C_XBC))
    y = _group_rms(y, p["ssd_norm_g"][...], SSD_GROUP_W)
    return o_gla, y


_MIX_PARAMS = ("w_in", "w_a2", "b_a", "conv_w", "conv_b", "dt_bias", "a_log", "uo", "uto", "eexp")
_OUT_PARAMS = ("gla_norm_g", "ssd_d", "ssd_norm_g")


def _meta_kernel(x_ref, ln_g, ln_b, *rest):
    np_ = len(_MIX_PARAMS)
    p = dict(zip(_MIX_PARAMS, rest[:np_]))
    s_gla_out, s_ssd_out, tail_out, xbc_buf = rest[np_:]
    tile = x_ref.shape[0]
    s_gla_out[...] = jnp.zeros_like(s_gla_out)
    s_ssd_out[...] = jnp.zeros_like(s_ssd_out)
    xbc_buf[0:CONV_PAD, :] = jnp.zeros((CONV_PAD, xbc_buf.shape[1]), F32)
    first_valid = tile - N_META
    valid_col = lax.broadcasted_iota(jnp.int32, (tile, 1), 0) >= first_valid
    valid_row = lax.broadcasted_iota(jnp.int32, (1, tile), 1) >= first_valid
    h = _layer_norm(x_ref[...], ln_g[...], ln_b[...])
    _run(_mixer_tile(h, tile, valid_col, valid_row, p, s_gla_out, s_ssd_out, xbc_buf, None, False))
    tail_out[...] = xbc_buf[0:CONV_PAD, :]


def _mixer_kernel(alpha, n_exp, x_ref, ln_g, ln_b, *rest):
    np_ = len(_MIX_PARAMS)
    p = dict(zip(_MIX_PARAMS, rest[:np_]))
    rest = rest[np_:]
    p.update(zip(_OUT_PARAMS, rest[:3]))
    (w_out, ln1_g, ln1_b, rw, rb, su, s_gla0, s_ssd0, tail0,
     h1_out, topi_out, gate_out, rank_out, cnt_out, base_out, tcnt_out) = rest[3:19]
    n_sub, tile = x_ref.shape[0], x_ref.shape[1]
    scratch = rest[19:]
    s_gla, s_ssd, xbc_buf, mix_buf = (scratch[k * n_sub:(k + 1) * n_sub] for k in range(4))
    run_cnt = scratch[4 * n_sub]
    b, j = pl.program_id(0), pl.program_id(1)

    @pl.when(j == 0)
    def _():
        for s in range(n_sub):
            s_gla[s][...] = s_gla0[...]
            s_ssd[s][...] = s_ssd0[...]
            xbc_buf[s][0:CONV_PAD, :] = tail0[...]

    @pl.when((b == 0) & (j == 0))
    def _():
        run_cnt[...] = jnp.zeros_like(run_cnt)

    count = [run_cnt[...]]
    pad_rows = topi_out.shape[1] - TOP_K

    def sequence(s):
        h = _layer_norm(x_ref[s], ln_g[...], ln_b[...])
        o_gla, y = yield from _mixer_tile(h, tile, None, None, p, s_gla[s], s_ssd[s], xbc_buf[s], mix_buf[s], True)
        yield
        mix = _dot(o_gla.astype(BF16), w_out[0:GLA_VAL, :]) + _dot(y.astype(BF16), w_out[GLA_VAL:, :])
        h1 = _layer_norm(alpha * h + mix, ln1_g[...], ln1_b[...])
        _store_rows(h1_out.at[s], h1)
        yield

        h_hi, h_lo = _split_hi_lo(h1)
        p_hi = _dot(h_hi, rw[...])
        logits = p_hi + (pltpu.roll(p_hi, LANES - n_exp, 1) + _dot(h_lo, rw[...]))
        lt = logits.T[0:n_exp, :] + rb[...]
        e_iota = lax.broadcasted_iota(jnp.int32, (n_exp, tile), 0).astype(F32)
        work = lt
        vals, hots = [], []
        for k in range(TOP_K):
            m = jnp.max(work, axis=0, keepdims=True)
            idx = jnp.min(jnp.where(work == m, e_iota, float(n_exp)), axis=0, keepdims=True)
            hot = e_iota == idx
            work = jnp.where(hot, -jnp.inf, work)
            vals.append(m)
            hots.append(hot)
            topi_out[s, k:k + 1, :] = idx.astype(jnp.int32)
        exps = [jnp.exp(vk - vals[0]) for vk in vals]
        denom = exps[0] + exps[1] + exps[2] + exps[3]
        for k in range(TOP_K):
            gate_out[s, k:k + 1, :] = exps[k] / denom
        member = (hots[0] | hots[1] | hots[2] | hots[3]).astype(F32).astype(BF16)
        yield
        before = _dot(member, su[...]) + count[0][:, 0:1]
        for k in range(TOP_K):
            rk = jnp.sum(jnp.where(hots[k], before, 0.0), axis=0, keepdims=True)
            rank_out[s, k:k + 1, :] = rk.astype(jnp.int32)
        topi_out[s, TOP_K:, :] = jnp.zeros((pad_rows, tile), jnp.int32)
        gate_out[s, TOP_K:, :] = jnp.zeros((pad_rows, tile), F32)
        rank_out[s, TOP_K:, :] = jnp.zeros((pad_rows, tile), jnp.int32)
        tile_count = _dot(member, jnp.ones((tile, LANES), BF16))
        base_out[s, 0] = count[0]
        tcnt_out[s, 0] = tile_count
        count[0] = count[0] + tile_count

    pending = [sequence(s) for s in range(n_sub)]
    for lead in range(1, n_sub):
        for _ in range(SEQ_SKEW):
            for steps in pending[:lead]:
                next(steps)
    while pending:
        for steps in list(pending):
            try:
                next(steps)
            except StopIteration:
                pending.remove(steps)
    run_cnt[...] = count[0]
    cnt_out[...] = count[0]


def _clamped_swiglu(a, u):
    a = jnp.minimum(a, SWIGLU_LIMIT)
    u = jnp.clip(u, -SWIGLU_LIMIT, SWIGLU_LIMIT)
    return a * jax.nn.sigmoid(SWIGLU_ALPHA * a) * (u + 1.0)


SUB = 8


def _load_rows(ref, n_rows):
    return jnp.concatenate([ref[pl.ds(j, n_rows, stride=SUB), :] for j in range(SUB)], axis=1)


def _store_rows(ref, val):
    for j in range(SUB):
        ref[pl.ds(j, val.shape[0], stride=SUB), :] = val[:, j * LANES:(j + 1) * LANES]


def _tile_rows(row):
    return pl.ds(pl.multiple_of(row * SUB, SUB), SUB)


RUN_ROWS = 8
PIECE_GROUP = 4
REST_SIZES = (4, 2, 1)


def _run_rows(row):
    return pl.ds(pl.multiple_of(row * SUB, SUB), RUN_ROWS * SUB)


def _stage_tile(addr):
    return pl.ds(pl.multiple_of(addr, SUB), SUB)


def _for_pieces(pieces, fn):
    def body(g, _):
        for u in range(PIECE_GROUP):
            p = g * PIECE_GROUP + u
            fn(p * RUN_ROWS, pieces[0, 0, 1 + p])
        return 0
    lax.fori_loop(0, pieces[0, 0, 0], body, 0)


def _dispatch_kernel(fill_lo, fill_hi, pos, meta_cur, meta_prv, tail_cur, tail_prv, h_hbm, xs_hbm,
                     tbuf, stage, zrow, lsem, rsem, zsem):
    i = pl.program_id(0)
    n = pl.num_programs(0)
    tile = pos.shape[-1] // TOP_K
    slot = i % 2
    n_exp = fill_lo.shape[0]
    stage_rows = stage.shape[0] // (2 * SUB)
    rest_base = stage_rows - n_exp * RUN_ROWS

    def tile_load(j, s):
        return pltpu.make_async_copy(h_hbm.at[pl.ds(j * tile * SUB, tile * SUB), :], tbuf.at[s], lsem.at[s])

    def rows_copy(s, stage_row, sorted_row, n_rows):
        src = stage.at[pl.ds(pl.multiple_of((s * stage_rows + stage_row) * SUB, SUB), n_rows * SUB), :]
        dst = xs_hbm.at[pl.ds(pl.multiple_of(sorted_row * SUB, SUB), n_rows * SUB), :]
        return pltpu.make_async_copy(src, dst, rsem.at[s])

    def for_rest(rest, s, wait):
        for c, size in enumerate(REST_SIZES):
            first = 8 + 2 * n_exp * c

            def body(q, _, first=first, size=size):
                if wait:
                    rows_copy(s, 0, 0, size).wait()
                else:
                    rows_copy(s, rest[0, 0, first + 2 * q + 1], rest[0, 0, first + 2 * q], size).start()
                return 0
            lax.fori_loop(0, rest[0, 0, c], body, 0)

    def zero_copy(dst_row):
        return pltpu.make_async_copy(zrow, xs_hbm.at[_tile_rows(dst_row), :], zsem.at[0])

    @pl.when(i == 0)
    def _():
        tile_load(0, 0).start()
        zrow[...] = jnp.zeros_like(zrow)
        stage[...] = jnp.zeros_like(stage)
        for wait in (False, True):
            def per_expert(e, _, wait=wait):
                def body(row, _):
                    if wait:
                        zero_copy(0).wait()
                    else:
                        zero_copy(row).start()
                    return 0
                lax.fori_loop(fill_lo[e], fill_hi[e], body, 0)
                return 0
            lax.fori_loop(0, n_exp, per_expert, 0)
        tail_lo = fill_hi[n_exp - 1]
        tail_pieces = (xs_hbm.shape[0] // SUB - tail_lo) // RUN_ROWS
        for wait in (False, True):
            def tail(c, _, wait=wait):
                cp = pltpu.make_async_copy(stage.at[_run_rows(0), :], xs_hbm.at[_run_rows(tail_lo + c * RUN_ROWS), :],
                                           zsem.at[0])
                if wait:
                    cp.wait()
                else:
                    cp.start()
                return 0
            lax.fori_loop(0, tail_pieces, tail, 0)

    @pl.when(i + 1 < n)
    def _():
        tile_load(i + 1, 1 - slot).start()

    tile_load(i, slot).wait()

    def pack(t, _):
        row = tbuf[slot, _tile_rows(t), :]
        for k in range(TOP_K):
            stage[_stage_tile(pos[0, 0, TOP_K * t + k]), :] = row
        return 0
    lax.fori_loop(0, tile, pack, 0, unroll=8)

    _for_pieces(meta_cur, lambda off, row: rows_copy(slot, off, row, RUN_ROWS).start())
    for_rest(tail_cur, slot, False)

    def wait_tile(rest, s):
        rows_copy(s, 0, 0, TOP_K * tile).wait()

        def body(q, _):
            rows_copy(s, 0, 0, RUN_ROWS).wait()
            return 0
        lax.fori_loop(0, rest[0, 0, len(REST_SIZES)], body, 0)

    @pl.when(i > 0)
    def _():
        wait_tile(tail_prv, 1 - slot)

    @pl.when(i == n - 1)
    def _():
        wait_tile(tail_cur, slot)


def _experts_kernel(blk_e, n_act, x_ref, wg, wu, bg, bu, wd, bd, y_out, wg_bf, wu_bf, wd_bf):
    i = pl.program_id(0)
    active = i < n_act[0]
    new_expert = (i == 0) | (blk_e[i] != blk_e[jnp.maximum(i - 1, 0)])

    @pl.when(active & new_expert)
    def _():
        wg_bf[...] = wg[...].astype(BF16)
        wu_bf[...] = wu[...].astype(BF16)
        wd_bf[...] = wd[...].astype(BF16)

    @pl.when(active)
    def _():
        xb = _load_rows(x_ref, x_ref.shape[0] // SUB).astype(BF16)
        hid = _clamped_swiglu(_dot(xb, wg_bf[...]) + bg[...], _dot(xb, wu_bf[...]) + bu[...])
        _store_rows(y_out, _dot(hid.astype(BF16), wd_bf[...]) + bd[...])

    @pl.when(i >= n_act[0])
    def _():
        y_out[...] = jnp.zeros_like(y_out)


def _combine_kernel(alpha, n_exp, pos, gate, meta_cur, meta_nxt, tail_cur, tail_nxt, h1, ln_g, ln_b, y_hbm, out,
                    stage, ff_rows, sem):
    i = pl.program_id(0)
    n = pl.num_programs(0)
    tile = out.shape[0]
    slot = i % 2
    stage_rows = stage.shape[0] // (2 * SUB)
    rest_base = stage_rows - n_exp * RUN_ROWS

    def run_copy(s, stage_row, sorted_row):
        return pltpu.make_async_copy(y_hbm.at[_run_rows(sorted_row), :],
                                     stage.at[_run_rows(s * stage_rows + stage_row), :], sem.at[s])

    def fetch(meta, tails, s, wait):
        act = (lambda cp: cp.wait()) if wait else (lambda cp: cp.start())
        _for_pieces(meta, lambda off, row: act(run_copy(s, 0 if wait else off, 0 if wait else row)))

        if wait:
            pltpu.make_async_copy(y_hbm.at[pl.ds(0, n_exp * RUN_ROWS * SUB), :],
                                  stage.at[pl.ds(0, n_exp * RUN_ROWS * SUB), :], sem.at[s]).wait()
            return

        def rest(g, _):
            for u in range(PIECE_GROUP):
                e = g * PIECE_GROUP + u
                run_copy(s, rest_base + e * RUN_ROWS, tails[0, 0, e]).start()
            return 0
        lax.fori_loop(0, n_exp // PIECE_GROUP, rest, 0)

    @pl.when(i == 0)
    def _():
        fetch(meta_cur, tail_cur, 0, False)

    @pl.when(i + 1 < n)
    def _():
        fetch(meta_nxt, tail_nxt, 1 - slot, False)

    fetch(meta_cur, tail_cur, slot, True)

    def gather(t, _):
        acc = gate[0, 0, TOP_K * t] * stage[_stage_tile(pos[0, 0, TOP_K * t]), :]
        for k in range(1, TOP_K):
            acc = acc + gate[0, 0, TOP_K * t + k] * stage[_stage_tile(pos[0, 0, TOP_K * t + k]), :]
        ff_rows[_tile_rows(t), :] = acc
        return 0
    lax.fori_loop(0, tile, gather, 0, unroll=8)

    out[...] = _layer_norm(alpha * _load_rows(h1, tile) + _load_rows(ff_rows, tile), ln_g[...], ln_b[...])


def _block_constants(tile):
    t = np.arange(tile)
    same = (t[:, None] // CHUNK) == (t[None, :] // CHUNK)
    later = same & (t[None, :] > t[:, None])
    chunk_rows = np.zeros((LANES, tile), np.float32)
    chunk_rows[t // CHUNK, t] = 1.0
    uo = np.concatenate([later.astype(np.float32), chunk_rows], axis=0)
    uto = np.concatenate([later.T.astype(np.float32), same.astype(np.float32)], axis=1)
    return jnp.asarray(uo, BF16), jnp.asarray(uto, BF16)


def _expand_constant():
    e = np.zeros((LANES, 2 * SSD_INNER), np.float32)
    for part in range(4):
        for hd in range(SSD_HEADS):
            base = (part // 2) * SSD_INNER + hd * SSD_HEADDIM
            e[part * SSD_HEADS + hd, base:base + SSD_HEADDIM] = 1.0
    return jnp.asarray(e, BF16)


def _full(shape):
    return pl.BlockSpec(shape, lambda *_: (0,) * len(shape))


def kernel(x, meta_tokens, ln_in_g, ln_in_b, w_in, gla_w_a2, gla_b_a, gla_norm_g, ssd_conv_w, ssd_conv_b,
           ssd_dt_bias, ssd_a_log, ssd_d, ssd_norm_g, w_out, ln1_g, ln1_b, router_w, router_b, moe_w_gate,
           moe_w_up, moe_b_gate, moe_b_up, moe_w_down, moe_b_down, ln2_g, ln2_b):
    batch, seq, d = x.shape
    depth = w_in.shape[0]
    assert depth == 1, "single-layer stack"
    n_exp = router_w.shape[-1]
    d_ff = moe_w_gate.shape[-1]
    alpha = (2.0 * depth) ** 0.25
    tile = MIXER_TILE
    assert seq % tile == 0 and d == 1024
    n_tok = batch * seq
    row = lambda a: a.reshape(1, -1).astype(F32)

    wi = w_in[0]
    o_a1 = 1536
    o_z = o_a1 + GLA_RANK
    o_xbc = o_z + SSD_INNER
    o_dt = o_xbc + SSD_CONV_CH
    misc_w = jnp.zeros((d, LANES), F32)
    misc_w = misc_w.at[:, MISC_A1:MISC_A1 + GLA_RANK].set(wi[:, o_a1:o_z])
    misc_w = misc_w.at[:, MISC_DT:MISC_DT + SSD_HEADS].set(wi[:, o_dt:o_dt + SSD_HEADS])
    w_in_r = jnp.concatenate([wi[:, 0:o_a1], wi[:, o_z:o_xbc], wi[:, o_xbc:o_dt], misc_w], axis=1).astype(BF16)
    w_a2 = jnp.zeros((LANES, GLA_KEY), F32).at[MISC_A1:MISC_A1 + GLA_RANK].set(gla_w_a2[0]).astype(BF16)
    mix_params = dict(
        w_in=w_in_r, w_a2=w_a2, b_a=row(gla_b_a[0]), conv_w=ssd_conv_w[0].astype(F32),
        conv_b=row(ssd_conv_b[0]), dt_bias=ssd_dt_bias[0].reshape(-1, 1).astype(F32),
        a_log=ssd_a_log[0].reshape(-1, 1).astype(F32), eexp=_expand_constant())
    out_params = dict(gla_norm_g=row(gla_norm_g[0]), ssd_d=row(jnp.repeat(ssd_d[0], SSD_HEADDIM)),
                      ssd_norm_g=row(ssd_norm_g[0]))

    def mix_args(t):
        uo, uto = _block_constants(t)
        vals = dict(mix_params, uo=uo, uto=uto)
        return [vals[k] for k in _MIX_PARAMS]

    m_tile = PAIR
    x_meta = jnp.concatenate([jnp.zeros((m_tile - N_META, d), F32), meta_tokens.astype(F32)], axis=0)
    meta_in = [x_meta, row(ln_in_g), row(ln_in_b)] + mix_args(m_tile)
    s_gla0, s_ssd0, tail0 = pl.pallas_call(
        _meta_kernel,
        out_shape=(jax.ShapeDtypeStruct((GLA_KEY, GLA_DV), F32),
                   jax.ShapeDtypeStruct((SSD_GROUPS * SSD_STATE, SSD_GROUP_W), F32),
                   jax.ShapeDtypeStruct((CONV_PAD, SSD_CONV_CH), F32)),
        scratch_shapes=[pltpu.VMEM((CONV_PAD + m_tile, SSD_CONV_CH), F32)],
        compiler_params=pltpu.CompilerParams(vmem_limit_bytes=VMEM_LIMIT),
        name="meta_state",
    )(*meta_in)

    n_j = seq // tile
    rw_hi, rw_lo = _split_hi_lo(router_w[0].astype(F32))
    rw = jnp.zeros((d, LANES), BF16).at[:, 0:n_exp].set(rw_hi).at[:, n_exp:2 * n_exp].set(rw_lo)
    rb = router_b[0].reshape(n_exp, 1).astype(F32)
    su = jnp.asarray(np.triu(np.ones((tile, tile), np.float32), 1), BF16)
    args = ([x, row(ln_in_g), row(ln_in_b)] + mix_args(tile) + [out_params[k] for k in _OUT_PARAMS]
            + [w_out[0].astype(BF16), row(ln1_g[0]), row(ln1_b[0]), rw, rb, su, s_gla0, s_ssd0, tail0])
    n_sub = MIXER_SEQS
    assert batch % n_sub == 0
    in_specs = [pl.BlockSpec((n_sub, tile, d), lambda b, j: (b, j, 0))] + [_full(a.shape) for a in args[1:]]
    tok_blk = pl.BlockSpec((n_sub, 8, tile), lambda b, j: (b, 0, j))
    tile_tab = pl.BlockSpec((n_sub, 1, n_exp, LANES), lambda b, j: (b, j, 0, 0))
    h1, topi, gates, rank, cnt, base, tile_cnt = pl.pallas_call(
        functools.partial(_mixer_kernel, alpha, n_exp),
        grid=(batch // n_sub, n_j),
        in_specs=in_specs,
        out_specs=(pl.BlockSpec((n_sub, tile * SUB, LANES), lambda b, j: (b, j, 0)),
                   tok_blk, tok_blk, tok_blk, _full((n_exp, LANES)), tile_tab, tile_tab),
        out_shape=(jax.ShapeDtypeStruct((batch, seq * SUB, LANES), F32),
                   jax.ShapeDtypeStruct((batch, 8, seq), jnp.int32),
                   jax.ShapeDtypeStruct((batch, 8, seq), F32),
                   jax.ShapeDtypeStruct((batch, 8, seq), jnp.int32),
                   jax.ShapeDtypeStruct((n_exp, LANES), F32),
                   jax.ShapeDtypeStruct((batch, n_j, n_exp, LANES), F32),
                   jax.ShapeDtypeStruct((batch, n_j, n_exp, LANES), F32)),
        scratch_shapes=([pltpu.VMEM((GLA_KEY, GLA_DV), F32)] * n_sub
                        + [pltpu.VMEM((SSD_GROUPS * SSD_STATE, SSD_GROUP_W), F32)] * n_sub
                        + [pltpu.VMEM((CONV_PAD + tile, SSD_CONV_CH), F32)] * n_sub
                        + [pltpu.VMEM((tile, d), F32)] * n_sub
                        + [pltpu.VMEM((n_exp, LANES), F32)]),
        compiler_params=pltpu.CompilerParams(dimension_semantics=("arbitrary", "arbitrary"),
                                             vmem_limit_bytes=VMEM_LIMIT),
        name="mixer",
    )(*args)
    h1 = h1.reshape(n_tok * SUB, LANES)
    per_tok = lambda a: a[:, :TOP_K].transpose(1, 0, 2).reshape(TOP_K, n_tok)
    top_e, gates, rank = per_tok(topi), per_tok(gates), per_tok(rank)

    counts = cnt[:, 0].astype(jnp.int32)
    padded = (counts + RUN_ROWS + MOE_BLOCK - 1) // MOE_BLOCK * MOE_BLOCK
    pad_end = jnp.cumsum(padded)
    pad_start = pad_end - padded
    n_blocks = -(-(n_tok * TOP_K) // MOE_BLOCK) + n_exp + 1
    n_rows = n_blocks * MOE_BLOCK
    blk_lo = jnp.arange(n_blocks, dtype=jnp.int32) * MOE_BLOCK
    block_e = jnp.minimum(jnp.sum((blk_lo[:, None] >= pad_end[None, :]).astype(jnp.int32), axis=1), n_exp - 1)
    n_act = (pad_end[-1] // MOE_BLOCK).astype(jnp.int32).reshape(1)
    c_tile = tile
    n_ct = n_tok // c_tile
    run_base = base[..., 0].astype(jnp.int32).reshape(n_ct, n_exp)
    run_len = tile_cnt[..., 0].astype(jnp.int32).reshape(n_ct, n_exp)
    src_row = pad_start[None, :] + run_base
    run_pieces = run_len // RUN_ROWS
    full_rows = run_pieces * RUN_ROWS
    piece_end = jnp.cumsum(run_pieces, axis=1)
    first_piece = piece_end - run_pieces
    stage_row = first_piece * RUN_ROWS
    max_pieces = -(-(TOP_K * c_tile // RUN_ROWS) // PIECE_GROUP) * PIECE_GROUP
    p_ids = jnp.arange(max_pieces, dtype=jnp.int32)
    in_run = (p_ids[None, :, None] >= first_piece[:, None, :]) & (p_ids[None, :, None] < piece_end[:, None, :])
    piece_row = jnp.sum(jnp.where(in_run, (src_row - stage_row)[:, None, :], 0), axis=2)
    piece_row = piece_row + p_ids[None, :] * RUN_ROWS
    n_pieces = piece_end[:, -1:]
    tile_parity = (jnp.arange(n_ct, dtype=jnp.int32) % 2)[:, None]
    spare_row = (n_blocks - 1) * MOE_BLOCK + (tile_parity * PIECE_GROUP + p_ids[None, :] - n_pieces) * RUN_ROWS
    piece_row = jnp.where(p_ids[None, :] < n_pieces, piece_row, jnp.clip(spare_row, 0, n_rows - RUN_ROWS))
    n_groups = (n_pieces + PIECE_GROUP - 1) // PIECE_GROUP
    meta_w = -(-(max_pieces + 1) // LANES) * LANES
    meta = jnp.concatenate([n_groups, piece_row,
                            jnp.zeros((n_ct, meta_w - 1 - max_pieces), jnp.int32)], axis=1).reshape(n_ct, 1, meta_w)
    rest_base = max_pieces * RUN_ROWS
    stage_rows = rest_base + n_exp * RUN_ROWS
    tails = jnp.concatenate([src_row + full_rows, run_len - full_rows,
                             jnp.zeros((n_ct, LANES - 2 * n_exp), jnp.int32)], axis=1).reshape(n_ct, 1, LANES)
    rest_len = run_len - full_rows
    e_row = jnp.arange(n_exp, dtype=jnp.int32)
    rest_cnt, rest_lists = [], []
    for size in REST_SIZES:
        has = (rest_len & size) != 0
        slot_in_list = jnp.cumsum(has, axis=1) - has
        before = rest_len & (7 & ~(2 * size - 1))
        sel = has[:, None, :] & (slot_in_list[:, None, :] == e_row[None, :, None])
        pick_run = lambda v: jnp.sum(jnp.where(sel, v[:, None, :], 0), axis=2)
        rest_lists.append(jnp.stack([pick_run(src_row + full_rows + before),
                                     pick_run(rest_base + e_row[None, :] * RUN_ROWS + before)], axis=2)
                          .reshape(n_ct, 2 * n_exp))
        rest_cnt.append(jnp.sum(has, axis=1, keepdims=True).astype(jnp.int32))
    rest_w = -(-(8 + 2 * n_exp * len(REST_SIZES)) // LANES) * LANES
    rest_cnt.append(n_groups * PIECE_GROUP - n_pieces)
    rest = jnp.concatenate(rest_cnt + [jnp.zeros((n_ct, 8 - len(rest_cnt)), jnp.int32)] + rest_lists
                           + [jnp.zeros((n_ct, rest_w - 8 - 2 * n_exp * len(REST_SIZES)), jnp.int32)],
                           axis=1).reshape(n_ct, 1, rest_w)
    e_ids = e_row[:, None, None]
    hot = top_e[None] == e_ids
    pick = lambda tab: jnp.sum(jnp.where(hot, jnp.repeat(tab.T, c_tile, axis=1)[:, None, :], 0), axis=0)
    in_full = rank < pick(run_base + full_rows)
    e_rest = rest_base + jnp.arange(n_exp, dtype=jnp.int32)[None, :] * RUN_ROWS
    pos = rank + jnp.where(in_full, pick(stage_row - run_base), pick(e_rest - full_rows - run_base))
    slot_tok = (jnp.arange(n_tok, dtype=jnp.int32) // c_tile) % 2
    pos_addr = (pos + slot_tok[None, :] * stage_rows) * SUB
    per_tile = lambda a: a.reshape(TOP_K, n_ct, c_tile).transpose(1, 2, 0).reshape(n_ct, 1, TOP_K * c_tile)
    pos3, gate3 = per_tile(pos_addr), per_tile(gates)
    smem_blk = lambda width, f: pl.BlockSpec((1, 1, width), f, memory_space=pltpu.SMEM)

    x_sorted = pl.pallas_call(
        _dispatch_kernel,
        grid_spec=pltpu.PrefetchScalarGridSpec(
            num_scalar_prefetch=2,
            grid=(n_ct,),
            in_specs=[smem_blk(TOP_K * c_tile, lambda i, lo, hi: (i, 0, 0)),
                      smem_blk(meta_w, lambda i, lo, hi: (i, 0, 0)),
                      smem_blk(meta_w, lambda i, lo, hi: (jnp.maximum(i - 1, 0), 0, 0)),
                      smem_blk(rest_w, lambda i, lo, hi: (i, 0, 0)),
                      smem_blk(rest_w, lambda i, lo, hi: (jnp.maximum(i - 1, 0), 0, 0)),
                      pl.BlockSpec(memory_space=pl.ANY)],
            out_specs=pl.BlockSpec(memory_space=pl.ANY),
            scratch_shapes=[pltpu.VMEM((2, c_tile * SUB, LANES), F32),
                            pltpu.VMEM((2 * stage_rows * SUB, LANES), F32), pltpu.VMEM((SUB, LANES), F32),
                            pltpu.SemaphoreType.DMA((2,)), pltpu.SemaphoreType.DMA((2,)),
                            pltpu.SemaphoreType.DMA((1,))]),
        out_shape=jax.ShapeDtypeStruct((n_rows * SUB, LANES), F32),
        compiler_params=pltpu.CompilerParams(dimension_semantics=("arbitrary",), vmem_limit_bytes=VMEM_LIMIT),
        name="moe_dispatch",
    )(pad_start + counts, pad_end, pos3, meta, meta, rest, rest, h1)

    last_act = lambda i, na: jnp.minimum(i, na[0] - 1)
    e_mat = lambda shape: pl.BlockSpec((None,) + shape, lambda i, be, na: (be[i], 0, 0))
    y_sorted = pl.pallas_call(
        _experts_kernel,
        grid_spec=pltpu.PrefetchScalarGridSpec(
            num_scalar_prefetch=2,
            grid=(n_blocks,),
            in_specs=[pl.BlockSpec((MOE_BLOCK * SUB, LANES), lambda i, be, na: (last_act(i, na), 0)),
                      e_mat((d, d_ff)), e_mat((d, d_ff)), e_mat((1, d_ff)), e_mat((1, d_ff)),
                      e_mat((d_ff, d)), e_mat((1, d))],
            out_specs=pl.BlockSpec((MOE_BLOCK * SUB, LANES), lambda i, be, na: (i, 0)),
            scratch_shapes=[pltpu.VMEM((d, d_ff), BF16), pltpu.VMEM((d, d_ff), BF16),
                            pltpu.VMEM((d_ff, d), BF16)]),
        out_shape=jax.ShapeDtypeStruct((n_rows * SUB, LANES), F32),
        compiler_params=pltpu.CompilerParams(dimension_semantics=("arbitrary",), vmem_limit_bytes=VMEM_LIMIT),
        name="moe_experts",
    )(block_e, n_act, x_sorted,
      moe_w_gate[0].astype(F32), moe_w_up[0].astype(F32),
      moe_b_gate[0].reshape(n_exp, 1, d_ff).astype(F32), moe_b_up[0].reshape(n_exp, 1, d_ff).astype(F32),
      moe_w_down[0].astype(F32), moe_b_down[0].reshape(n_exp, 1, d).astype(F32))

    out = pl.pallas_call(
        functools.partial(_combine_kernel, alpha, n_exp),
        grid=(n_ct,),
        in_specs=[smem_blk(TOP_K * c_tile, lambda i: (i, 0, 0)),
                  smem_blk(TOP_K * c_tile, lambda i: (i, 0, 0)),
                  smem_blk(meta_w, lambda i: (i, 0, 0)),
                  smem_blk(meta_w, lambda i: (jnp.minimum(i + 1, n_ct - 1), 0, 0)),
                  smem_blk(LANES, lambda i: (i, 0, 0)),
                  smem_blk(LANES, lambda i: (jnp.minimum(i + 1, n_ct - 1), 0, 0)),
                  pl.BlockSpec((c_tile * SUB, LANES), lambda i: (i, 0)),
                  _full((1, d)), _full((1, d)),
                  pl.BlockSpec(memory_space=pl.ANY)],
        out_specs=pl.BlockSpec((c_tile, d), lambda i: (i, 0)),
        out_shape=jax.ShapeDtypeStruct((n_tok, d), F32),
        scratch_shapes=[pltpu.VMEM((2 * stage_rows * SUB, LANES), F32), pltpu.VMEM((c_tile * SUB, LANES), F32),
                        pltpu.SemaphoreType.DMA((2,))],
        compiler_params=pltpu.CompilerParams(dimension_semantics=("arbitrary",), vmem_limit_bytes=VMEM_LIMIT),
        name="moe_combine",
    )(pos3, gate3, meta, meta, tails, tails, h1, row(ln2_g[0]), row(ln2_b[0]), y_sorted)
    return out.reshape(batch, seq, d).astype(x.dtype)
```

```python
import functools

import numpy as np
import jax
import jax.numpy as jnp
from jax import lax
from jax.experimental import pallas as pl
from jax.experimental.pallas import tpu as pltpu

F32 = jnp.float32
BF16 = jnp.bfloat16

CHUNK = 64
N_META = 16
PAIR = 2 * CHUNK
GLA_HEADS = 4
GLA_DK = 64
GLA_DV = 128
GLA_KEY = GLA_HEADS * GLA_DK
GLA_VAL = GLA_HEADS * GLA_DV
GLA_RANK = 16
GLA_TAU = 16.0
SSD_INNER = 512
SSD_HEADS = 8
SSD_HEADDIM = 64
SSD_GROUPS = 2
SSD_STATE = 128
SSD_GROUP_W = SSD_INNER // SSD_GROUPS
SSD_CONV = 4
SSD_CONV_CH = SSD_INNER + 2 * SSD_GROUPS * SSD_STATE
TOP_K = 4
SWIGLU_LIMIT = 7.0
SWIGLU_ALPHA = 1.702
MOE_BLOCK = 512
LN_EPS = 1e-5
RMS_EPS = 1e-6
LANES = 128

C_Q, C_K, C_V, C_OG, C_Z, C_XBC, C_MISC, C_END = 0, 256, 512, 1024, 1536, 2048, 3072, 3200
MISC_A1 = 0
MISC_DT = 16
CONV_PAD = 8

MIXER_TILE = 256
MIXER_SEQS = 2
SEQ_SKEW = 5
VMEM_LIMIT = 56 * 1024 * 1024


def _dot(a, b):
    return jnp.dot(a, b, preferred_element_type=F32)


def _split_hi_lo(x):
    hi = x.astype(BF16)
    lo = (x - hi.astype(F32)).astype(BF16)
    return hi, lo


def _log_sigmoid(x):
    return jnp.minimum(x, 0.0) - jnp.log1p(jnp.exp(-jnp.abs(x)))


def _softplus(x):
    return jnp.maximum(x, 0.0) + jnp.log1p(jnp.exp(-jnp.abs(x)))


def _silu(x):
    return x * jax.nn.sigmoid(x)


def _layer_norm(t, g, b):
    mu = jnp.mean(t, axis=-1, keepdims=True)
    tc = t - mu
    var = jnp.mean(tc * tc, axis=-1, keepdims=True)
    return tc * lax.rsqrt(var + LN_EPS) * g + b


def _group_rms(t, g, width):
    outs = []
    for s in range(0, t.shape[-1], width):
        seg = t[:, s:s + width]
        ms = jnp.mean(seg * seg, axis=-1, keepdims=True)
        outs.append(seg * lax.rsqrt(ms + RMS_EPS) * g[:, s:s + width])
    return jnp.concatenate(outs, axis=-1)


def _run(steps):
    try:
        while True:
            next(steps)
    except StopIteration as done:
        return done.value


def _mixer_tile(h, tile, valid_col, valid_row, p, s_gla, s_ssd, xbc_buf, mix_buf, need_out):
    hb = h.astype(BF16)
    w_in = p["w_in"]

    def proj(lo, hi):
        r = _dot(hb, w_in[:, lo:hi])
        if valid_col is not None:
            r = jnp.where(valid_col, r, 0.0)
        return r

    misc = proj(C_MISC, C_END)

    la = _log_sigmoid(_dot(misc.astype(BF16), p["w_a2"][...]) + p["b_a"][...]) * (1.0 / GLA_TAU)
    if valid_col is not None:
        la = jnp.where(valid_col, la, 0.0)
    la_hi, la_lo = _split_hi_lo(la)
    r = _dot(p["uo"][...], jnp.concatenate([la_hi, la_lo], axis=1))
    dec_exp = r[0:tile, 0:GLA_KEY] + r[0:tile, GLA_KEY:]
    tot_rows = r[tile:, 0:GLA_KEY] + r[tile:, GLA_KEY:]
    gla_dec_t = jnp.exp(tot_rows.T)
    kd = proj(C_K, C_V) * jnp.exp(dec_exp)
    v_bf = proj(C_V, C_OG).astype(BF16)

    yield
    misc_t = misc.T
    dt = _softplus(misc_t[MISC_DT:MISC_DT + SSD_HEADS, :] + p["dt_bias"][...])
    if valid_row is not None:
        dt = jnp.where(valid_row, dt, 0.0)
    dta = dt * (-jnp.exp(p["a_log"][...]))
    d_hi, d_lo = _split_hi_lo(dta)
    r2 = _dot(jnp.concatenate([d_hi, d_lo], axis=0), p["uto"][...])
    w = jnp.exp(r2[0:8, 0:tile] + r2[8:16, 0:tile]) * dt
    ssd_dec = jnp.exp(r2[0:8, tile:] + r2[8:16, tile:])
    w_hi, w_lo = _split_hi_lo(w)
    c_hi, c_lo = _split_hi_lo(ssd_dec)
    packed = jnp.concatenate(
        [w_hi.astype(F32), w_lo.astype(F32), c_hi.astype(F32), c_lo.astype(F32),
         jnp.zeros((LANES - 4 * SSD_HEADS, tile), F32)], axis=0)
    expanded = _dot(packed.T.astype(BF16), p["eexp"][...])
    w_exp = expanded[:, 0:SSD_INNER]
    ssd_dec_exp = expanded[:, SSD_INNER:]

    yield
    xbc_buf[CONV_PAD:CONV_PAD + tile, :] = proj(C_XBC, C_MISC)
    acc = p["conv_b"][...] + p["conv_w"][0:1, :] * xbc_buf[pl.ds(CONV_PAD - 3, tile), :]
    for j in range(1, SSD_CONV):
        acc = acc + p["conv_w"][j:j + 1, :] * xbc_buf[pl.ds(CONV_PAD - 3 + j, tile), :]
    xbc_buf[0:CONV_PAD, :] = xbc_buf[tile:tile + CONV_PAD, :]
    xa = _silu(acc)
    xs = xa[:, 0:SSD_INNER]
    bm = xa[:, SSD_INNER:SSD_INNER + SSD_GROUPS * SSD_STATE]
    cm_bf = xa[:, SSD_INNER + SSD_GROUPS * SSD_STATE:].astype(BF16)
    xw_bf = (xs * w_exp).astype(BF16)
    yield

    if need_out:
        q = proj(C_Q, C_K) * (GLA_DK ** -0.5)

    lane = lax.broadcasted_iota(jnp.int32, (1, PAIR), 1)
    for pr in range(tile // PAIR):
        rows = slice(pr * PAIR, (pr + 1) * PAIR)
        kd_t = kd[rows].T
        bm_t = bm[rows].T
        for half in range(2):
            c = 2 * pr + half
            r0 = c * CHUNK
            sel = (lane >= half * CHUNK) & (lane < (half + 1) * CHUNK)
            kd_m = jnp.where(sel, kd_t, 0.0).astype(BF16)
            upd = jnp.concatenate(
                [_dot(kd_m[hd * GLA_DK:(hd + 1) * GLA_DK], v_bf[rows, hd * GLA_DV:(hd + 1) * GLA_DV])
                 for hd in range(GLA_HEADS)], axis=0)
            s_new = gla_dec_t[:, c:c + 1] * s_gla[...] + upd
            s_gla[...] = s_new
            bm_m = jnp.where(sel, bm_t, 0.0).astype(BF16)
            ssd_new = []
            for g in range(SSD_GROUPS):
                gl = slice(g * SSD_GROUP_W, (g + 1) * SSD_GROUP_W)
                gr = slice(g * SSD_STATE, (g + 1) * SSD_STATE)
                u = _dot(bm_m[gr], xw_bf[rows, gl])
                sg = ssd_dec_exp[r0:r0 + 1, gl] * s_ssd[gr, :] + u
                s_ssd[gr, :] = sg
                ssd_new.append(sg)
            if need_out:
                s_bf = s_new.astype(BF16)
                zero = jnp.zeros((GLA_DK, GLA_DV), BF16)
                s_diag = jnp.concatenate(
                    [jnp.concatenate([s_bf[hd * GLA_DK:(hd + 1) * GLA_DK] if col == hd else zero
                                      for col in range(GLA_HEADS)], axis=1) for hd in range(GLA_HEADS)],
                    axis=0)
                mix_buf[r0:r0 + CHUNK, 0:GLA_VAL] = _dot(q[r0:r0 + CHUNK].astype(BF16), s_diag)
                for g in range(SSD_GROUPS):
                    y = _dot(cm_bf[r0:r0 + CHUNK, g * SSD_STATE:(g + 1) * SSD_STATE],
                             ssd_new[g].astype(BF16))
                    mix_buf[r0:r0 + CHUNK, GLA_VAL + g * SSD_GROUP_W:GLA_VAL + (g + 1) * SSD_GROUP_W] = y
        yield

    if not need_out:
        return None
    o_gla = _group_rms(mix_buf[:, 0:GLA_VAL], p["gla_norm_g"][...], GLA_DV) * _silu(proj(C_OG, C_Z))
    yield
    y = (mix_buf[:, GLA_VAL:] + xs * p["ssd_d"][...]) * _silu(proj(C_Z, C_XBC))
    y = _group_rms(y, p["ssd_norm_g"][...], SSD_GROUP_W)
    return o_gla, y


_MIX_PARAMS = ("w_in", "w_a2", "b_a", "conv_w", "conv_b", "dt_bias", "a_log", "uo", "uto", "eexp")
_OUT_PARAMS = ("gla_norm_g", "ssd_d", "ssd_norm_g")


def _meta_kernel(x_ref, ln_g, ln_b, *rest):
    np_ = len(_MIX_PARAMS)
    p = dict(zip(_MIX_PARAMS, rest[:np_]))
    s_gla_out, s_ssd_out, tail_out, xbc_buf = rest[np_:]
    tile = x_ref.shape[0]
    s_gla_out[...] = jnp.zeros_like(s_gla_out)
    s_ssd_out[...] = jnp.zeros_like(s_ssd_out)
    xbc_buf[0:CONV_PAD, :] = jnp.zeros((CONV_PAD, xbc_buf.shape[1]), F32)
    first_valid = tile - N_META
    valid_col = lax.broadcasted_iota(jnp.int32, (tile, 1), 0) >= first_valid
    valid_row = lax.broadcasted_iota(jnp.int32, (1, tile), 1) >= first_valid
    h = _layer_norm(x_ref[...], ln_g[...], ln_b[...])
    _run(_mixer_tile(h, tile, valid_col, valid_row, p, s_gla_out, s_ssd_out, xbc_buf, None, False))
    tail_out[...] = xbc_buf[0:CONV_PAD, :]


def _mixer_kernel(alpha, n_exp, x_ref, ln_g, ln_b, *rest):
    np_ = len(_MIX_PARAMS)
    p = dict(zip(_MIX_PARAMS, rest[:np_]))
    rest = rest[np_:]
    p.update(zip(_OUT_PARAMS, rest[:3]))
    (w_out, ln1_g, ln1_b, rw, rb, su, s_gla0, s_ssd0, tail0,
     h1_out, topi_out, gate_out, rank_out, cnt_out, base_out, tcnt_out) = rest[3:19]
    n_sub, tile = x_ref.shape[0], x_ref.shape[1]
    scratch = rest[19:]
    s_gla, s_ssd, xbc_buf, mix_buf = (scratch[k * n_sub:(k + 1) * n_sub] for k in range(4))
    run_cnt = scratch[4 * n_sub]
    b, j = pl.program_id(0), pl.program_id(1)

    @pl.when(j == 0)
    def _():
        for s in range(n_sub):
            s_gla[s][...] = s_gla0[...]
            s_ssd[s][...] = s_ssd0[...]
            xbc_buf[s][0:CONV_PAD, :] = tail0[...]

    @pl.when((b == 0) & (j == 0))
    def _():
        run_cnt[...] = jnp.zeros_like(run_cnt)

    count = [run_cnt[...]]
    pad_rows = topi_out.shape[1] - TOP_K

    def sequence(s):
        h = _layer_norm(x_ref[s], ln_g[...], ln_b[...])
        o_gla, y = yield from _mixer_tile(h, tile, None, None, p, s_gla[s], s_ssd[s], xbc_buf[s], mix_buf[s], True)
        yield
        mix = _dot(o_gla.astype(BF16), w_out[0:GLA_VAL, :]) + _dot(y.astype(BF16), w_out[GLA_VAL:, :])
        h1 = _layer_norm(alpha * h + mix, ln1_g[...], ln1_b[...])
        _store_rows(h1_out.at[s], h1)
        yield

        h_hi, h_lo = _split_hi_lo(h1)
        p_hi = _dot(h_hi, rw[...])
        logits = p_hi + (pltpu.roll(p_hi, LANES - n_exp, 1) + _dot(h_lo, rw[...]))
        lt = logits.T[0:n_exp, :] + rb[...]
        e_iota = lax.broadcasted_iota(jnp.int32, (n_exp, tile), 0).astype(F32)
        work = lt
        vals, hots = [], []
        for k in range(TOP_K):
            m = jnp.max(work, axis=0, keepdims=True)
            idx = jnp.min(jnp.where(work == m, e_iota, float(n_exp)), axis=0, keepdims=True)
            hot = e_iota == idx
            work = jnp.where(hot, -jnp.inf, work)
            vals.append(m)
            hots.append(hot)
            topi_out[s, k:k + 1, :] = idx.astype(jnp.int32)
        exps = [jnp.exp(vk - vals[0]) for vk in vals]
        denom = exps[0] + exps[1] + exps[2] + exps[3]
        for k in range(TOP_K):
            gate_out[s, k:k + 1, :] = exps[k] / denom
        member = (hots[0] | hots[1] | hots[2] | hots[3]).astype(F32).astype(BF16)
        yield
        before = _dot(member, su[...])
        for k in range(TOP_K):
            rk = jnp.sum(jnp.where(hots[k], before, 0.0), axis=0, keepdims=True)
            rank_out[s, k:k + 1, :] = rk.astype(jnp.int32)
        topi_out[s, TOP_K:, :] = jnp.zeros((pad_rows, tile), jnp.int32)
        gate_out[s, TOP_K:, :] = jnp.zeros((pad_rows, tile), F32)
        rank_out[s, TOP_K:, :] = jnp.zeros((pad_rows, tile), jnp.int32)
        tile_count = _dot(member, jnp.ones((tile, LANES), BF16))
        base_out[s, 0] = count[0]
        tcnt_out[s, 0] = tile_count
        count[0] = count[0] + tile_count

    pending = [sequence(s) for s in range(n_sub)]
    for lead in range(1, n_sub):
        for _ in range(SEQ_SKEW):
            for steps in pending[:lead]:
                next(steps)
    while pending:
        for steps in list(pending):
            try:
                next(steps)
            except StopIteration:
                pending.remove(steps)
    run_cnt[...] = count[0]
    cnt_out[...] = count[0]


def _clamped_swiglu(a, u):
    a = jnp.minimum(a, SWIGLU_LIMIT)
    u = jnp.clip(u, -SWIGLU_LIMIT, SWIGLU_LIMIT)
    return a * jax.nn.sigmoid(SWIGLU_ALPHA * a) * (u + 1.0)


SUB = 8


def _load_rows(ref, n_rows):
    return jnp.concatenate([ref[pl.ds(j, n_rows, stride=SUB), :] for j in range(SUB)], axis=1)


def _store_rows(ref, val):
    for j in range(SUB):
        ref[pl.ds(j, val.shape[0], stride=SUB), :] = val[:, j * LANES:(j + 1) * LANES]


def _tile_rows(row):
    return pl.ds(pl.multiple_of(row * SUB, SUB), SUB)


RUN_ROWS = 8
PIECE_GROUP = 4
REST_SIZES = (4, 2, 1)


def _run_rows(row):
    return pl.ds(pl.multiple_of(row * SUB, SUB), RUN_ROWS * SUB)


def _stage_tile(addr):
    return pl.ds(pl.multiple_of(addr, SUB), SUB)


def _for_pieces(pieces, fn):
    def body(g, _):
        for u in range(PIECE_GROUP):
            p = g * PIECE_GROUP + u
            fn(p * RUN_ROWS, pieces[0, 0, 1 + p])
        return 0
    lax.fori_loop(0, pieces[0, 0, 0], body, 0)


def _dispatch_kernel(fill_lo, fill_hi, pos, meta_cur, meta_prv, tail_cur, tail_prv, h_hbm, xs_hbm,
                     tbuf, stage, zrow, lsem, rsem, zsem):
    i = pl.program_id(0)
    n = pl.num_programs(0)
    tile = pos.shape[-1] // TOP_K
    slot = i % 2
    n_exp = fill_lo.shape[0]
    stage_rows = stage.shape[0] // (2 * SUB)
    rest_base = stage_rows - n_exp * RUN_ROWS

    def tile_load(j, s):
        return pltpu.make_async_copy(h_hbm.at[pl.ds(j * tile * SUB, tile * SUB), :], tbuf.at[s], lsem.at[s])

    def rows_copy(s, stage_row, sorted_row, n_rows):
        src = stage.at[pl.ds(pl.multiple_of((s * stage_rows + stage_row) * SUB, SUB), n_rows * SUB), :]
        dst = xs_hbm.at[pl.ds(pl.multiple_of(sorted_row * SUB, SUB), n_rows * SUB), :]
        return pltpu.make_async_copy(src, dst, rsem.at[s])

    def for_rest(rest, s, wait):
        for c, size in enumerate(REST_SIZES):
            first = 8 + 2 * n_exp * c

            def body(q, _, first=first, size=size):
                if wait:
                    rows_copy(s, 0, 0, size).wait()
                else:
                    rows_copy(s, rest[0, 0, first + 2 * q + 1], rest[0, 0, first + 2 * q], size).start()
                return 0
            lax.fori_loop(0, rest[0, 0, c], body, 0)

    def zero_copy(dst_row):
        return pltpu.make_async_copy(zrow, xs_hbm.at[_tile_rows(dst_row), :], zsem.at[0])

    @pl.when(i == 0)
    def _():
        tile_load(0, 0).start()
        zrow[...] = jnp.zeros_like(zrow)
        stage[...] = jnp.zeros_like(stage)
        for wait in (False, True):
            def per_expert(e, _, wait=wait):
                def body(row, _):
                    if wait:
                        zero_copy(0).wait()
                    else:
                        zero_copy(row).start()
                    return 0
                lax.fori_loop(fill_lo[e], fill_hi[e], body, 0)
                return 0
            lax.fori_loop(0, n_exp, per_expert, 0)
        tail_lo = fill_hi[n_exp - 1]
        tail_pieces = (xs_hbm.shape[0] // SUB - tail_lo) // RUN_ROWS
        for wait in (False, True):
            def tail(c, _, wait=wait):
                cp = pltpu.make_async_copy(stage.at[_run_rows(0), :], xs_hbm.at[_run_rows(tail_lo + c * RUN_ROWS), :],
                                           zsem.at[0])
                if wait:
                    cp.wait()
                else:
                    cp.start()
                return 0
            lax.fori_loop(0, tail_pieces, tail, 0)

    @pl.when(i + 1 < n)
    def _():
        tile_load(i + 1, 1 - slot).start()

    tile_load(i, slot).wait()

    def pack(t, _):
        row = tbuf[slot, _tile_rows(t), :]
        for k in range(TOP_K):
            stage[_stage_tile(pos[0, 0, TOP_K * t + k]), :] = row
        return 0
    lax.fori_loop(0, tile, pack, 0, unroll=8)

    _for_pieces(meta_cur, lambda off, row: rows_copy(slot, off, row, RUN_ROWS).start())
    for_rest(tail_cur, slot, False)

    def wait_tile(rest, s):
        rows_copy(s, 0, 0, TOP_K * tile).wait()

        def body(q, _):
            rows_copy(s, 0, 0, RUN_ROWS).wait()
            return 0
        lax.fori_loop(0, rest[0, 0, len(REST_SIZES)], body, 0)

    @pl.when(i > 0)
    def _():
        wait_tile(tail_prv, 1 - slot)

    @pl.when(i == n - 1)
    def _():
        wait_tile(tail_cur, slot)


def _experts_kernel(blk_e, n_act, x_ref, wg, wu, bg, bu, wd, bd, y_out, wg_bf, wu_bf, wd_bf):
    i = pl.program_id(0)
    active = i < n_act[0]
    new_expert = (i == 0) | (blk_e[i] != blk_e[jnp.maximum(i - 1, 0)])

    @pl.when(active & new_expert)
    def _():
        wg_bf[...] = wg[...].astype(BF16)
        wu_bf[...] = wu[...].astype(BF16)
        wd_bf[...] = wd[...].astype(BF16)

    @pl.when(active)
    def _():
        xb = _load_rows(x_ref, x_ref.shape[0] // SUB).astype(BF16)
        hid = _clamped_swiglu(_dot(xb, wg_bf[...]) + bg[...], _dot(xb, wu_bf[...]) + bu[...])
        _store_rows(y_out, _dot(hid.astype(BF16), wd_bf[...]) + bd[...])

    @pl.when(i >= n_act[0])
    def _():
        y_out[...] = jnp.zeros_like(y_out)


def _combine_kernel(alpha, n_exp, pos, gate, meta_cur, meta_nxt, tail_cur, tail_nxt, h1, ln_g, ln_b, y_hbm, out,
                    stage, ff_rows, sem):
    i = pl.program_id(0)
    n = pl.num_programs(0)
    tile = out.shape[0]
    slot = i % 2
    stage_rows = stage.shape[0] // (2 * SUB)
    rest_base = stage_rows - n_exp * RUN_ROWS

    def run_copy(s, stage_row, sorted_row):
        return pltpu.make_async_copy(y_hbm.at[_run_rows(sorted_row), :],
                                     stage.at[_run_rows(s * stage_rows + stage_row), :], sem.at[s])

    def fetch(meta, tails, s, wait):
        act = (lambda cp: cp.wait()) if wait else (lambda cp: cp.start())
        _for_pieces(meta, lambda off, row: act(run_copy(s, 0 if wait else off, 0 if wait else row)))

        if wait:
            pltpu.make_async_copy(y_hbm.at[pl.ds(0, n_exp * RUN_ROWS * SUB), :],
                                  stage.at[pl.ds(0, n_exp * RUN_ROWS * SUB), :], sem.at[s]).wait()
            return

        def rest(g, _):
            for u in range(PIECE_GROUP):
                e = g * PIECE_GROUP + u
                run_copy(s, rest_base + e * RUN_ROWS, tails[0, 0, e]).start()
            return 0
        lax.fori_loop(0, n_exp // PIECE_GROUP, rest, 0)

    @pl.when(i == 0)
    def _():
        fetch(meta_cur, tail_cur, 0, False)

    @pl.when(i + 1 < n)
    def _():
        fetch(meta_nxt, tail_nxt, 1 - slot, False)

    fetch(meta_cur, tail_cur, slot, True)

    def gather(t, _):
        acc = gate[0, 0, TOP_K * t] * stage[_stage_tile(pos[0, 0, TOP_K * t]), :]
        for k in range(1, TOP_K):
            acc = acc + gate[0, 0, TOP_K * t + k] * stage[_stage_tile(pos[0, 0, TOP_K * t + k]), :]
        ff_rows[_tile_rows(t), :] = acc
        return 0
    lax.fori_loop(0, tile, gather, 0, unroll=8)

    out[...] = _layer_norm(alpha * _load_rows(h1, tile) + _load_rows(ff_rows, tile), ln_g[...], ln_b[...])


def _block_constants(tile):
    t = np.arange(tile)
    same = (t[:, None] // CHUNK) == (t[None, :] // CHUNK)
    later = same & (t[None, :] > t[:, None])
    chunk_rows = np.zeros((LANES, tile), np.float32)
    chunk_rows[t // CHUNK, t] = 1.0
    uo = np.concatenate([later.astype(np.float32), chunk_rows], axis=0)
    uto = np.concatenate([later.T.astype(np.float32), same.astype(np.float32)], axis=1)
    return jnp.asarray(uo, BF16), jnp.asarray(uto, BF16)


def _expand_constant():
    e = np.zeros((LANES, 2 * SSD_INNER), np.float32)
    for part in range(4):
        for hd in range(SSD_HEADS):
            base = (part // 2) * SSD_INNER + hd * SSD_HEADDIM
            e[part * SSD_HEADS + hd, base:base + SSD_HEADDIM] = 1.0
    return jnp.asarray(e, BF16)


def _full(shape):
    return pl.BlockSpec(shape, lambda *_: (0,) * len(shape))


def kernel(x, meta_tokens, ln_in_g, ln_in_b, w_in, gla_w_a2, gla_b_a, gla_norm_g, ssd_conv_w, ssd_conv_b,
           ssd_dt_bias, ssd_a_log, ssd_d, ssd_norm_g, w_out, ln1_g, ln1_b, router_w, router_b, moe_w_gate,
           moe_w_up, moe_b_gate, moe_b_up, moe_w_down, moe_b_down, ln2_g, ln2_b):
    batch, seq, d = x.shape
    depth = w_in.shape[0]
    assert depth == 1, "single-layer stack"
    n_exp = router_w.shape[-1]
    d_ff = moe_w_gate.shape[-1]
    alpha = (2.0 * depth) ** 0.25
    tile = MIXER_TILE
    assert seq % tile == 0 and d == 1024
    n_tok = batch * seq
    row = lambda a: a.reshape(1, -1).astype(F32)

    wi = w_in[0]
    o_a1 = 1536
    o_z = o_a1 + GLA_RANK
    o_xbc = o_z + SSD_INNER
    o_dt = o_xbc + SSD_CONV_CH
    misc_w = jnp.zeros((d, LANES), F32)
    misc_w = misc_w.at[:, MISC_A1:MISC_A1 + GLA_RANK].set(wi[:, o_a1:o_z])
    misc_w = misc_w.at[:, MISC_DT:MISC_DT + SSD_HEADS].set(wi[:, o_dt:o_dt + SSD_HEADS])
    w_in_r = jnp.concatenate([wi[:, 0:o_a1], wi[:, o_z:o_xbc], wi[:, o_xbc:o_dt], misc_w], axis=1).astype(BF16)
    w_a2 = jnp.zeros((LANES, GLA_KEY), F32).at[MISC_A1:MISC_A1 + GLA_RANK].set(gla_w_a2[0]).astype(BF16)
    mix_params = dict(
        w_in=w_in_r, w_a2=w_a2, b_a=row(gla_b_a[0]), conv_w=ssd_conv_w[0].astype(F32),
        conv_b=row(ssd_conv_b[0]), dt_bias=ssd_dt_bias[0].reshape(-1, 1).astype(F32),
        a_log=ssd_a_log[0].reshape(-1, 1).astype(F32), eexp=_expand_constant())
    out_params = dict(gla_norm_g=row(gla_norm_g[0]), ssd_d=row(jnp.repeat(ssd_d[0], SSD_HEADDIM)),
                      ssd_norm_g=row(ssd_norm_g[0]))

    def mix_args(t):
        uo, uto = _block_constants(t)
        vals = dict(mix_params, uo=uo, uto=uto)
        return [vals[k] for k in _MIX_PARAMS]

    m_tile = PAIR
    x_meta = jnp.concatenate([jnp.zeros((m_tile - N_META, d), F32), meta_tokens.astype(F32)], axis=0)
    meta_in = [x_meta, row(ln_in_g), row(ln_in_b)] + mix_args(m_tile)
    s_gla0, s_ssd0, tail0 = pl.pallas_call(
        _meta_kernel,
        out_shape=(jax.ShapeDtypeStruct((GLA_KEY, GLA_DV), F32),
                   jax.ShapeDtypeStruct((SSD_GROUPS * SSD_STATE, SSD_GROUP_W), F32),
                   jax.ShapeDtypeStruct((CONV_PAD, SSD_CONV_CH), F32)),
        scratch_shapes=[pltpu.VMEM((CONV_PAD + m_tile, SSD_CONV_CH), F32)],
        compiler_params=pltpu.CompilerParams(vmem_limit_bytes=VMEM_LIMIT),
        name="meta_state",
    )(*meta_in)

    n_j = seq // tile
    rw_hi, rw_lo = _split_hi_lo(router_w[0].astype(F32))
    rw = jnp.zeros((d, LANES), BF16).at[:, 0:n_exp].set(rw_hi).at[:, n_exp:2 * n_exp].set(rw_lo)
    rb = router_b[0].reshape(n_exp, 1).astype(F32)
    su = jnp.asarray(np.triu(np.ones((tile, tile), np.float32), 1), BF16)
    args = ([x, row(ln_in_g), row(ln_in_b)] + mix_args(tile) + [out_params[k] for k in _OUT_PARAMS]
            + [w_out[0].astype(BF16), row(ln1_g[0]), row(ln1_b[0]), rw, rb, su, s_gla0, s_ssd0, tail0])
    n_sub = MIXER_SEQS
    assert batch % n_sub == 0
    in_specs = [pl.BlockSpec((n_sub, tile, d), lambda b, j: (b, j, 0))] + [_full(a.shape) for a in args[1:]]
    tok_blk = pl.BlockSpec((n_sub, 8, tile), lambda b, j: (b, 0, j))
    tile_tab = pl.BlockSpec((n_sub, 1, n_exp, LANES), lambda b, j: (b, j, 0, 0))
    h1, topi, gates, rank, cnt, base, tile_cnt = pl.pallas_call(
        functools.partial(_mixer_kernel, alpha, n_exp),
        grid=(batch // n_sub, n_j),
        in_specs=in_specs,
        out_specs=(pl.BlockSpec((n_sub, tile * SUB, LANES), lambda b, j: (b, j, 0)),
                   tok_blk, tok_blk, tok_blk, _full((n_exp, LANES)), tile_tab, tile_tab),
        out_shape=(jax.ShapeDtypeStruct((batch, seq * SUB, LANES), F32),
                   jax.ShapeDtypeStruct((batch, 8, seq), jnp.int32),
                   jax.ShapeDtypeStruct((batch, 8, seq), F32),
                   jax.ShapeDtypeStruct((batch, 8, seq), jnp.int32),
                   jax.ShapeDtypeStruct((n_exp, LANES), F32),
                   jax.ShapeDtypeStruct((batch, n_j, n_exp, LANES), F32),
                   jax.ShapeDtypeStruct((batch, n_j, n_exp, LANES), F32)),
        scratch_shapes=([pltpu.VMEM((GLA_KEY, GLA_DV), F32)] * n_sub
                        + [pltpu.VMEM((SSD_GROUPS * SSD_STATE, SSD_GROUP_W), F32)] * n_sub
                        + [pltpu.VMEM((CONV_PAD + tile, SSD_CONV_CH), F32)] * n_sub
                        + [pltpu.VMEM((tile, d), F32)] * n_sub
                        + [pltpu.VMEM((n_exp, LANES), F32)]),
        compiler_params=pltpu.CompilerParams(dimension_semantics=("arbitrary", "arbitrary"),
                                             vmem_limit_bytes=VMEM_LIMIT),
        name="mixer",
    )(*args)
    h1 = h1.reshape(n_tok * SUB, LANES)
    per_tok = lambda a: a[:, :TOP_K].transpose(1, 0, 2).reshape(TOP_K, n_tok)
    top_e, gates, rank = per_tok(topi), per_tok(gates), per_tok(rank)

    counts = cnt[:, 0].astype(jnp.int32)
    padded = (counts + RUN_ROWS + MOE_BLOCK - 1) // MOE_BLOCK * MOE_BLOCK
    pad_end = jnp.cumsum(padded)
    pad_start = pad_end - padded
    n_blocks = -(-(n_tok * TOP_K) // MOE_BLOCK) + n_exp + 1
    n_rows = n_blocks * MOE_BLOCK
    blk_lo = jnp.arange(n_blocks, dtype=jnp.int32) * MOE_BLOCK
    block_e = jnp.minimum(jnp.sum((blk_lo[:, None] >= pad_end[None, :]).astype(jnp.int32), axis=1), n_exp - 1)
    n_act = (pad_end[-1] // MOE_BLOCK).astype(jnp.int32).reshape(1)
    c_tile = tile
    n_ct = n_tok // c_tile
    run_base = base[..., 0].astype(jnp.int32).reshape(n_ct, n_exp)
    run_len = tile_cnt[..., 0].astype(jnp.int32).reshape(n_ct, n_exp)
    src_row = pad_start[None, :] + run_base
    run_pieces = run_len // RUN_ROWS
    full_rows = run_pieces * RUN_ROWS
    piece_end = jnp.cumsum(run_pieces, axis=1)
    first_piece = piece_end - run_pieces
    stage_row = first_piece * RUN_ROWS
    max_pieces = -(-(TOP_K * c_tile // RUN_ROWS) // PIECE_GROUP) * PIECE_GROUP
    p_ids = jnp.arange(max_pieces, dtype=jnp.int32)
    in_run = (p_ids[None, :, None] >= first_piece[:, None, :]) & (p_ids[None, :, None] < piece_end[:, None, :])
    piece_row = jnp.sum(jnp.where(in_run, (src_row - stage_row)[:, None, :], 0), axis=2)
    piece_row = piece_row + p_ids[None, :] * RUN_ROWS
    n_pieces = piece_end[:, -1:]
    tile_parity = (jnp.arange(n_ct, dtype=jnp.int32) % 2)[:, None]
    spare_row = (n_blocks - 1) * MOE_BLOCK + (tile_parity * PIECE_GROUP + p_ids[None, :] - n_pieces) * RUN_ROWS
    piece_row = jnp.where(p_ids[None, :] < n_pieces, piece_row, jnp.clip(spare_row, 0, n_rows - RUN_ROWS))
    n_groups = (n_pieces + PIECE_GROUP - 1) // PIECE_GROUP
    meta_w = -(-(max_pieces + 1) // LANES) * LANES
    meta = jnp.concatenate([n_groups, piece_row,
                            jnp.zeros((n_ct, meta_w - 1 - max_pieces), jnp.int32)], axis=1).reshape(n_ct, 1, meta_w)
    rest_base = max_pieces * RUN_ROWS
    stage_rows = rest_base + n_exp * RUN_ROWS
    tails = jnp.concatenate([src_row + full_rows, run_len - full_rows,
                             jnp.zeros((n_ct, LANES - 2 * n_exp), jnp.int32)], axis=1).reshape(n_ct, 1, LANES)
    rest_len = run_len - full_rows
    e_row = jnp.arange(n_exp, dtype=jnp.int32)
    rest_cnt, rest_lists = [], []
    for size in REST_SIZES:
        has = (rest_len & size) != 0
        slot_in_list = jnp.cumsum(has, axis=1) - has
        before = rest_len & (7 & ~(2 * size - 1))
        sel = has[:, None, :] & (slot_in_list[:, None, :] == e_row[None, :, None])
        pick_run = lambda v: jnp.sum(jnp.where(sel, v[:, None, :], 0), axis=2)
        rest_lists.append(jnp.stack([pick_run(src_row + full_rows + before),
                                     pick_run(rest_base + e_row[None, :] * RUN_ROWS + before)], axis=2)
                          .reshape(n_ct, 2 * n_exp))
        rest_cnt.append(jnp.sum(has, axis=1, keepdims=True).astype(jnp.int32))
    rest_w = -(-(8 + 2 * n_exp * len(REST_SIZES)) // LANES) * LANES
    rest_cnt.append(n_groups * PIECE_GROUP - n_pieces)
    rest = jnp.concatenate(rest_cnt + [jnp.zeros((n_ct, 8 - len(rest_cnt)), jnp.int32)] + rest_lists
                           + [jnp.zeros((n_ct, rest_w - 8 - 2 * n_exp * len(REST_SIZES)), jnp.int32)],
                           axis=1).reshape(n_ct, 1, rest_w)
    e_ids = e_row[:, None, None]
    hot = top_e[None] == e_ids
    pack = 2 * TOP_K * c_tile
    packed = jnp.repeat((stage_row * pack + full_rows).T, c_tile, axis=1)
    picked = jnp.sum(jnp.where(hot, packed[:, None, :], 0), axis=0)
    run_stage, run_full = picked // pack, picked % pack
    pos = jnp.where(rank < run_full, run_stage + rank, rest_base + top_e * RUN_ROWS + rank - run_full)
    slot_tok = (jnp.arange(n_tok, dtype=jnp.int32) // c_tile) % 2
    pos_addr = (pos + slot_tok[None, :] * stage_rows) * SUB
    per_tile = lambda a: a.reshape(TOP_K, n_ct, c_tile).transpose(1, 2, 0).reshape(n_ct, 1, TOP_K * c_tile)
    pos3, gate3 = per_tile(pos_addr), per_tile(gates)
    smem_blk = lambda width, f: pl.BlockSpec((1, 1, width), f, memory_space=pltpu.SMEM)

    x_sorted = pl.pallas_call(
        _dispatch_kernel,
        grid_spec=pltpu.PrefetchScalarGridSpec(
            num_scalar_prefetch=2,
            grid=(n_ct,),
            in_specs=[smem_blk(TOP_K * c_tile, lambda i, lo, hi: (i, 0, 0)),
                      smem_blk(meta_w, lambda i, lo, hi: (i, 0, 0)),
                      smem_blk(meta_w, lambda i, lo, hi: (jnp.maximum(i - 1, 0), 0, 0)),
                      smem_blk(rest_w, lambda i, lo, hi: (i, 0, 0)),
                      smem_blk(rest_w, lambda i, lo, hi: (jnp.maximum(i - 1, 0), 0, 0)),
                      pl.BlockSpec(memory_space=pl.ANY)],
            out_specs=pl.BlockSpec(memory_space=pl.ANY),
            scratch_shapes=[pltpu.VMEM((2, c_tile * SUB, LANES), F32),
                            pltpu.VMEM((2 * stage_rows * SUB, LANES), F32), pltpu.VMEM((SUB, LANES), F32),
                            pltpu.SemaphoreType.DMA((2,)), pltpu.SemaphoreType.DMA((2,)),
                            pltpu.SemaphoreType.DMA((1,))]),
        out_shape=jax.ShapeDtypeStruct((n_rows * SUB, LANES), F32),
        compiler_params=pltpu.CompilerParams(dimension_semantics=("arbitrary",), vmem_limit_bytes=VMEM_LIMIT),
        name="moe_dispatch",
    )(pad_start + counts, pad_end, pos3, meta, meta, rest, rest, h1)

    last_act = lambda i, na: jnp.minimum(i, na[0] - 1)
    e_mat = lambda shape: pl.BlockSpec((None,) + shape, lambda i, be, na: (be[i], 0, 0))
    y_sorted = pl.pallas_call(
        _experts_kernel,
        grid_spec=pltpu.PrefetchScalarGridSpec(
            num_scalar_prefetch=2,
            grid=(n_blocks,),
            in_specs=[pl.BlockSpec((MOE_BLOCK * SUB, LANES), lambda i, be, na: (last_act(i, na), 0)),
                      e_mat((d, d_ff)), e_mat((d, d_ff)), e_mat((1, d_ff)), e_mat((1, d_ff)),
                      e_mat((d_ff, d)), e_mat((1, d))],
            out_specs=pl.BlockSpec((MOE_BLOCK * SUB, LANES), lambda i, be, na: (i, 0)),
            scratch_shapes=[pltpu.VMEM((d, d_ff), BF16), pltpu.VMEM((d, d_ff), BF16),
                            pltpu.VMEM((d_ff, d), BF16)]),
        out_shape=jax.ShapeDtypeStruct((n_rows * SUB, LANES), F32),
        compiler_params=pltpu.CompilerParams(dimension_semantics=("arbitrary",), vmem_limit_bytes=VMEM_LIMIT),
        name="moe_experts",
    )(block_e, n_act, x_sorted,
      moe_w_gate[0].astype(F32), moe_w_up[0].astype(F32),
      moe_b_gate[0].reshape(n_exp, 1, d_ff).astype(F32), moe_b_up[0].reshape(n_exp, 1, d_ff).astype(F32),
      moe_w_down[0].astype(F32), moe_b_down[0].reshape(n_exp, 1, d).astype(F32))

    out = pl.pallas_call(
        functools.partial(_combine_kernel, alpha, n_exp),
        grid=(n_ct,),
        in_specs=[smem_blk(TOP_K * c_tile, lambda i: (i, 0, 0)),
                  smem_blk(TOP_K * c_tile, lambda i: (i, 0, 0)),
                  smem_blk(meta_w, lambda i: (i, 0, 0)),
                  smem_blk(meta_w, lambda i: (jnp.minimum(i + 1, n_ct - 1), 0, 0)),
                  smem_blk(LANES, lambda i: (i, 0, 0)),
                  smem_blk(LANES, lambda i: (jnp.minimum(i + 1, n_ct - 1), 0, 0)),
                  pl.BlockSpec((c_tile * SUB, LANES), lambda i: (i, 0)),
                  _full((1, d)), _full((1, d)),
                  pl.BlockSpec(memory_space=pl.ANY)],
        out_specs=pl.BlockSpec((c_tile, d), lambda i: (i, 0)),
        out_shape=jax.ShapeDtypeStruct((n_tok, d), F32),
        scratch_shapes=[pltpu.VMEM((2 * stage_rows * SUB, LANES), F32), pltpu.VMEM((c_tile * SUB, LANES), F32),
                        pltpu.SemaphoreType.DMA((2,))],
        compiler_params=pltpu.CompilerParams(dimension_semantics=("arbitrary",), vmem_limit_bytes=VMEM_LIMIT),
        name="moe_combine",
    )(pos3, gate3, meta, meta, tails, tails, h1, row(ln2_g[0]), row(ln2_b[0]), y_sorted)
    return out.reshape(batch, seq, d).astype(x.dtype)
```

```python
import functools

import numpy as np
import jax
import jax.numpy as jnp
from jax import lax
from jax.experimental import pallas as pl
from jax.experimental.pallas import tpu as pltpu

F32 = jnp.float32
BF16 = jnp.bfloat16

CHUNK = 64
N_META = 16
PAIR = 2 * CHUNK
GLA_HEADS = 4
GLA_DK = 64
GLA_DV = 128
GLA_KEY = GLA_HEADS * GLA_DK
GLA_VAL = GLA_HEADS * GLA_DV
GLA_RANK = 16
GLA_TAU = 16.0
SSD_INNER = 512
SSD_HEADS = 8
SSD_HEADDIM = 64
SSD_GROUPS = 2
SSD_STATE = 128
SSD_GROUP_W = SSD_INNER // SSD_GROUPS
SSD_CONV = 4
SSD_CONV_CH = SSD_INNER + 2 * SSD_GROUPS * SSD_STATE
TOP_K = 4
SWIGLU_LIMIT = 7.0
SWIGLU_ALPHA = 1.702
MOE_BLOCK = 512
LN_EPS = 1e-5
RMS_EPS = 1e-6
LANES = 128

C_Q, C_K, C_V, C_OG, C_Z, C_XBC, C_MISC, C_END = 0, 256, 512, 1024, 1536, 2048, 3072, 3200
MISC_A1 = 0
MISC_DT = 16
CONV_PAD = 8

MIXER_TILE = 256
MIXER_SEQS = 2
SEQ_SKEW = 5
VMEM_LIMIT = 56 * 1024 * 1024


def _dot(a, b):
    return jnp.dot(a, b, preferred_element_type=F32)


def _split_hi_lo(x):
    hi = x.astype(BF16)
    lo = (x - hi.astype(F32)).astype(BF16)
    return hi, lo


def _log_sigmoid(x):
    return jnp.minimum(x, 0.0) - jnp.log1p(jnp.exp(-jnp.abs(x)))


def _softplus(x):
    return jnp.maximum(x, 0.0) + jnp.log1p(jnp.exp(-jnp.abs(x)))


def _silu(x):
    return x * jax.nn.sigmoid(x)


def _layer_norm(t, g, b):
    mu = jnp.mean(t, axis=-1, keepdims=True)
    tc = t - mu
    var = jnp.mean(tc * tc, axis=-1, keepdims=True)
    return tc * lax.rsqrt(var + LN_EPS) * g + b


def _group_rms(t, g, width):
    outs = []
    for s in range(0, t.shape[-1], width):
        seg = t[:, s:s + width]
        ms = jnp.mean(seg * seg, axis=-1, keepdims=True)
        outs.append(seg * lax.rsqrt(ms + RMS_EPS) * g[:, s:s + width])
    return jnp.concatenate(outs, axis=-1)


def _run(steps):
    try:
        while True:
            next(steps)
    except StopIteration as done:
        return done.value


def _mixer_tile(h, tile, valid_col, valid_row, p, s_gla, s_ssd, xbc_buf, mix_buf, need_out):
    hb = h.astype(BF16)
    w_in = p["w_in"]

    def proj(lo, hi):
        r = _dot(hb, w_in[:, lo:hi])
        if valid_col is not None:
            r = jnp.where(valid_col, r, 0.0)
        return r

    misc = proj(C_MISC, C_END)

    la = _log_sigmoid(_dot(misc.astype(BF16), p["w_a2"][...]) + p["b_a"][...]) * (1.0 / GLA_TAU)
    if valid_col is not None:
        la = jnp.where(valid_col, la, 0.0)
    la_hi, la_lo = _split_hi_lo(la)
    r = _dot(p["uo"][...], jnp.concatenate([la_hi, la_lo], axis=1))
    dec_exp = r[0:tile, 0:GLA_KEY] + r[0:tile, GLA_KEY:]
    tot_rows = r[tile:, 0:GLA_KEY] + r[tile:, GLA_KEY:]
    gla_dec_t = jnp.exp(tot_rows.T)
    kd = proj(C_K, C_V) * jnp.exp(dec_exp)
    v_bf = proj(C_V, C_OG).astype(BF16)

    yield
    misc_t = misc.T
    dt = _softplus(misc_t[MISC_DT:MISC_DT + SSD_HEADS, :] + p["dt_bias"][...])
    if valid_row is not None:
        dt = jnp.where(valid_row, dt, 0.0)
    dta = dt * (-jnp.exp(p["a_log"][...]))
    d_hi, d_lo = _split_hi_lo(dta)
    r2 = _dot(jnp.concatenate([d_hi, d_lo], axis=0), p["uto"][...])
    w = jnp.exp(r2[0:8, 0:tile] + r2[8:16, 0:tile]) * dt
    ssd_dec = jnp.exp(r2[0:8, tile:] + r2[8:16, tile:])
    w_hi, w_lo = _split_hi_lo(w)
    c_hi, c_lo = _split_hi_lo(ssd_dec)
    packed = jnp.concatenate(
        [w_hi.astype(F32), w_lo.astype(F32), c_hi.astype(F32), c_lo.astype(F32),
         jnp.zeros((LANES - 4 * SSD_HEADS, tile), F32)], axis=0)
    expanded = _dot(packed.T.astype(BF16), p["eexp"][...])
    w_exp = expanded[:, 0:SSD_INNER]
    ssd_dec_exp = expanded[:, SSD_INNER:]

    yield
    xbc_buf[CONV_PAD:CONV_PAD + tile, :] = proj(C_XBC, C_MISC)
    acc = p["conv_b"][...] + p["conv_w"][0:1, :] * xbc_buf[pl.ds(CONV_PAD - 3, tile), :]
    for j in range(1, SSD_CONV):
        acc = acc + p["conv_w"][j:j + 1, :] * xbc_buf[pl.ds(CONV_PAD - 3 + j, tile), :]
    xbc_buf[0:CONV_PAD, :] = xbc_buf[tile:tile + CONV_PAD, :]
    xa = _silu(acc)
    xs = xa[:, 0:SSD_INNER]
    bm = xa[:, SSD_INNER:SSD_INNER + SSD_GROUPS * SSD_STATE]
    cm_bf = xa[:, SSD_INNER + SSD_GROUPS * SSD_STATE:].astype(BF16)
    xw_bf = (xs * w_exp).astype(BF16)
    yield

    if need_out:
        q = proj(C_Q, C_K) * (GLA_DK ** -0.5)

    lane = lax.broadcasted_iota(jnp.int32, (1, PAIR), 1)
    for pr in range(tile // PAIR):
        rows = slice(pr * PAIR, (pr + 1) * PAIR)
        kd_t = kd[rows].T
        bm_t = bm[rows].T
        for half in range(2):
            c = 2 * pr + half
            r0 = c * CHUNK
            sel = (lane >= half * CHUNK) & (lane < (half + 1) * CHUNK)
            kd_m = jnp.where(sel, kd_t, 0.0).astype(BF16)
            upd = jnp.concatenate(
                [_dot(kd_m[hd * GLA_DK:(hd + 1) * GLA_DK], v_bf[rows, hd * GLA_DV:(hd + 1) * GLA_DV])
                 for hd in range(GLA_HEADS)], axis=0)
            s_new = gla_dec_t[:, c:c + 1] * s_gla[...] + upd
            s_gla[...] = s_new
            bm_m = jnp.where(sel, bm_t, 0.0).astype(BF16)
            ssd_new = []
            for g in range(SSD_GROUPS):
                gl = slice(g * SSD_GROUP_W, (g + 1) * SSD_GROUP_W)
                gr = slice(g * SSD_STATE, (g + 1) * SSD_STATE)
                u = _dot(bm_m[gr], xw_bf[rows, gl])
                sg = ssd_dec_exp[r0:r0 + 1, gl] * s_ssd[gr, :] + u
                s_ssd[gr, :] = sg
                ssd_new.append(sg)
            if need_out:
                s_bf = s_new.astype(BF16)
                zero = jnp.zeros((GLA_DK, GLA_DV), BF16)
                s_diag = jnp.concatenate(
                    [jnp.concatenate([s_bf[hd * GLA_DK:(hd + 1) * GLA_DK] if col == hd else zero
                                      for col in range(GLA_HEADS)], axis=1) for hd in range(GLA_HEADS)],
                    axis=0)
                mix_buf[r0:r0 + CHUNK, 0:GLA_VAL] = _dot(q[r0:r0 + CHUNK].astype(BF16), s_diag)
                for g in range(SSD_GROUPS):
                    y = _dot(cm_bf[r0:r0 + CHUNK, g * SSD_STATE:(g + 1) * SSD_STATE],
                             ssd_new[g].astype(BF16))
                    mix_buf[r0:r0 + CHUNK, GLA_VAL + g * SSD_GROUP_W:GLA_VAL + (g + 1) * SSD_GROUP_W] = y
        yield

    if not need_out:
        return None
    o_gla = _group_rms(mix_buf[:, 0:GLA_VAL], p["gla_norm_g"][...], GLA_DV) * _silu(proj(C_OG, C_Z))
    yield
    y = (mix_buf[:, GLA_VAL:] + xs * p["ssd_d"][...]) * _silu(proj(C_Z, C_XBC))
    y = _group_rms(y, p["ssd_norm_g"][...], SSD_GROUP_W)
    return o_gla, y


_MIX_PARAMS = ("w_in", "w_a2", "b_a", "conv_w", "conv_b", "dt_bias", "a_log", "uo", "uto", "eexp")
_OUT_PARAMS = ("gla_norm_g", "ssd_d", "ssd_norm_g")


def _meta_kernel(x_ref, ln_g, ln_b, *rest):
    np_ = len(_MIX_PARAMS)
    p = dict(zip(_MIX_PARAMS, rest[:np_]))
    s_gla_out, s_ssd_out, tail_out, xbc_buf = rest[np_:]
    tile = x_ref.shape[0]
    s_gla_out[...] = jnp.zeros_like(s_gla_out)
    s_ssd_out[...] = jnp.zeros_like(s_ssd_out)
    xbc_buf[0:CONV_PAD, :] = jnp.zeros((CONV_PAD, xbc_buf.shape[1]), F32)
    first_valid = tile - N_META
    valid_col = lax.broadcasted_iota(jnp.int32, (tile, 1), 0) >= first_valid
    valid_row = lax.broadcasted_iota(jnp.int32, (1, tile), 1) >= first_valid
    h = _layer_norm(x_ref[...], ln_g[...], ln_b[...])
    _run(_mixer_tile(h, tile, valid_col, valid_row, p, s_gla_out, s_ssd_out, xbc_buf, None, False))
    tail_out[...] = xbc_buf[0:CONV_PAD, :]


def _mixer_kernel(alpha, n_exp, x_ref, ln_g, ln_b, *rest):
    np_ = len(_MIX_PARAMS)
    p = dict(zip(_MIX_PARAMS, rest[:np_]))
    rest = rest[np_:]
    p.update(zip(_OUT_PARAMS, rest[:3]))
    (w_out, ln1_g, ln1_b, rw, rb, su, s_gla0, s_ssd0, tail0,
     h1_out, topi_out, gate_out, rank_out, cnt_out, base_out, tcnt_out) = rest[3:19]
    n_sub, tile = x_ref.shape[0], x_ref.shape[1]
    scratch = rest[19:]
    s_gla, s_ssd, xbc_buf, mix_buf = (scratch[k * n_sub:(k + 1) * n_sub] for k in range(4))
    run_cnt = scratch[4 * n_sub]
    b, j = pl.program_id(0), pl.program_id(1)

    @pl.when(j == 0)
    def _():
        for s in range(n_sub):
            s_gla[s][...] = s_gla0[...]
            s_ssd[s][...] = s_ssd0[...]
            xbc_buf[s][0:CONV_PAD, :] = tail0[...]

    @pl.when((b == 0) & (j == 0))
    def _():
        run_cnt[...] = jnp.zeros_like(run_cnt)

    count = [run_cnt[...]]
    pad_rows = topi_out.shape[1] - TOP_K

    def sequence(s):
        h = _layer_norm(x_ref[s], ln_g[...], ln_b[...])
        o_gla, y = yield from _mixer_tile(h, tile, None, None, p, s_gla[s], s_ssd[s], xbc_buf[s], mix_buf[s], True)
        yield
        mix = _dot(o_gla.astype(BF16), w_out[0:GLA_VAL, :]) + _dot(y.astype(BF16), w_out[GLA_VAL:, :])
        h1 = _layer_norm(alpha * h + mix, ln1_g[...], ln1_b[...])
        _store_rows(h1_out.at[s], h1)
        yield

        h_hi, h_lo = _split_hi_lo(h1)
        p_hi = _dot(h_hi, rw[...])
        logits = p_hi + (pltpu.roll(p_hi, LANES - n_exp, 1) + _dot(h_lo, rw[...]))
        lt = logits.T[0:n_exp, :] + rb[...]
        e_iota = lax.broadcasted_iota(jnp.int32, (n_exp, tile), 0).astype(F32)
        work = lt
        vals, hots = [], []
        for k in range(TOP_K):
            m = jnp.max(work, axis=0, keepdims=True)
            idx = jnp.min(jnp.where(work == m, e_iota, float(n_exp)), axis=0, keepdims=True)
            hot = e_iota == idx
            work = jnp.where(hot, -jnp.inf, work)
            vals.append(m)
            hots.append(hot)
            topi_out[s, k:k + 1, :] = idx.astype(jnp.int32)
        exps = [jnp.exp(vk - vals[0]) for vk in vals]
        denom = exps[0] + exps[1] + exps[2] + exps[3]
        for k in range(TOP_K):
            gate_out[s, k:k + 1, :] = exps[k] / denom
        member = (hots[0] | hots[1] | hots[2] | hots[3]).astype(F32).astype(BF16)
        yield
        before = _dot(member, su[...])
        for k in range(TOP_K):
            rk = jnp.sum(jnp.where(hots[k], before, 0.0), axis=0, keepdims=True)
            rank_out[s, k:k + 1, :] = rk.astype(jnp.int32)
        topi_out[s, TOP_K:, :] = jnp.zeros((pad_rows, tile), jnp.int32)
        gate_out[s, TOP_K:, :] = jnp.zeros((pad_rows, tile), F32)
        rank_out[s, TOP_K:, :] = jnp.zeros((pad_rows, tile), jnp.int32)
        tile_count = _dot(member, jnp.ones((tile, LANES), BF16))
        base_out[s, 0] = count[0]
        tcnt_out[s, 0] = tile_count
        count[0] = count[0] + tile_count

    pending = [sequence(s) for s in range(n_sub)]
    for lead in range(1, n_sub):
        for _ in range(SEQ_SKEW):
            for steps in pending[:lead]:
                next(steps)
    while pending:
        for steps in list(pending):
            try:
                next(steps)
            except StopIteration:
                pending.remove(steps)
    run_cnt[...] = count[0]
    cnt_out[...] = count[0]


def _clamped_swiglu(a, u):
    a = jnp.minimum(a, SWIGLU_LIMIT)
    u = jnp.clip(u, -SWIGLU_LIMIT, SWIGLU_LIMIT)
    return a * jax.nn.sigmoid(SWIGLU_ALPHA * a) * (u + 1.0)


SUB = 8


def _load_rows(ref, n_rows):
    return jnp.concatenate([ref[pl.ds(j, n_rows, stride=SUB), :] for j in range(SUB)], axis=1)


def _store_rows(ref, val):
    for j in range(SUB):
        ref[pl.ds(j, val.shape[0], stride=SUB), :] = val[:, j * LANES:(j + 1) * LANES]


def _tile_rows(row):
    return pl.ds(pl.multiple_of(row * SUB, SUB), SUB)


RUN_ROWS = 8
PIECE_GROUP = 4
REST_SIZES = (4, 2, 1)


def _run_rows(row):
    return pl.ds(pl.multiple_of(row * SUB, SUB), RUN_ROWS * SUB)


def _stage_tile(addr):
    return pl.ds(pl.multiple_of(addr, SUB), SUB)


def _for_pieces(pieces, fn):
    def body(g, _):
        for u in range(PIECE_GROUP):
            p = g * PIECE_GROUP + u
            fn(p * RUN_ROWS, pieces[0, 0, 1 + p], u % 2)
        return 0
    lax.fori_loop(0, pieces[0, 0, 0], body, 0)


def _dispatch_kernel(fill_lo, fill_hi, pos, meta_cur, meta_prv, tail_cur, tail_prv, h_hbm, xs_hbm,
                     tbuf, stage, zrow, lsem, rsem, zsem):
    i = pl.program_id(0)
    n = pl.num_programs(0)
    tile = pos.shape[-1] // TOP_K
    slot = i % 2
    n_exp = fill_lo.shape[0]
    stage_rows = stage.shape[0] // (2 * SUB)
    rest_base = stage_rows - n_exp * RUN_ROWS

    def tile_load(j, s):
        return pltpu.make_async_copy(h_hbm.at[pl.ds(j * tile * SUB, tile * SUB), :], tbuf.at[s], lsem.at[s])

    def rows_copy(s, stage_row, sorted_row, n_rows):
        src = stage.at[pl.ds(pl.multiple_of((s * stage_rows + stage_row) * SUB, SUB), n_rows * SUB), :]
        dst = xs_hbm.at[pl.ds(pl.multiple_of(sorted_row * SUB, SUB), n_rows * SUB), :]
        return pltpu.make_async_copy(src, dst, rsem.at[s])

    def for_rest(rest, s, wait):
        for c, size in enumerate(REST_SIZES):
            first = 8 + 2 * n_exp * c

            def body(q, _, first=first, size=size):
                if wait:
                    rows_copy(s, 0, 0, size).wait()
                else:
                    rows_copy(s, rest[0, 0, first + 2 * q + 1], rest[0, 0, first + 2 * q], size).start()
                return 0
            lax.fori_loop(0, rest[0, 0, c], body, 0)

    def zero_copy(dst_row):
        return pltpu.make_async_copy(zrow, xs_hbm.at[_tile_rows(dst_row), :], zsem.at[0])

    @pl.when(i == 0)
    def _():
        tile_load(0, 0).start()
        zrow[...] = jnp.zeros_like(zrow)
        stage[...] = jnp.zeros_like(stage)
        for wait in (False, True):
            def per_expert(e, _, wait=wait):
                def body(row, _):
                    if wait:
                        zero_copy(0).wait()
                    else:
                        zero_copy(row).start()
                    return 0
                lax.fori_loop(fill_lo[e], fill_hi[e], body, 0)
                return 0
            lax.fori_loop(0, n_exp, per_expert, 0)
        tail_lo = fill_hi[n_exp - 1]
        tail_pieces = (xs_hbm.shape[0] // SUB - tail_lo) // RUN_ROWS
        for wait in (False, True):
            def tail(c, _, wait=wait):
                cp = pltpu.make_async_copy(stage.at[_run_rows(0), :], xs_hbm.at[_run_rows(tail_lo + c * RUN_ROWS), :],
                                           zsem.at[0])
                if wait:
                    cp.wait()
                else:
                    cp.start()
                return 0
            lax.fori_loop(0, tail_pieces, tail, 0)

    @pl.when(i + 1 < n)
    def _():
        tile_load(i + 1, 1 - slot).start()

    tile_load(i, slot).wait()

    def pack(t, _):
        row = tbuf[slot, _tile_rows(t), :]
        for k in range(TOP_K):
            stage[_stage_tile(pos[0, 0, TOP_K * t + k]), :] = row
        return 0
    lax.fori_loop(0, tile, pack, 0, unroll=8)

    _for_pieces(meta_cur, lambda off, row, pri: rows_copy(slot, off, row, RUN_ROWS).start(priority=pri))
    for_rest(tail_cur, slot, False)

    def wait_tile(rest, s):
        rows_copy(s, 0, 0, TOP_K * tile).wait()

        def body(q, _):
            rows_copy(s, 0, 0, RUN_ROWS).wait()
            return 0
        lax.fori_loop(0, rest[0, 0, len(REST_SIZES)], body, 0)

    @pl.when(i > 0)
    def _():
        wait_tile(tail_prv, 1 - slot)

    @pl.when(i == n - 1)
    def _():
        wait_tile(tail_cur, slot)


def _experts_kernel(blk_e, n_act, x_ref, wg, wu, bg, bu, wd, bd, y_out, wg_bf, wu_bf, wd_bf):
    i = pl.program_id(0)
    active = i < n_act[0]
    new_expert = (i == 0) | (blk_e[i] != blk_e[jnp.maximum(i - 1, 0)])

    @pl.when(active & new_expert)
    def _():
        wg_bf[...] = wg[...].astype(BF16)
        wu_bf[...] = wu[...].astype(BF16)
        wd_bf[...] = wd[...].astype(BF16)

    @pl.when(active)
    def _():
        xb = _load_rows(x_ref, x_ref.shape[0] // SUB).astype(BF16)
        hid = _clamped_swiglu(_dot(xb, wg_bf[...]) + bg[...], _dot(xb, wu_bf[...]) + bu[...])
        _store_rows(y_out, _dot(hid.astype(BF16), wd_bf[...]) + bd[...])

    @pl.when(i >= n_act[0])
    def _():
        y_out[...] = jnp.zeros_like(y_out)


def _combine_kernel(alpha, n_exp, pos, gate, meta_cur, meta_nxt, tail_cur, tail_nxt, h1, ln_g, ln_b, y_hbm, out,
                    stage, ff_rows, sem):
    i = pl.program_id(0)
    n = pl.num_programs(0)
    tile = out.shape[0]
    slot = i % 2
    stage_rows = stage.shape[0] // (2 * SUB)
    rest_base = stage_rows - n_exp * RUN_ROWS

    def run_copy(s, stage_row, sorted_row):
        return pltpu.make_async_copy(y_hbm.at[_run_rows(sorted_row), :],
                                     stage.at[_run_rows(s * stage_rows + stage_row), :], sem.at[s])

    def fetch(meta, tails, s, wait):
        if wait:
            _for_pieces(meta, lambda off, row, pri: run_copy(s, 0, 0).wait())
            pltpu.make_async_copy(y_hbm.at[pl.ds(0, n_exp * RUN_ROWS * SUB), :],
                                  stage.at[pl.ds(0, n_exp * RUN_ROWS * SUB), :], sem.at[s]).wait()
            return
        _for_pieces(meta, lambda off, row, pri: run_copy(s, off, row).start(priority=pri))

        def rest(g, _):
            for u in range(PIECE_GROUP):
                e = g * PIECE_GROUP + u
                run_copy(s, rest_base + e * RUN_ROWS, tails[0, 0, e]).start(priority=u % 2)
            return 0
        lax.fori_loop(0, n_exp // PIECE_GROUP, rest, 0)

    @pl.when(i == 0)
    def _():
        fetch(meta_cur, tail_cur, 0, False)

    @pl.when(i + 1 < n)
    def _():
        fetch(meta_nxt, tail_nxt, 1 - slot, False)

    fetch(meta_cur, tail_cur, slot, True)

    def gather(t, _):
        acc = gate[0, 0, TOP_K * t] * stage[_stage_tile(pos[0, 0, TOP_K * t]), :]
        for k in range(1, TOP_K):
            acc = acc + gate[0, 0, TOP_K * t + k] * stage[_stage_tile(pos[0, 0, TOP_K * t + k]), :]
        ff_rows[_tile_rows(t), :] = acc
        return 0
    lax.fori_loop(0, tile, gather, 0, unroll=8)

    out[...] = _layer_norm(alpha * _load_rows(h1, tile) + _load_rows(ff_rows, tile), ln_g[...], ln_b[...])


def _block_constants(tile):
    t = np.arange(tile)
    same = (t[:, None] // CHUNK) == (t[None, :] // CHUNK)
    later = same & (t[None, :] > t[:, None])
    chunk_rows = np.zeros((LANES, tile), np.float32)
    chunk_rows[t // CHUNK, t] = 1.0
    uo = np.concatenate([later.astype(np.float32), chunk_rows], axis=0)
    uto = np.concatenate([later.T.astype(np.float32), same.astype(np.float32)], axis=1)
    return jnp.asarray(uo, BF16), jnp.asarray(uto, BF16)


def _expand_constant():
    e = np.zeros((LANES, 2 * SSD_INNER), np.float32)
    for part in range(4):
        for hd in range(SSD_HEADS):
            base = (part // 2) * SSD_INNER + hd * SSD_HEADDIM
            e[part * SSD_HEADS + hd, base:base + SSD_HEADDIM] = 1.0
    return jnp.asarray(e, BF16)


def _full(shape):
    return pl.BlockSpec(shape, lambda *_: (0,) * len(shape))


def kernel(x, meta_tokens, ln_in_g, ln_in_b, w_in, gla_w_a2, gla_b_a, gla_norm_g, ssd_conv_w, ssd_conv_b,
           ssd_dt_bias, ssd_a_log, ssd_d, ssd_norm_g, w_out, ln1_g, ln1_b, router_w, router_b, moe_w_gate,
           moe_w_up, moe_b_gate, moe_b_up, moe_w_down, moe_b_down, ln2_g, ln2_b):
    batch, seq, d = x.shape
    depth = w_in.shape[0]
    assert depth == 1, "single-layer stack"
    n_exp = router_w.shape[-1]
    d_ff = moe_w_gate.shape[-1]
    alpha = (2.0 * depth) ** 0.25
    tile = MIXER_TILE
    assert seq % tile == 0 and d == 1024
    n_tok = batch * seq
    row = lambda a: a.reshape(1, -1).astype(F32)

    wi = w_in[0]
    o_a1 = 1536
    o_z = o_a1 + GLA_RANK
    o_xbc = o_z + SSD_INNER
    o_dt = o_xbc + SSD_CONV_CH
    misc_w = jnp.zeros((d, LANES), F32)
    misc_w = misc_w.at[:, MISC_A1:MISC_A1 + GLA_RANK].set(wi[:, o_a1:o_z])
    misc_w = misc_w.at[:, MISC_DT:MISC_DT + SSD_HEADS].set(wi[:, o_dt:o_dt + SSD_HEADS])
    w_in_r = jnp.concatenate([wi[:, 0:o_a1], wi[:, o_z:o_xbc], wi[:, o_xbc:o_dt], misc_w], axis=1).astype(BF16)
    w_a2 = jnp.zeros((LANES, GLA_KEY), F32).at[MISC_A1:MISC_A1 + GLA_RANK].set(gla_w_a2[0]).astype(BF16)
    mix_params = dict(
        w_in=w_in_r, w_a2=w_a2, b_a=row(gla_b_a[0]), conv_w=ssd_conv_w[0].astype(F32),
        conv_b=row(ssd_conv_b[0]), dt_bias=ssd_dt_bias[0].reshape(-1, 1).astype(F32),
        a_log=ssd_a_log[0].reshape(-1, 1).astype(F32), eexp=_expand_constant())
    out_params = dict(gla_norm_g=row(gla_norm_g[0]), ssd_d=row(jnp.repeat(ssd_d[0], SSD_HEADDIM)),
                      ssd_norm_g=row(ssd_norm_g[0]))

    def mix_args(t):
        uo, uto = _block_constants(t)
        vals = dict(mix_params, uo=uo, uto=uto)
        return [vals[k] for k in _MIX_PARAMS]

    m_tile = PAIR
    x_meta = jnp.concatenate([jnp.zeros((m_tile - N_META, d), F32), meta_tokens.astype(F32)], axis=0)
    meta_in = [x_meta, row(ln_in_g), row(ln_in_b)] + mix_args(m_tile)
    s_gla0, s_ssd0, tail0 = pl.pallas_call(
        _meta_kernel,
        out_shape=(jax.ShapeDtypeStruct((GLA_KEY, GLA_DV), F32),
                   jax.ShapeDtypeStruct((SSD_GROUPS * SSD_STATE, SSD_GROUP_W), F32),
                   jax.ShapeDtypeStruct((CONV_PAD, SSD_CONV_CH), F32)),
        scratch_shapes=[pltpu.VMEM((CONV_PAD + m_tile, SSD_CONV_CH), F32)],
        compiler_params=pltpu.CompilerParams(vmem_limit_bytes=VMEM_LIMIT),
        name="meta_state",
    )(*meta_in)

    n_j = seq // tile
    rw_hi, rw_lo = _split_hi_lo(router_w[0].astype(F32))
    rw = jnp.zeros((d, LANES), BF16).at[:, 0:n_exp].set(rw_hi).at[:, n_exp:2 * n_exp].set(rw_lo)
    rb = router_b[0].reshape(n_exp, 1).astype(F32)
    su = jnp.asarray(np.triu(np.ones((tile, tile), np.float32), 1), BF16)
    args = ([x, row(ln_in_g), row(ln_in_b)] + mix_args(tile) + [out_params[k] for k in _OUT_PARAMS]
            + [w_out[0].astype(BF16), row(ln1_g[0]), row(ln1_b[0]), rw, rb, su, s_gla0, s_ssd0, tail0])
    n_sub = MIXER_SEQS
    assert batch % n_sub == 0
    in_specs = [pl.BlockSpec((n_sub, tile, d), lambda b, j: (b, j, 0))] + [_full(a.shape) for a in args[1:]]
    tok_blk = pl.BlockSpec((n_sub, 8, tile), lambda b, j: (b, 0, j))
    tile_tab = pl.BlockSpec((n_sub, 1, n_exp, LANES), lambda b, j: (b, j, 0, 0))
    h1, topi, gates, rank, cnt, base, tile_cnt = pl.pallas_call(
        functools.partial(_mixer_kernel, alpha, n_exp),
        grid=(batch // n_sub, n_j),
        in_specs=in_specs,
        out_specs=(pl.BlockSpec((n_sub, tile * SUB, LANES), lambda b, j: (b, j, 0)),
                   tok_blk, tok_blk, tok_blk, _full((n_exp, LANES)), tile_tab, tile_tab),
        out_shape=(jax.ShapeDtypeStruct((batch, seq * SUB, LANES), F32),
                   jax.ShapeDtypeStruct((batch, 8, seq), jnp.int32),
                   jax.ShapeDtypeStruct((batch, 8, seq), F32),
                   jax.ShapeDtypeStruct((batch, 8, seq), jnp.int32),
                   jax.ShapeDtypeStruct((n_exp, LANES), F32),
                   jax.ShapeDtypeStruct((batch, n_j, n_exp, LANES), F32),
                   jax.ShapeDtypeStruct((batch, n_j, n_exp, LANES), F32)),
        scratch_shapes=([pltpu.VMEM((GLA_KEY, GLA_DV), F32)] * n_sub
                        + [pltpu.VMEM((SSD_GROUPS * SSD_STATE, SSD_GROUP_W), F32)] * n_sub
                        + [pltpu.VMEM((CONV_PAD + tile, SSD_CONV_CH), F32)] * n_sub
                        + [pltpu.VMEM((tile, d), F32)] * n_sub
                        + [pltpu.VMEM((n_exp, LANES), F32)]),
        compiler_params=pltpu.CompilerParams(dimension_semantics=("arbitrary", "arbitrary"),
                                             vmem_limit_bytes=VMEM_LIMIT),
        name="mixer",
    )(*args)
    h1 = h1.reshape(n_tok * SUB, LANES)
    per_tok = lambda a: a[:, :TOP_K].transpose(1, 0, 2).reshape(TOP_K, n_tok)
    top_e, gates, rank = per_tok(topi), per_tok(gates), per_tok(rank)

    counts = cnt[:, 0].astype(jnp.int32)
    padded = (counts + RUN_ROWS + MOE_BLOCK - 1) // MOE_BLOCK * MOE_BLOCK
    pad_end = jnp.cumsum(padded)
    pad_start = pad_end - padded
    n_blocks = -(-(n_tok * TOP_K) // MOE_BLOCK) + n_exp + 1
    n_rows = n_blocks * MOE_BLOCK
    blk_lo = jnp.arange(n_blocks, dtype=jnp.int32) * MOE_BLOCK
    block_e = jnp.minimum(jnp.sum((blk_lo[:, None] >= pad_end[None, :]).astype(jnp.int32), axis=1), n_exp - 1)
    n_act = (pad_end[-1] // MOE_BLOCK).astype(jnp.int32).reshape(1)
    c_tile = tile
    n_ct = n_tok // c_tile
    run_base = base[..., 0].astype(jnp.int32).reshape(n_ct, n_exp)
    run_len = tile_cnt[..., 0].astype(jnp.int32).reshape(n_ct, n_exp)
    src_row = pad_start[None, :] + run_base
    run_pieces = run_len // RUN_ROWS
    full_rows = run_pieces * RUN_ROWS
    piece_end = jnp.cumsum(run_pieces, axis=1)
    first_piece = piece_end - run_pieces
    stage_row = first_piece * RUN_ROWS
    max_pieces = -(-(TOP_K * c_tile // RUN_ROWS) // PIECE_GROUP) * PIECE_GROUP
    p_ids = jnp.arange(max_pieces, dtype=jnp.int32)
    in_run = (p_ids[None, :, None] >= first_piece[:, None, :]) & (p_ids[None, :, None] < piece_end[:, None, :])
    piece_row = jnp.sum(jnp.where(in_run, (src_row - stage_row)[:, None, :], 0), axis=2)
    piece_row = piece_row + p_ids[None, :] * RUN_ROWS
    n_pieces = piece_end[:, -1:]
    tile_parity = (jnp.arange(n_ct, dtype=jnp.int32) % 2)[:, None]
    spare_row = (n_blocks - 1) * MOE_BLOCK + (tile_parity * PIECE_GROUP + p_ids[None, :] - n_pieces) * RUN_ROWS
    piece_row = jnp.where(p_ids[None, :] < n_pieces, piece_row, jnp.clip(spare_row, 0, n_rows - RUN_ROWS))
    n_groups = (n_pieces + PIECE_GROUP - 1) // PIECE_GROUP
    meta_w = -(-(max_pieces + 1) // LANES) * LANES
    meta = jnp.concatenate([n_groups, piece_row,
                            jnp.zeros((n_ct, meta_w - 1 - max_pieces), jnp.int32)], axis=1).reshape(n_ct, 1, meta_w)
    rest_base = max_pieces * RUN_ROWS
    stage_rows = rest_base + n_exp * RUN_ROWS
    tails = jnp.concatenate([src_row + full_rows, run_len - full_rows,
                             jnp.zeros((n_ct, LANES - 2 * n_exp), jnp.int32)], axis=1).reshape(n_ct, 1, LANES)
    rest_len = run_len - full_rows
    e_row = jnp.arange(n_exp, dtype=jnp.int32)
    rest_cnt, rest_lists = [], []
    for size in REST_SIZES:
        has = (rest_len & size) != 0
        slot_in_list = jnp.cumsum(has, axis=1) - has
        before = rest_len & (7 & ~(2 * size - 1))
        sel = has[:, None, :] & (slot_in_list[:, None, :] == e_row[None, :, None])
        pick_run = lambda v: jnp.sum(jnp.where(sel, v[:, None, :], 0), axis=2)
        rest_lists.append(jnp.stack([pick_run(src_row + full_rows + before),
                                     pick_run(rest_base + e_row[None, :] * RUN_ROWS + before)], axis=2)
                          .reshape(n_ct, 2 * n_exp))
        rest_cnt.append(jnp.sum(has, axis=1, keepdims=True).astype(jnp.int32))
    rest_w = -(-(8 + 2 * n_exp * len(REST_SIZES)) // LANES) * LANES
    rest_cnt.append(n_groups * PIECE_GROUP - n_pieces)
    rest = jnp.concatenate(rest_cnt + [jnp.zeros((n_ct, 8 - len(rest_cnt)), jnp.int32)] + rest_lists
                           + [jnp.zeros((n_ct, rest_w - 8 - 2 * n_exp * len(REST_SIZES)), jnp.int32)],
                           axis=1).reshape(n_ct, 1, rest_w)
    e_ids = e_row[:, None, None]
    hot = top_e[None] == e_ids
    pack = 2 * TOP_K * c_tile
    packed = jnp.repeat((stage_row * pack + full_rows).T, c_tile, axis=1)
    picked = jnp.sum(jnp.where(hot, packed[:, None, :], 0), axis=0)
    run_stage, run_full = picked // pack, picked % pack
    pos = jnp.where(rank < run_full, run_stage + rank, rest_base + top_e * RUN_ROWS + rank - run_full)
    slot_tok = (jnp.arange(n_tok, dtype=jnp.int32) // c_tile) % 2
    pos_addr = (pos + slot_tok[None, :] * stage_rows) * SUB
    per_tile = lambda a: a.reshape(TOP_K, n_ct, c_tile).transpose(1, 2, 0).reshape(n_ct, 1, TOP_K * c_tile)
    pos3, gate3 = per_tile(pos_addr), per_tile(gates)
    smem_blk = lambda width, f: pl.BlockSpec((1, 1, width), f, memory_space=pltpu.SMEM)

    x_sorted = pl.pallas_call(
        _dispatch_kernel,
        grid_spec=pltpu.PrefetchScalarGridSpec(
            num_scalar_prefetch=2,
            grid=(n_ct,),
            in_specs=[smem_blk(TOP_K * c_tile, lambda i, lo, hi: (i, 0, 0)),
                      smem_blk(meta_w, lambda i, lo, hi: (i, 0, 0)),
                      smem_blk(meta_w, lambda i, lo, hi: (jnp.maximum(i - 1, 0), 0, 0)),
                      smem_blk(rest_w, lambda i, lo, hi: (i, 0, 0)),
                      smem_blk(rest_w, lambda i, lo, hi: (jnp.maximum(i - 1, 0), 0, 0)),
                      pl.BlockSpec(memory_space=pl.ANY)],
            out_specs=pl.BlockSpec(memory_space=pl.ANY),
            scratch_shapes=[pltpu.VMEM((2, c_tile * SUB, LANES), F32),
                            pltpu.VMEM((2 * stage_rows * SUB, LANES), F32), pltpu.VMEM((SUB, LANES), F32),
                            pltpu.SemaphoreType.DMA((2,)), pltpu.SemaphoreType.DMA((2,)),
                            pltpu.SemaphoreType.DMA((1,))]),
        out_shape=jax.ShapeDtypeStruct((n_rows * SUB, LANES), F32),
        compiler_params=pltpu.CompilerParams(dimension_semantics=("arbitrary",), vmem_limit_bytes=VMEM_LIMIT),
        name="moe_dispatch",
    )(pad_start + counts, pad_end, pos3, meta, meta, rest, rest, h1)

    last_act = lambda i, na: jnp.minimum(i, na[0] - 1)
    e_mat = lambda shape: pl.BlockSpec((None,) + shape, lambda i, be, na: (be[i], 0, 0))
    y_sorted = pl.pallas_call(
        _experts_kernel,
        grid_spec=pltpu.PrefetchScalarGridSpec(
            num_scalar_prefetch=2,
            grid=(n_blocks,),
            in_specs=[pl.BlockSpec((MOE_BLOCK * SUB, LANES), lambda i, be, na: (last_act(i, na), 0)),
                      e_mat((d, d_ff)), e_mat((d, d_ff)), e_mat((1, d_ff)), e_mat((1, d_ff)),
                      e_mat((d_ff, d)), e_mat((1, d))],
            out_specs=pl.BlockSpec((MOE_BLOCK * SUB, LANES), lambda i, be, na: (i, 0)),
            scratch_shapes=[pltpu.VMEM((d, d_ff), BF16), pltpu.VMEM((d, d_ff), BF16),
                            pltpu.VMEM((d_ff, d), BF16)]),
        out_shape=jax.ShapeDtypeStruct((n_rows * SUB, LANES), F32),
        compiler_params=pltpu.CompilerParams(dimension_semantics=("arbitrary",), vmem_limit_bytes=VMEM_LIMIT),
        name="moe_experts",
    )(block_e, n_act, x_sorted,
      moe_w_gate[0].astype(F32), moe_w_up[0].astype(F32),
      moe_b_gate[0].reshape(n_exp, 1, d_ff).astype(F32), moe_b_up[0].reshape(n_exp, 1, d_ff).astype(F32),
      moe_w_down[0].astype(F32), moe_b_down[0].reshape(n_exp, 1, d).astype(F32))

    out = pl.pallas_call(
        functools.partial(_combine_kernel, alpha, n_exp),
        grid=(n_ct,),
        in_specs=[smem_blk(TOP_K * c_tile, lambda i: (i, 0, 0)),
                  smem_blk(TOP_K * c_tile, lambda i: (i, 0, 0)),
                  smem_blk(meta_w, lambda i: (i, 0, 0)),
                  smem_blk(meta_w, lambda i: (jnp.minimum(i + 1, n_ct - 1), 0, 0)),
                  smem_blk(LANES, lambda i: (i, 0, 0)),
                  smem_blk(LANES, lambda i: (jnp.minimum(i + 1, n_ct - 1), 0, 0)),
                  pl.BlockSpec((c_tile * SUB, LANES), lambda i: (i, 0)),
                  _full((1, d)), _full((1, d)),
                  pl.BlockSpec(memory_space=pl.ANY)],
        out_specs=pl.BlockSpec((c_tile, d), lambda i: (i, 0)),
        out_shape=jax.ShapeDtypeStruct((n_tok, d), F32),
        scratch_shapes=[pltpu.VMEM((2 * stage_rows * SUB, LANES), F32), pltpu.VMEM((c_tile * SUB, LANES), F32),
                        pltpu.SemaphoreType.DMA((2,))],
        compiler_params=pltpu.CompilerParams(dimension_semantics=("arbitrary",), vmem_limit_bytes=VMEM_LIMIT),
        name="moe_combine",
    )(pos3, gate3, meta, meta, tails, tails, h1, row(ln2_g[0]), row(ln2_b[0]), y_sorted)
    return out.reshape(batch, seq, d).astype(x.dtype)
```

```python
import functools

import numpy as np
import jax
import jax.numpy as jnp
from jax import lax
from jax.experimental import pallas as pl
from jax.experimental.pallas import tpu as pltpu

F32 = jnp.float32
BF16 = jnp.bfloat16

CHUNK = 64
N_META = 16
PAIR = 2 * CHUNK
GLA_HEADS = 4
GLA_DK = 64
GLA_DV = 128
GLA_KEY = GLA_HEADS * GLA_DK
GLA_VAL = GLA_HEADS * GLA_DV
GLA_RANK = 16
GLA_TAU = 16.0
SSD_INNER = 512
SSD_HEADS = 8
SSD_HEADDIM = 64
SSD_GROUPS = 2
SSD_STATE = 128
SSD_GROUP_W = SSD_INNER // SSD_GROUPS
SSD_CONV = 4
SSD_CONV_CH = SSD_INNER + 2 * SSD_GROUPS * SSD_STATE
TOP_K = 4
SWIGLU_LIMIT = 7.0
SWIGLU_ALPHA = 1.702
MOE_BLOCK = 768
LN_EPS = 1e-5
RMS_EPS = 1e-6
LANES = 128

C_Q, C_K, C_V, C_OG, C_Z, C_XBC, C_MISC, C_END = 0, 256, 512, 1024, 1536, 2048, 3072, 3200
MISC_A1 = 0
MISC_DT = 16
CONV_PAD = 8

MIXER_TILE = 256
MIXER_SEQS = 2
SEQ_SKEW = 5
VMEM_LIMIT = 56 * 1024 * 1024


def _dot(a, b):
    return jnp.dot(a, b, preferred_element_type=F32)


def _split_hi_lo(x):
    hi = x.astype(BF16)
    lo = (x - hi.astype(F32)).astype(BF16)
    return hi, lo


def _log_sigmoid(x):
    return jnp.minimum(x, 0.0) - jnp.log1p(jnp.exp(-jnp.abs(x)))


def _softplus(x):
    return jnp.maximum(x, 0.0) + jnp.log1p(jnp.exp(-jnp.abs(x)))


def _silu(x):
    return x * jax.nn.sigmoid(x)


def _layer_norm(t, g, b):
    mu = jnp.mean(t, axis=-1, keepdims=True)
    tc = t - mu
    var = jnp.mean(tc * tc, axis=-1, keepdims=True)
    return tc * lax.rsqrt(var + LN_EPS) * g + b


def _group_rms(t, g, width):
    outs = []
    for s in range(0, t.shape[-1], width):
        seg = t[:, s:s + width]
        ms = jnp.mean(seg * seg, axis=-1, keepdims=True)
        outs.append(seg * lax.rsqrt(ms + RMS_EPS) * g[:, s:s + width])
    return jnp.concatenate(outs, axis=-1)


def _run(steps):
    try:
        while True:
            next(steps)
    except StopIteration as done:
        return done.value


def _mixer_tile(h, tile, valid_col, valid_row, p, s_gla, s_ssd, xbc_buf, mix_buf, need_out):
    hb = h.astype(BF16)
    w_in = p["w_in"]

    def proj(lo, hi):
        r = _dot(hb, w_in[:, lo:hi])
        if valid_col is not None:
            r = jnp.where(valid_col, r, 0.0)
        return r

    misc = proj(C_MISC, C_END)

    la = _log_sigmoid(_dot(misc.astype(BF16), p["w_a2"][...]) + p["b_a"][...]) * (1.0 / GLA_TAU)
    if valid_col is not None:
        la = jnp.where(valid_col, la, 0.0)
    la_hi, la_lo = _split_hi_lo(la)
    r = _dot(p["uo"][...], jnp.concatenate([la_hi, la_lo], axis=1))
    dec_exp = r[0:tile, 0:GLA_KEY] + r[0:tile, GLA_KEY:]
    tot_rows = r[tile:, 0:GLA_KEY] + r[tile:, GLA_KEY:]
    gla_dec_t = jnp.exp(tot_rows.T)
    kd = proj(C_K, C_V) * jnp.exp(dec_exp)
    v_bf = proj(C_V, C_OG).astype(BF16)

    yield
    misc_t = misc.T
    dt = _softplus(misc_t[MISC_DT:MISC_DT + SSD_HEADS, :] + p["dt_bias"][...])
    if valid_row is not None:
        dt = jnp.where(valid_row, dt, 0.0)
    dta = dt * (-jnp.exp(p["a_log"][...]))
    d_hi, d_lo = _split_hi_lo(dta)
    r2 = _dot(jnp.concatenate([d_hi, d_lo], axis=0), p["uto"][...])
    w = jnp.exp(r2[0:8, 0:tile] + r2[8:16, 0:tile]) * dt
    ssd_dec = jnp.exp(r2[0:8, tile:] + r2[8:16, tile:])
    w_hi, w_lo = _split_hi_lo(w)
    c_hi, c_lo = _split_hi_lo(ssd_dec)
    packed = jnp.concatenate(
        [w_hi.astype(F32), w_lo.astype(F32), c_hi.astype(F32), c_lo.astype(F32),
         jnp.zeros((LANES - 4 * SSD_HEADS, tile), F32)], axis=0)
    expanded = _dot(packed.T.astype(BF16), p["eexp"][...])
    w_exp = expanded[:, 0:SSD_INNER]
    ssd_dec_exp = expanded[:, SSD_INNER:]

    yield
    xbc_buf[CONV_PAD:CONV_PAD + tile, :] = proj(C_XBC, C_MISC)
    acc = p["conv_b"][...] + p["conv_w"][0:1, :] * xbc_buf[pl.ds(CONV_PAD - 3, tile), :]
    for j in range(1, SSD_CONV):
        acc = acc + p["conv_w"][j:j + 1, :] * xbc_buf[pl.ds(CONV_PAD - 3 + j, tile), :]
    xbc_buf[0:CONV_PAD, :] = xbc_buf[tile:tile + CONV_PAD, :]
    xa = _silu(acc)
    xs = xa[:, 0:SSD_INNER]
    bm = xa[:, SSD_INNER:SSD_INNER + SSD_GROUPS * SSD_STATE]
    cm_bf = xa[:, SSD_INNER + SSD_GROUPS * SSD_STATE:].astype(BF16)
    xw_bf = (xs * w_exp).astype(BF16)
    yield

    if need_out:
        q = proj(C_Q, C_K) * (GLA_DK ** -0.5)

    lane = lax.broadcasted_iota(jnp.int32, (1, PAIR), 1)
    for pr in range(tile // PAIR):
        rows = slice(pr * PAIR, (pr + 1) * PAIR)
        kd_t = kd[rows].T
        bm_t = bm[rows].T
        for half in range(2):
            c = 2 * pr + half
            r0 = c * CHUNK
            sel = (lane >= half * CHUNK) & (lane < (half + 1) * CHUNK)
            kd_m = jnp.where(sel, kd_t, 0.0).astype(BF16)
            upd = jnp.concatenate(
                [_dot(kd_m[hd * GLA_DK:(hd + 1) * GLA_DK], v_bf[rows, hd * GLA_DV:(hd + 1) * GLA_DV])
                 for hd in range(GLA_HEADS)], axis=0)
            s_new = gla_dec_t[:, c:c + 1] * s_gla[...] + upd
            s_gla[...] = s_new
            bm_m = jnp.where(sel, bm_t, 0.0).astype(BF16)
            ssd_new = []
            for g in range(SSD_GROUPS):
                gl = slice(g * SSD_GROUP_W, (g + 1) * SSD_GROUP_W)
                gr = slice(g * SSD_STATE, (g + 1) * SSD_STATE)
                u = _dot(bm_m[gr], xw_bf[rows, gl])
                sg = ssd_dec_exp[r0:r0 + 1, gl] * s_ssd[gr, :] + u
                s_ssd[gr, :] = sg
                ssd_new.append(sg)
            if need_out:
                s_bf = s_new.astype(BF16)
                zero = jnp.zeros((GLA_DK, GLA_DV), BF16)
                s_diag = jnp.concatenate(
                    [jnp.concatenate([s_bf[hd * GLA_DK:(hd + 1) * GLA_DK] if col == hd else zero
                                      for col in range(GLA_HEADS)], axis=1) for hd in range(GLA_HEADS)],
                    axis=0)
                mix_buf[r0:r0 + CHUNK, 0:GLA_VAL] = _dot(q[r0:r0 + CHUNK].astype(BF16), s_diag)
                for g in range(SSD_GROUPS):
                    y = _dot(cm_bf[r0:r0 + CHUNK, g * SSD_STATE:(g + 1) * SSD_STATE],
                             ssd_new[g].astype(BF16))
                    mix_buf[r0:r0 + CHUNK, GLA_VAL + g * SSD_GROUP_W:GLA_VAL + (g + 1) * SSD_GROUP_W] = y
        yield

    if not need_out:
        return None
    o_gla = _group_rms(mix_buf[:, 0:GLA_VAL], p["gla_norm_g"][...], GLA_DV) * _silu(proj(C_OG, C_Z))
    yield
    y = (mix_buf[:, GLA_VAL:] + xs * p["ssd_d"][...]) * _silu(proj(C_Z, C_XBC))
    y = _group_rms(y, p["ssd_norm_g"][...], SSD_GROUP_W)
    return o_gla, y


_MIX_PARAMS = ("w_in", "w_a2", "b_a", "conv_w", "conv_b", "dt_bias", "a_log", "uo", "uto", "eexp")
_OUT_PARAMS = ("gla_norm_g", "ssd_d", "ssd_norm_g")


def _meta_kernel(x_ref, ln_g, ln_b, *rest):
    np_ = len(_MIX_PARAMS)
    p = dict(zip(_MIX_PARAMS, rest[:np_]))
    s_gla_out, s_ssd_out, tail_out, xbc_buf = rest[np_:]
    tile = x_ref.shape[0]
    s_gla_out[...] = jnp.zeros_like(s_gla_out)
    s_ssd_out[...] = jnp.zeros_like(s_ssd_out)
    xbc_buf[0:CONV_PAD, :] = jnp.zeros((CONV_PAD, xbc_buf.shape[1]), F32)
    first_valid = tile - N_META
    valid_col = lax.broadcasted_iota(jnp.int32, (tile, 1), 0) >= first_valid
    valid_row = lax.broadcasted_iota(jnp.int32, (1, tile), 1) >= first_valid
    h = _layer_norm(x_ref[...], ln_g[...], ln_b[...])
    _run(_mixer_tile(h, tile, valid_col, valid_row, p, s_gla_out, s_ssd_out, xbc_buf, None, False))
    tail_out[...] = xbc_buf[0:CONV_PAD, :]


def _mixer_kernel(alpha, n_exp, x_ref, ln_g, ln_b, *rest):
    np_ = len(_MIX_PARAMS)
    p = dict(zip(_MIX_PARAMS, rest[:np_]))
    rest = rest[np_:]
    p.update(zip(_OUT_PARAMS, rest[:3]))
    (w_out, ln1_g, ln1_b, rw, rb, su, s_gla0, s_ssd0, tail0,
     h1_out, topi_out, gate_out, rank_out, cnt_out, base_out, tcnt_out) = rest[3:19]
    n_sub, tile = x_ref.shape[0], x_ref.shape[1]
    scratch = rest[19:]
    s_gla, s_ssd, xbc_buf, mix_buf = (scratch[k * n_sub:(k + 1) * n_sub] for k in range(4))
    run_cnt = scratch[4 * n_sub]
    b, j = pl.program_id(0), pl.program_id(1)

    @pl.when(j == 0)
    def _():
        for s in range(n_sub):
            s_gla[s][...] = s_gla0[...]
            s_ssd[s][...] = s_ssd0[...]
            xbc_buf[s][0:CONV_PAD, :] = tail0[...]

    @pl.when((b == 0) & (j == 0))
    def _():
        run_cnt[...] = jnp.zeros_like(run_cnt)

    count = [run_cnt[...]]
    pad_rows = topi_out.shape[1] - TOP_K

    def sequence(s):
        h = _layer_norm(x_ref[s], ln_g[...], ln_b[...])
        o_gla, y = yield from _mixer_tile(h, tile, None, None, p, s_gla[s], s_ssd[s], xbc_buf[s], mix_buf[s], True)
        yield
        mix = _dot(o_gla.astype(BF16), w_out[0:GLA_VAL, :]) + _dot(y.astype(BF16), w_out[GLA_VAL:, :])
        h1 = _layer_norm(alpha * h + mix, ln1_g[...], ln1_b[...])
        _store_rows(h1_out.at[s], h1)
        yield

        h_hi, h_lo = _split_hi_lo(h1)
        p_hi = _dot(h_hi, rw[...])
        logits = p_hi + (pltpu.roll(p_hi, LANES - n_exp, 1) + _dot(h_lo, rw[...]))
        lt = logits.T[0:n_exp, :] + rb[...]
        e_iota = lax.broadcasted_iota(jnp.int32, (n_exp, tile), 0).astype(F32)
        work = lt
        vals, hots = [], []
        for k in range(TOP_K):
            m = jnp.max(work, axis=0, keepdims=True)
            idx = jnp.min(jnp.where(work == m, e_iota, float(n_exp)), axis=0, keepdims=True)
            hot = e_iota == idx
            work = jnp.where(hot, -jnp.inf, work)
            vals.append(m)
            hots.append(hot)
            topi_out[s, k:k + 1, :] = idx.astype(jnp.int32)
        exps = [jnp.exp(vk - vals[0]) for vk in vals]
        denom = exps[0] + exps[1] + exps[2] + exps[3]
        for k in range(TOP_K):
            gate_out[s, k:k + 1, :] = exps[k] / denom
        member = (hots[0] | hots[1] | hots[2] | hots[3]).astype(F32).astype(BF16)
        yield
        before = _dot(member, su[...])
        for k in range(TOP_K):
            rk = jnp.sum(jnp.where(hots[k], before, 0.0), axis=0, keepdims=True)
            rank_out[s, k:k + 1, :] = rk.astype(jnp.int32)
        topi_out[s, TOP_K:, :] = jnp.zeros((pad_rows, tile), jnp.int32)
        gate_out[s, TOP_K:, :] = jnp.zeros((pad_rows, tile), F32)
        rank_out[s, TOP_K:, :] = jnp.zeros((pad_rows, tile), jnp.int32)
        tile_count = _dot(member, jnp.ones((tile, LANES), BF16))
        base_out[s, 0] = count[0]
        tcnt_out[s, 0] = tile_count
        count[0] = count[0] + tile_count

    pending = [sequence(s) for s in range(n_sub)]
    for lead in range(1, n_sub):
        for _ in range(SEQ_SKEW):
            for steps in pending[:lead]:
                next(steps)
    while pending:
        for steps in list(pending):
            try:
                next(steps)
            except StopIteration:
                pending.remove(steps)
    run_cnt[...] = count[0]
    cnt_out[...] = count[0]


def _clamped_swiglu(a, u):
    a = jnp.minimum(a, SWIGLU_LIMIT)
    u = jnp.clip(u, -SWIGLU_LIMIT, SWIGLU_LIMIT)
    return a * jax.nn.sigmoid(SWIGLU_ALPHA * a) * (u + 1.0)


SUB = 8


def _load_rows(ref, n_rows):
    return jnp.concatenate([ref[pl.ds(j, n_rows, stride=SUB), :] for j in range(SUB)], axis=1)


def _store_rows(ref, val):
    for j in range(SUB):
        ref[pl.ds(j, val.shape[0], stride=SUB), :] = val[:, j * LANES:(j + 1) * LANES]


def _tile_rows(row):
    return pl.ds(pl.multiple_of(row * SUB, SUB), SUB)


RUN_ROWS = 8
PIECE_GROUP = 4
REST_SIZES = (4, 2, 1)


def _run_rows(row):
    return pl.ds(pl.multiple_of(row * SUB, SUB), RUN_ROWS * SUB)


def _stage_tile(addr):
    return pl.ds(pl.multiple_of(addr, SUB), SUB)


def _for_pieces(pieces, fn):
    def body(g, _):
        for u in range(PIECE_GROUP):
            p = g * PIECE_GROUP + u
            fn(p * RUN_ROWS, pieces[0, 0, 1 + p])
        return 0
    lax.fori_loop(0, pieces[0, 0, 0], body, 0)


def _dispatch_kernel(fill_lo, fill_hi, pos, meta_cur, meta_prv, tail_cur, tail_prv, h_hbm, xs_hbm,
                     tbuf, stage, zrow, lsem, rsem, zsem):
    i = pl.program_id(0)
    n = pl.num_programs(0)
    tile = pos.shape[-1] // TOP_K
    slot = i % 2
    n_exp = fill_lo.shape[0]
    stage_rows = stage.shape[0] // (2 * SUB)
    rest_base = stage_rows - n_exp * RUN_ROWS

    def tile_load(j, s):
        return pltpu.make_async_copy(h_hbm.at[pl.ds(j * tile * SUB, tile * SUB), :], tbuf.at[s], lsem.at[s])

    def rows_copy(s, stage_row, sorted_row, n_rows):
        src = stage.at[pl.ds(pl.multiple_of((s * stage_rows + stage_row) * SUB, SUB), n_rows * SUB), :]
        dst = xs_hbm.at[pl.ds(pl.multiple_of(sorted_row * SUB, SUB), n_rows * SUB), :]
        return pltpu.make_async_copy(src, dst, rsem.at[s])

    def for_rest(rest, s, wait):
        for c, size in enumerate(REST_SIZES):
            first = 8 + 2 * n_exp * c

            def body(q, _, first=first, size=size):
                if wait:
                    rows_copy(s, 0, 0, size).wait()
                else:
                    rows_copy(s, rest[0, 0, first + 2 * q + 1], rest[0, 0, first + 2 * q], size).start()
                return 0
            lax.fori_loop(0, rest[0, 0, c], body, 0)

    def zero_copy(dst_row):
        return pltpu.make_async_copy(zrow, xs_hbm.at[_tile_rows(dst_row), :], zsem.at[0])

    @pl.when(i == 0)
    def _():
        tile_load(0, 0).start()
        zrow[...] = jnp.zeros_like(zrow)
        stage[...] = jnp.zeros_like(stage)
        for wait in (False, True):
            def per_expert(e, _, wait=wait):
                def body(row, _):
                    if wait:
                        zero_copy(0).wait()
                    else:
                        zero_copy(row).start()
                    return 0
                lax.fori_loop(fill_lo[e], fill_hi[e], body, 0)
                return 0
            lax.fori_loop(0, n_exp, per_expert, 0)
        tail_lo = fill_hi[n_exp - 1]
        tail_pieces = (xs_hbm.shape[0] // SUB - tail_lo) // RUN_ROWS
        for wait in (False, True):
            def tail(c, _, wait=wait):
                cp = pltpu.make_async_copy(stage.at[_run_rows(0), :], xs_hbm.at[_run_rows(tail_lo + c * RUN_ROWS), :],
                                           zsem.at[0])
                if wait:
                    cp.wait()
                else:
                    cp.start()
                return 0
            lax.fori_loop(0, tail_pieces, tail, 0)

    @pl.when(i + 1 < n)
    def _():
        tile_load(i + 1, 1 - slot).start()

    tile_load(i, slot).wait()

    def pack(t, _):
        row = tbuf[slot, _tile_rows(t), :]
        for k in range(TOP_K):
            stage[_stage_tile(pos[0, 0, TOP_K * t + k]), :] = row
        return 0
    lax.fori_loop(0, tile, pack, 0, unroll=8)

    _for_pieces(meta_cur, lambda off, row: rows_copy(slot, off, row, RUN_ROWS).start())
    for_rest(tail_cur, slot, False)

    def wait_tile(rest, s):
        rows_copy(s, 0, 0, TOP_K * tile).wait()

        def body(q, _):
            rows_copy(s, 0, 0, RUN_ROWS).wait()
            return 0
        lax.fori_loop(0, rest[0, 0, len(REST_SIZES)], body, 0)

    @pl.when(i > 0)
    def _():
        wait_tile(tail_prv, 1 - slot)

    @pl.when(i == n - 1)
    def _():
        wait_tile(tail_cur, slot)


def _experts_kernel(blk_e, n_act, x_ref, wg, wu, bg, bu, wd, bd, y_out, wg_bf, wu_bf, wd_bf):
    i = pl.program_id(0)
    active = i < n_act[0]
    new_expert = (i == 0) | (blk_e[i] != blk_e[jnp.maximum(i - 1, 0)])

    @pl.when(active & new_expert)
    def _():
        wg_bf[...] = wg[...].astype(BF16)
        wu_bf[...] = wu[...].astype(BF16)
        wd_bf[...] = wd[...].astype(BF16)

    @pl.when(active)
    def _():
        xb = _load_rows(x_ref, x_ref.shape[0] // SUB).astype(BF16)
        hid = _clamped_swiglu(_dot(xb, wg_bf[...]) + bg[...], _dot(xb, wu_bf[...]) + bu[...])
        _store_rows(y_out, _dot(hid.astype(BF16), wd_bf[...]) + bd[...])

    @pl.when(i >= n_act[0])
    def _():
        y_out[...] = jnp.zeros_like(y_out)


def _combine_kernel(alpha, n_exp, pos, gate, meta_cur, meta_nxt, tail_cur, tail_nxt, h1, ln_g, ln_b, y_hbm, out,
                    stage, ff_rows, sem):
    i = pl.program_id(0)
    n = pl.num_programs(0)
    tile = out.shape[0]
    slot = i % 2
    stage_rows = stage.shape[0] // (2 * SUB)
    rest_base = stage_rows - n_exp * RUN_ROWS

    def run_copy(s, stage_row, sorted_row):
        return pltpu.make_async_copy(y_hbm.at[_run_rows(sorted_row), :],
                                     stage.at[_run_rows(s * stage_rows + stage_row), :], sem.at[s])

    def fetch(meta, tails, s, wait):
        act = (lambda cp: cp.wait()) if wait else (lambda cp: cp.start())
        _for_pieces(meta, lambda off, row: act(run_copy(s, 0 if wait else off, 0 if wait else row)))

        if wait:
            pltpu.make_async_copy(y_hbm.at[pl.ds(0, n_exp * RUN_ROWS * SUB), :],
                                  stage.at[pl.ds(0, n_exp * RUN_ROWS * SUB), :], sem.at[s]).wait()
            return

        def rest(g, _):
            for u in range(PIECE_GROUP):
                e = g * PIECE_GROUP + u
                run_copy(s, rest_base + e * RUN_ROWS, tails[0, 0, e]).start()
            return 0
        lax.fori_loop(0, n_exp // PIECE_GROUP, rest, 0)

    @pl.when(i == 0)
    def _():
        fetch(meta_cur, tail_cur, 0, False)

    @pl.when(i + 1 < n)
    def _():
        fetch(meta_nxt, tail_nxt, 1 - slot, False)

    fetch(meta_cur, tail_cur, slot, True)

    def gather(t, _):
        acc = gate[0, 0, TOP_K * t] * stage[_stage_tile(pos[0, 0, TOP_K * t]), :]
        for k in range(1, TOP_K):
            acc = acc + gate[0, 0, TOP_K * t + k] * stage[_stage_tile(pos[0, 0, TOP_K * t + k]), :]
        ff_rows[_tile_rows(t), :] = acc
        return 0
    lax.fori_loop(0, tile, gather, 0, unroll=8)

    out[...] = _layer_norm(alpha * _load_rows(h1, tile) + _load_rows(ff_rows, tile), ln_g[...], ln_b[...])


def _block_constants(tile):
    t = np.arange(tile)
    same = (t[:, None] // CHUNK) == (t[None, :] // CHUNK)
    later = same & (t[None, :] > t[:, None])
    chunk_rows = np.zeros((LANES, tile), np.float32)
    chunk_rows[t // CHUNK, t] = 1.0
    uo = np.concatenate([later.astype(np.float32), chunk_rows], axis=0)
    uto = np.concatenate([later.T.astype(np.float32), same.astype(np.float32)], axis=1)
    return jnp.asarray(uo, BF16), jnp.asarray(uto, BF16)


def _expand_constant():
    e = np.zeros((LANES, 2 * SSD_INNER), np.float32)
    for part in range(4):
        for hd in range(SSD_HEADS):
            base = (part // 2) * SSD_INNER + hd * SSD_HEADDIM
            e[part * SSD_HEADS + hd, base:base + SSD_HEADDIM] = 1.0
    return jnp.asarray(e, BF16)


def _full(shape):
    return pl.BlockSpec(shape, lambda *_: (0,) * len(shape))


def kernel(x, meta_tokens, ln_in_g, ln_in_b, w_in, gla_w_a2, gla_b_a, gla_norm_g, ssd_conv_w, ssd_conv_b,
           ssd_dt_bias, ssd_a_log, ssd_d, ssd_norm_g, w_out, ln1_g, ln1_b, router_w, router_b, moe_w_gate,
           moe_w_up, moe_b_gate, moe_b_up, moe_w_down, moe_b_down, ln2_g, ln2_b):
    batch, seq, d = x.shape
    depth = w_in.shape[0]
    assert depth == 1, "single-layer stack"
    n_exp = router_w.shape[-1]
    d_ff = moe_w_gate.shape[-1]
    alpha = (2.0 * depth) ** 0.25
    tile = MIXER_TILE
    assert seq % tile == 0 and d == 1024
    n_tok = batch * seq
    row = lambda a: a.reshape(1, -1).astype(F32)

    wi = w_in[0]
    o_a1 = 1536
    o_z = o_a1 + GLA_RANK
    o_xbc = o_z + SSD_INNER
    o_dt = o_xbc + SSD_CONV_CH
    misc_w = jnp.zeros((d, LANES), F32)
    misc_w = misc_w.at[:, MISC_A1:MISC_A1 + GLA_RANK].set(wi[:, o_a1:o_z])
    misc_w = misc_w.at[:, MISC_DT:MISC_DT + SSD_HEADS].set(wi[:, o_dt:o_dt + SSD_HEADS])
    w_in_r = jnp.concatenate([wi[:, 0:o_a1], wi[:, o_z:o_xbc], wi[:, o_xbc:o_dt], misc_w], axis=1).astype(BF16)
    w_a2 = jnp.zeros((LANES, GLA_KEY), F32).at[MISC_A1:MISC_A1 + GLA_RANK].set(gla_w_a2[0]).astype(BF16)
    mix_params = dict(
        w_in=w_in_r, w_a2=w_a2, b_a=row(gla_b_a[0]), conv_w=ssd_conv_w[0].astype(F32),
        conv_b=row(ssd_conv_b[0]), dt_bias=ssd_dt_bias[0].reshape(-1, 1).astype(F32),
        a_log=ssd_a_log[0].reshape(-1, 1).astype(F32), eexp=_expand_constant())
    out_params = dict(gla_norm_g=row(gla_norm_g[0]), ssd_d=row(jnp.repeat(ssd_d[0], SSD_HEADDIM)),
                      ssd_norm_g=row(ssd_norm_g[0]))

    def mix_args(t):
        uo, uto = _block_constants(t)
        vals = dict(mix_params, uo=uo, uto=uto)
        return [vals[k] for k in _MIX_PARAMS]

    m_tile = PAIR
    x_meta = jnp.concatenate([jnp.zeros((m_tile - N_META, d), F32), meta_tokens.astype(F32)], axis=0)
    meta_in = [x_meta, row(ln_in_g), row(ln_in_b)] + mix_args(m_tile)
    s_gla0, s_ssd0, tail0 = pl.pallas_call(
        _meta_kernel,
        out_shape=(jax.ShapeDtypeStruct((GLA_KEY, GLA_DV), F32),
                   jax.ShapeDtypeStruct((SSD_GROUPS * SSD_STATE, SSD_GROUP_W), F32),
                   jax.ShapeDtypeStruct((CONV_PAD, SSD_CONV_CH), F32)),
        scratch_shapes=[pltpu.VMEM((CONV_PAD + m_tile, SSD_CONV_CH), F32)],
        compiler_params=pltpu.CompilerParams(vmem_limit_bytes=VMEM_LIMIT),
        name="meta_state",
    )(*meta_in)

    n_j = seq // tile
    rw_hi, rw_lo = _split_hi_lo(router_w[0].astype(F32))
    rw = jnp.zeros((d, LANES), BF16).at[:, 0:n_exp].set(rw_hi).at[:, n_exp:2 * n_exp].set(rw_lo)
    rb = router_b[0].reshape(n_exp, 1).astype(F32)
    su = jnp.asarray(np.triu(np.ones((tile, tile), np.float32), 1), BF16)
    args = ([x, row(ln_in_g), row(ln_in_b)] + mix_args(tile) + [out_params[k] for k in _OUT_PARAMS]
            + [w_out[0].astype(BF16), row(ln1_g[0]), row(ln1_b[0]), rw, rb, su, s_gla0, s_ssd0, tail0])
    n_sub = MIXER_SEQS
    assert batch % n_sub == 0
    in_specs = [pl.BlockSpec((n_sub, tile, d), lambda b, j: (b, j, 0))] + [_full(a.shape) for a in args[1:]]
    tok_blk = pl.BlockSpec((n_sub, 8, tile), lambda b, j: (b, 0, j))
    tile_tab = pl.BlockSpec((n_sub, 1, n_exp, LANES), lambda b, j: (b, j, 0, 0))
    h1, topi, gates, rank, cnt, base, tile_cnt = pl.pallas_call(
        functools.partial(_mixer_kernel, alpha, n_exp),
        grid=(batch // n_sub, n_j),
        in_specs=in_specs,
        out_specs=(pl.BlockSpec((n_sub, tile * SUB, LANES), lambda b, j: (b, j, 0)),
                   tok_blk, tok_blk, tok_blk, _full((n_exp, LANES)), tile_tab, tile_tab),
        out_shape=(jax.ShapeDtypeStruct((batch, seq * SUB, LANES), F32),
                   jax.ShapeDtypeStruct((batch, 8, seq), jnp.int32),
                   jax.ShapeDtypeStruct((batch, 8, seq), F32),
                   jax.ShapeDtypeStruct((batch, 8, seq), jnp.int32),
                   jax.ShapeDtypeStruct((n_exp, LANES), F32),
                   jax.ShapeDtypeStruct((batch, n_j, n_exp, LANES), F32),
                   jax.ShapeDtypeStruct((batch, n_j, n_exp, LANES), F32)),
        scratch_shapes=([pltpu.VMEM((GLA_KEY, GLA_DV), F32)] * n_sub
                        + [pltpu.VMEM((SSD_GROUPS * SSD_STATE, SSD_GROUP_W), F32)] * n_sub
                        + [pltpu.VMEM((CONV_PAD + tile, SSD_CONV_CH), F32)] * n_sub
                        + [pltpu.VMEM((tile, d), F32)] * n_sub
                        + [pltpu.VMEM((n_exp, LANES), F32)]),
        compiler_params=pltpu.CompilerParams(dimension_semantics=("arbitrary", "arbitrary"),
                                             vmem_limit_bytes=VMEM_LIMIT),
        name="mixer",
    )(*args)
    h1 = h1.reshape(n_tok * SUB, LANES)
    per_tok = lambda a: a[:, :TOP_K].transpose(1, 0, 2).reshape(TOP_K, n_tok)
    top_e, gates, rank = per_tok(topi), per_tok(gates), per_tok(rank)

    counts = cnt[:, 0].astype(jnp.int32)
    padded = (counts + RUN_ROWS + MOE_BLOCK - 1) // MOE_BLOCK * MOE_BLOCK
    pad_end = jnp.cumsum(padded)
    pad_start = pad_end - padded
    n_blocks = -(-(n_tok * TOP_K) // MOE_BLOCK) + n_exp + 1
    n_rows = n_blocks * MOE_BLOCK
    blk_lo = jnp.arange(n_blocks, dtype=jnp.int32) * MOE_BLOCK
    block_e = jnp.minimum(jnp.sum((blk_lo[:, None] >= pad_end[None, :]).astype(jnp.int32), axis=1), n_exp - 1)
    n_act = (pad_end[-1] // MOE_BLOCK).astype(jnp.int32).reshape(1)
    c_tile = tile
    n_ct = n_tok // c_tile
    run_base = base[..., 0].astype(jnp.int32).reshape(n_ct, n_exp)
    run_len = tile_cnt[..., 0].astype(jnp.int32).reshape(n_ct, n_exp)
    src_row = pad_start[None, :] + run_base
    run_pieces = run_len // RUN_ROWS
    full_rows = run_pieces * RUN_ROWS
    piece_end = jnp.cumsum(run_pieces, axis=1)
    first_piece = piece_end - run_pieces
    stage_row = first_piece * RUN_ROWS
    max_pieces = -(-(TOP_K * c_tile // RUN_ROWS) // PIECE_GROUP) * PIECE_GROUP
    p_ids = jnp.arange(max_pieces, dtype=jnp.int32)
    in_run = (p_ids[None, :, None] >= first_piece[:, None, :]) & (p_ids[None, :, None] < piece_end[:, None, :])
    piece_row = jnp.sum(jnp.where(in_run, (src_row - stage_row)[:, None, :], 0), axis=2)
    piece_row = piece_row + p_ids[None, :] * RUN_ROWS
    n_pieces = piece_end[:, -1:]
    tile_parity = (jnp.arange(n_ct, dtype=jnp.int32) % 2)[:, None]
    spare_row = (n_blocks - 1) * MOE_BLOCK + (tile_parity * PIECE_GROUP + p_ids[None, :] - n_pieces) * RUN_ROWS
    piece_row = jnp.where(p_ids[None, :] < n_pieces, piece_row, jnp.clip(spare_row, 0, n_rows - RUN_ROWS))
    n_groups = (n_pieces + PIECE_GROUP - 1) // PIECE_GROUP
    meta_w = -(-(max_pieces + 1) // LANES) * LANES
    meta = jnp.concatenate([n_groups, piece_row,
                            jnp.zeros((n_ct, meta_w - 1 - max_pieces), jnp.int32)], axis=1).reshape(n_ct, 1, meta_w)
    rest_base = max_pieces * RUN_ROWS
    stage_rows = rest_base + n_exp * RUN_ROWS
    tails = jnp.concatenate([src_row + full_rows, run_len - full_rows,
                             jnp.zeros((n_ct, LANES - 2 * n_exp), jnp.int32)], axis=1).reshape(n_ct, 1, LANES)
    rest_len = run_len - full_rows
    e_row = jnp.arange(n_exp, dtype=jnp.int32)
    rest_cnt, rest_lists = [], []
    for size in REST_SIZES:
        has = (rest_len & size) != 0
        slot_in_list = jnp.cumsum(has, axis=1) - has
        before = rest_len & (7 & ~(2 * size - 1))
        sel = has[:, None, :] & (slot_in_list[:, None, :] == e_row[None, :, None])
        pick_run = lambda v: jnp.sum(jnp.where(sel, v[:, None, :], 0), axis=2)
        rest_lists.append(jnp.stack([pick_run(src_row + full_rows + before),
                                     pick_run(rest_base + e_row[None, :] * RUN_ROWS + before)], axis=2)
                          .reshape(n_ct, 2 * n_exp))
        rest_cnt.append(jnp.sum(has, axis=1, keepdims=True).astype(jnp.int32))
    rest_w = -(-(8 + 2 * n_exp * len(REST_SIZES)) // LANES) * LANES
    rest_cnt.append(n_groups * PIECE_GROUP - n_pieces)
    rest = jnp.concatenate(rest_cnt + [jnp.zeros((n_ct, 8 - len(rest_cnt)), jnp.int32)] + rest_lists
                           + [jnp.zeros((n_ct, rest_w - 8 - 2 * n_exp * len(REST_SIZES)), jnp.int32)],
                           axis=1).reshape(n_ct, 1, rest_w)
    e_ids = e_row[:, None, None]
    hot = top_e[None] == e_ids
    pack = 2 * TOP_K * c_tile
    packed = jnp.repeat((stage_row * pack + full_rows).T, c_tile, axis=1)
    picked = jnp.sum(jnp.where(hot, packed[:, None, :], 0), axis=0)
    run_stage, run_full = picked // pack, picked % pack
    pos = jnp.where(rank < run_full, run_stage + rank, rest_base + top_e * RUN_ROWS + rank - run_full)
    slot_tok = (jnp.arange(n_tok, dtype=jnp.int32) // c_tile) % 2
    pos_addr = (pos + slot_tok[None, :] * stage_rows) * SUB
    per_tile = lambda a: a.reshape(TOP_K, n_ct, c_tile).transpose(1, 2, 0).reshape(n_ct, 1, TOP_K * c_tile)
    pos3, gate3 = per_tile(pos_addr), per_tile(gates)
    smem_blk = lambda width, f: pl.BlockSpec((1, 1, width), f, memory_space=pltpu.SMEM)

    x_sorted = pl.pallas_call(
        _dispatch_kernel,
        grid_spec=pltpu.PrefetchScalarGridSpec(
            num_scalar_prefetch=2,
            grid=(n_ct,),
            in_specs=[smem_blk(TOP_K * c_tile, lambda i, lo, hi: (i, 0, 0)),
                      smem_blk(meta_w, lambda i, lo, hi: (i, 0, 0)),
                      smem_blk(meta_w, lambda i, lo, hi: (jnp.maximum(i - 1, 0), 0, 0)),
                      smem_blk(rest_w, lambda i, lo, hi: (i, 0, 0)),
                      smem_blk(rest_w, lambda i, lo, hi: (jnp.maximum(i - 1, 0), 0, 0)),
                      pl.BlockSpec(memory_space=pl.ANY)],
            out_specs=pl.BlockSpec(memory_space=pl.ANY),
            scratch_shapes=[pltpu.VMEM((2, c_tile * SUB, LANES), F32),
                            pltpu.VMEM((2 * stage_rows * SUB, LANES), F32), pltpu.VMEM((SUB, LANES), F32),
                            pltpu.SemaphoreType.DMA((2,)), pltpu.SemaphoreType.DMA((2,)),
                            pltpu.SemaphoreType.DMA((1,))]),
        out_shape=jax.ShapeDtypeStruct((n_rows * SUB, LANES), F32),
        compiler_params=pltpu.CompilerParams(dimension_semantics=("arbitrary",), vmem_limit_bytes=VMEM_LIMIT),
        name="moe_dispatch",
    )(pad_start + counts, pad_end, pos3, meta, meta, rest, rest, h1)

    last_act = lambda i, na: jnp.minimum(i, na[0] - 1)
    e_mat = lambda shape: pl.BlockSpec((None,) + shape, lambda i, be, na: (be[i], 0, 0))
    y_sorted = pl.pallas_call(
        _experts_kernel,
        grid_spec=pltpu.PrefetchScalarGridSpec(
            num_scalar_prefetch=2,
            grid=(n_blocks,),
            in_specs=[pl.BlockSpec((MOE_BLOCK * SUB, LANES), lambda i, be, na: (last_act(i, na), 0)),
                      e_mat((d, d_ff)), e_mat((d, d_ff)), e_mat((1, d_ff)), e_mat((1, d_ff)),
                      e_mat((d_ff, d)), e_mat((1, d))],
            out_specs=pl.BlockSpec((MOE_BLOCK * SUB, LANES), lambda i, be, na: (i, 0)),
            scratch_shapes=[pltpu.VMEM((d, d_ff), BF16), pltpu.VMEM((d, d_ff), BF16),
                            pltpu.VMEM((d_ff, d), BF16)]),
        out_shape=jax.ShapeDtypeStruct((n_rows * SUB, LANES), F32),
        compiler_params=pltpu.CompilerParams(dimension_semantics=("arbitrary",), vmem_limit_bytes=VMEM_LIMIT),
        name="moe_experts",
    )(block_e, n_act, x_sorted,
      moe_w_gate[0].astype(F32), moe_w_up[0].astype(F32),
      moe_b_gate[0].reshape(n_exp, 1, d_ff).astype(F32), moe_b_up[0].reshape(n_exp, 1, d_ff).astype(F32),
      moe_w_down[0].astype(F32), moe_b_down[0].reshape(n_exp, 1, d).astype(F32))

    out = pl.pallas_call(
        functools.partial(_combine_kernel, alpha, n_exp),
        grid=(n_ct,),
        in_specs=[smem_blk(TOP_K * c_tile, lambda i: (i, 0, 0)),
                  smem_blk(TOP_K * c_tile, lambda i: (i, 0, 0)),
                  smem_blk(meta_w, lambda i: (i, 0, 0)),
                  smem_blk(meta_w, lambda i: (jnp.minimum(i + 1, n_ct - 1), 0, 0)),
                  smem_blk(LANES, lambda i: (i, 0, 0)),
                  smem_blk(LANES, lambda i: (jnp.minimum(i + 1, n_ct - 1), 0, 0)),
                  pl.BlockSpec((c_tile * SUB, LANES), lambda i: (i, 0)),
                  _full((1, d)), _full((1, d)),
                  pl.BlockSpec(memory_space=pl.ANY)],
        out_specs=pl.BlockSpec((c_tile, d), lambda i: (i, 0)),
        out_shape=jax.ShapeDtypeStruct((n_tok, d), F32),
        scratch_shapes=[pltpu.VMEM((2 * stage_rows * SUB, LANES), F32), pltpu.VMEM((c_tile * SUB, LANES), F32),
                        pltpu.SemaphoreType.DMA((2,))],
        compiler_params=pltpu.CompilerParams(dimension_semantics=("arbitrary",), vmem_limit_bytes=VMEM_LIMIT),
        name="moe_combine",
    )(pos3, gate3, meta, meta, tails, tails, h1, row(ln2_g[0]), row(ln2_b[0]), y_sorted)
    return out.reshape(batch, seq, d).astype(x.dtype)
```

```python
import functools

import numpy as np
import jax
import jax.numpy as jnp
from jax import lax
from jax.experimental import pallas as pl
from jax.experimental.pallas import tpu as pltpu

F32 = jnp.float32
BF16 = jnp.bfloat16

CHUNK = 64
N_META = 16
PAIR = 2 * CHUNK
GLA_HEADS = 4
GLA_DK = 64
GLA_DV = 128
GLA_KEY = GLA_HEADS * GLA_DK
GLA_VAL = GLA_HEADS * GLA_DV
GLA_RANK = 16
GLA_TAU = 16.0
SSD_INNER = 512
SSD_HEADS = 8
SSD_HEADDIM = 64
SSD_GROUPS = 2
SSD_STATE = 128
SSD_GROUP_W = SSD_INNER // SSD_GROUPS
SSD_CONV = 4
SSD_CONV_CH = SSD_INNER + 2 * SSD_GROUPS * SSD_STATE
TOP_K = 4
SWIGLU_LIMIT = 7.0
SWIGLU_ALPHA = 1.702
MOE_BLOCK = 768
LN_EPS = 1e-5
RMS_EPS = 1e-6
LANES = 128

C_Q, C_K, C_V, C_OG, C_Z, C_XBC, C_MISC, C_END = 0, 256, 512, 1024, 1536, 2048, 3072, 3200
MISC_A1 = 0
MISC_DT = 16
CONV_PAD = 8

MIXER_TILE = 256
MIXER_SEQS = 2
SEQ_SKEW = 5
VMEM_LIMIT = 56 * 1024 * 1024


def _dot(a, b):
    return jnp.dot(a, b, preferred_element_type=F32)


def _split_hi_lo(x):
    hi = x.astype(BF16)
    lo = (x - hi.astype(F32)).astype(BF16)
    return hi, lo


def _log_sigmoid(x):
    return jnp.minimum(x, 0.0) - jnp.log1p(jnp.exp(-jnp.abs(x)))


def _softplus(x):
    return jnp.maximum(x, 0.0) + jnp.log1p(jnp.exp(-jnp.abs(x)))


def _silu(x):
    return x * jax.nn.sigmoid(x)


def _layer_norm(t, g, b):
    mu = jnp.mean(t, axis=-1, keepdims=True)
    tc = t - mu
    var = jnp.mean(tc * tc, axis=-1, keepdims=True)
    return tc * lax.rsqrt(var + LN_EPS) * g + b


def _group_rms(t, g, width):
    outs = []
    for s in range(0, t.shape[-1], width):
        seg = t[:, s:s + width]
        ms = jnp.mean(seg * seg, axis=-1, keepdims=True)
        outs.append(seg * lax.rsqrt(ms + RMS_EPS) * g[:, s:s + width])
    return jnp.concatenate(outs, axis=-1)


def _run(steps):
    try:
        while True:
            next(steps)
    except StopIteration as done:
        return done.value


def _mixer_tile(h, tile, valid_col, valid_row, p, s_gla, s_ssd, xbc_buf, mix_buf, need_out):
    hb = h.astype(BF16)
    w_in = p["w_in"]

    def proj(lo, hi):
        r = _dot(hb, w_in[:, lo:hi])
        if valid_col is not None:
            r = jnp.where(valid_col, r, 0.0)
        return r

    misc = proj(C_MISC, C_END)

    la = _log_sigmoid(_dot(misc.astype(BF16), p["w_a2"][...]) + p["b_a"][...]) * (1.0 / GLA_TAU)
    if valid_col is not None:
        la = jnp.where(valid_col, la, 0.0)
    la_hi, la_lo = _split_hi_lo(la)
    r = _dot(p["uo"][...], jnp.concatenate([la_hi, la_lo], axis=1))
    dec_exp = r[0:tile, 0:GLA_KEY] + r[0:tile, GLA_KEY:]
    tot_rows = r[tile:, 0:GLA_KEY] + r[tile:, GLA_KEY:]
    gla_dec_t = jnp.exp(tot_rows.T)
    kd = proj(C_K, C_V) * jnp.exp(dec_exp)
    v_bf = proj(C_V, C_OG).astype(BF16)

    yield
    misc_t = misc.T
    dt = _softplus(misc_t[MISC_DT:MISC_DT + SSD_HEADS, :] + p["dt_bias"][...])
    if valid_row is not None:
        dt = jnp.where(valid_row, dt, 0.0)
    dta = dt * (-jnp.exp(p["a_log"][...]))
    d_hi, d_lo = _split_hi_lo(dta)
    r2 = _dot(jnp.concatenate([d_hi, d_lo], axis=0), p["uto"][...])
    w = jnp.exp(r2[0:8, 0:tile] + r2[8:16, 0:tile]) * dt
    ssd_dec = jnp.exp(r2[0:8, tile:] + r2[8:16, tile:])
    w_hi, w_lo = _split_hi_lo(w)
    c_hi, c_lo = _split_hi_lo(ssd_dec)
    packed = jnp.concatenate(
        [w_hi.astype(F32), w_lo.astype(F32), c_hi.astype(F32), c_lo.astype(F32),
         jnp.zeros((LANES - 4 * SSD_HEADS, tile), F32)], axis=0)
    expanded = _dot(packed.T.astype(BF16), p["eexp"][...])
    w_exp = expanded[:, 0:SSD_INNER]
    ssd_dec_exp = expanded[:, SSD_INNER:]

    yield
    xbc_buf[CONV_PAD:CONV_PAD + tile, :] = proj(C_XBC, C_MISC)
    acc = p["conv_b"][...] + p["conv_w"][0:1, :] * xbc_buf[pl.ds(CONV_PAD - 3, tile), :]
    for j in range(1, SSD_CONV):
        acc = acc + p["conv_w"][j:j + 1, :] * xbc_buf[pl.ds(CONV_PAD - 3 + j, tile), :]
    xbc_buf[0:CONV_PAD, :] = xbc_buf[tile:tile + CONV_PAD, :]
    xa = _silu(acc)
    xs = xa[:, 0:SSD_INNER]
    bm = xa[:, SSD_INNER:SSD_INNER + SSD_GROUPS * SSD_STATE]
    cm_bf = xa[:, SSD_INNER + SSD_GROUPS * SSD_STATE:].astype(BF16)
    xw_bf = (xs * w_exp).astype(BF16)
    yield

    if need_out:
        q = proj(C_Q, C_K) * (GLA_DK ** -0.5)

    lane = lax.broadcasted_iota(jnp.int32, (1, PAIR), 1)
    for pr in range(tile // PAIR):
        rows = slice(pr * PAIR, (pr + 1) * PAIR)
        kd_t = kd[rows].T
        bm_t = bm[rows].T
        for half in range(2):
            c = 2 * pr + half
            r0 = c * CHUNK
            sel = (lane >= half * CHUNK) & (lane < (half + 1) * CHUNK)
            kd_m = jnp.where(sel, kd_t, 0.0).astype(BF16)
            upd = jnp.concatenate(
                [_dot(kd_m[hd * GLA_DK:(hd + 1) * GLA_DK], v_bf[rows, hd * GLA_DV:(hd + 1) * GLA_DV])
                 for hd in range(GLA_HEADS)], axis=0)
            s_new = gla_dec_t[:, c:c + 1] * s_gla[...] + upd
            s_gla[...] = s_new
            bm_m = jnp.where(sel, bm_t, 0.0).astype(BF16)
            ssd_new = []
            for g in range(SSD_GROUPS):
                gl = slice(g * SSD_GROUP_W, (g + 1) * SSD_GROUP_W)
                gr = slice(g * SSD_STATE, (g + 1) * SSD_STATE)
                u = _dot(bm_m[gr], xw_bf[rows, gl])
                sg = ssd_dec_exp[r0:r0 + 1, gl] * s_ssd[gr, :] + u
                s_ssd[gr, :] = sg
                ssd_new.append(sg)
            if need_out:
                s_bf = s_new.astype(BF16)
                zero = jnp.zeros((GLA_DK, GLA_DV), BF16)
                s_diag = jnp.concatenate(
                    [jnp.concatenate([s_bf[hd * GLA_DK:(hd + 1) * GLA_DK] if col == hd else zero
                                      for col in range(GLA_HEADS)], axis=1) for hd in range(GLA_HEADS)],
                    axis=0)
                mix_buf[r0:r0 + CHUNK, 0:GLA_VAL] = _dot(q[r0:r0 + CHUNK].astype(BF16), s_diag)
                for g in range(SSD_GROUPS):
                    y = _dot(cm_bf[r0:r0 + CHUNK, g * SSD_STATE:(g + 1) * SSD_STATE],
                             ssd_new[g].astype(BF16))
                    mix_buf[r0:r0 + CHUNK, GLA_VAL + g * SSD_GROUP_W:GLA_VAL + (g + 1) * SSD_GROUP_W] = y
        yield

    if not need_out:
        return None
    o_gla = _group_rms(mix_buf[:, 0:GLA_VAL], p["gla_norm_g"][...], GLA_DV) * _silu(proj(C_OG, C_Z))
    yield
    y = (mix_buf[:, GLA_VAL:] + xs * p["ssd_d"][...]) * _silu(proj(C_Z, C_XBC))
    y = _group_rms(y, p["ssd_norm_g"][...], SSD_GROUP_W)
    return o_gla, y


_MIX_PARAMS = ("w_in", "w_a2", "b_a", "conv_w", "conv_b", "dt_bias", "a_log", "uo", "uto", "eexp")
_OUT_PARAMS = ("gla_norm_g", "ssd_d", "ssd_norm_g")


def _meta_kernel(x_ref, ln_g, ln_b, *rest):
    np_ = len(_MIX_PARAMS)
    p = dict(zip(_MIX_PARAMS, rest[:np_]))
    s_gla_out, s_ssd_out, tail_out, xbc_buf = rest[np_:]
    tile = x_ref.shape[0]
    s_gla_out[...] = jnp.zeros_like(s_gla_out)
    s_ssd_out[...] = jnp.zeros_like(s_ssd_out)
    xbc_buf[0:CONV_PAD, :] = jnp.zeros((CONV_PAD, xbc_buf.shape[1]), F32)
    first_valid = tile - N_META
    valid_col = lax.broadcasted_iota(jnp.int32, (tile, 1), 0) >= first_valid
    valid_row = lax.broadcasted_iota(jnp.int32, (1, tile), 1) >= first_valid
    h = _layer_norm(x_ref[...], ln_g[...], ln_b[...])
    _run(_mixer_tile(h, tile, valid_col, valid_row, p, s_gla_out, s_ssd_out, xbc_buf, None, False))
    tail_out[...] = xbc_buf[0:CONV_PAD, :]


def _mixer_kernel(alpha, n_exp, x_ref, ln_g, ln_b, *rest):
    np_ = len(_MIX_PARAMS)
    p = dict(zip(_MIX_PARAMS, rest[:np_]))
    rest = rest[np_:]
    p.update(zip(_OUT_PARAMS, rest[:3]))
    (w_out, ln1_g, ln1_b, rw, rb, su, s_gla0, s_ssd0, tail0,
     h1_out, topi_out, gate_out, rank_out, cnt_out, base_out, tcnt_out) = rest[3:19]
    n_sub, tile = x_ref.shape[0], x_ref.shape[1]
    scratch = rest[19:]
    s_gla, s_ssd, xbc_buf, mix_buf = (scratch[k * n_sub:(k + 1) * n_sub] for k in range(4))
    run_cnt = scratch[4 * n_sub]
    b, j = pl.program_id(0), pl.program_id(1)

    @pl.when(j == 0)
    def _():
        for s in range(n_sub):
            s_gla[s][...] = s_gla0[...]
            s_ssd[s][...] = s_ssd0[...]
            xbc_buf[s][0:CONV_PAD, :] = tail0[...]

    @pl.when((b == 0) & (j == 0))
    def _():
        run_cnt[...] = jnp.zeros_like(run_cnt)

    count = [run_cnt[...]]
    pad_rows = topi_out.shape[1] - TOP_K

    def sequence(s):
        h = _layer_norm(x_ref[s], ln_g[...], ln_b[...])
        o_gla, y = yield from _mixer_tile(h, tile, None, None, p, s_gla[s], s_ssd[s], xbc_buf[s], mix_buf[s], True)
        yield
        mix = _dot(o_gla.astype(BF16), w_out[0:GLA_VAL, :]) + _dot(y.astype(BF16), w_out[GLA_VAL:, :])
        h1 = _layer_norm(alpha * h + mix, ln1_g[...], ln1_b[...])
        _store_rows(h1_out.at[s], h1)
        yield

        h_hi, h_lo = _split_hi_lo(h1)
        p_hi = _dot(h_hi, rw[...])
        logits = p_hi + (pltpu.roll(p_hi, LANES - n_exp, 1) + _dot(h_lo, rw[...]))
        lt = logits.T[0:n_exp, :] + rb[...]
        e_iota = lax.broadcasted_iota(jnp.int32, (n_exp, tile), 0).astype(F32)
        work = lt
        vals, hots = [], []
        for k in range(TOP_K):
            m = jnp.max(work, axis=0, keepdims=True)
            idx = jnp.min(jnp.where(work == m, e_iota, float(n_exp)), axis=0, keepdims=True)
            hot = e_iota == idx
            work = jnp.where(hot, -jnp.inf, work)
            vals.append(m)
            hots.append(hot)
            topi_out[s, k:k + 1, :] = idx.astype(jnp.int32)
        exps = [jnp.exp(vk - vals[0]) for vk in vals]
        denom = exps[0] + exps[1] + exps[2] + exps[3]
        for k in range(TOP_K):
            gate_out[s, k:k + 1, :] = exps[k] / denom
        member = (hots[0] | hots[1] | hots[2] | hots[3]).astype(F32).astype(BF16)
        yield
        before = _dot(member, su[...])
        for k in range(TOP_K):
            rk = jnp.sum(jnp.where(hots[k], before, 0.0), axis=0, keepdims=True)
            rank_out[s, k:k + 1, :] = rk.astype(jnp.int32)
        topi_out[s, TOP_K:, :] = jnp.zeros((pad_rows, tile), jnp.int32)
        gate_out[s, TOP_K:, :] = jnp.zeros((pad_rows, tile), F32)
        rank_out[s, TOP_K:, :] = jnp.zeros((pad_rows, tile), jnp.int32)
        tile_count = _dot(member, jnp.ones((tile, LANES), BF16))
        base_out[s, 0] = count[0]
        tcnt_out[s, 0] = tile_count
        count[0] = count[0] + tile_count

    pending = [sequence(s) for s in range(n_sub)]
    for lead in range(1, n_sub):
        for _ in range(SEQ_SKEW):
            for steps in pending[:lead]:
                next(steps)
    while pending:
        for steps in list(pending):
            try:
                next(steps)
            except StopIteration:
                pending.remove(steps)
    run_cnt[...] = count[0]
    cnt_out[...] = count[0]


def _clamped_swiglu(a, u):
    a = jnp.minimum(a, SWIGLU_LIMIT)
    u = jnp.clip(u, -SWIGLU_LIMIT, SWIGLU_LIMIT)
    return a * jax.nn.sigmoid(SWIGLU_ALPHA * a) * (u + 1.0)


SUB = 8


def _load_rows(ref, n_rows):
    return jnp.concatenate([ref[pl.ds(j, n_rows, stride=SUB), :] for j in range(SUB)], axis=1)


def _store_rows(ref, val):
    for j in range(SUB):
        ref[pl.ds(j, val.shape[0], stride=SUB), :] = val[:, j * LANES:(j + 1) * LANES]


def _tile_rows(row):
    return pl.ds(pl.multiple_of(row * SUB, SUB), SUB)


RUN_ROWS = 8
PIECE_GROUP = 4
REST_SIZES = (4, 2, 1)


def _run_rows(row):
    return pl.ds(pl.multiple_of(row * SUB, SUB), RUN_ROWS * SUB)


def _stage_tile(addr):
    return pl.ds(pl.multiple_of(addr, SUB), SUB)


def _for_pieces(pieces, fn):
    def body(g, _):
        for u in range(PIECE_GROUP):
            p = g * PIECE_GROUP + u
            fn(p * RUN_ROWS, pieces[0, 0, 1 + p])
        return 0
    lax.fori_loop(0, pieces[0, 0, 0], body, 0)


def _dispatch_kernel(fill_lo, fill_hi, pos, meta_cur, meta_prv, tail_cur, tail_prv, h_hbm, xs_hbm,
                     tbuf, stage, zrow, lsem, rsem, zsem):
    i = pl.program_id(0)
    n = pl.num_programs(0)
    tile = pos.shape[-1] // TOP_K
    slot = i % 2
    n_exp = fill_lo.shape[0]
    stage_rows = stage.shape[0] // (2 * SUB)
    rest_base = stage_rows - n_exp * RUN_ROWS

    def tile_load(j, s):
        return pltpu.make_async_copy(h_hbm.at[pl.ds(j * tile * SUB, tile * SUB), :], tbuf.at[s], lsem.at[s])

    def rows_copy(s, stage_row, sorted_row, n_rows):
        src = stage.at[pl.ds(pl.multiple_of((s * stage_rows + stage_row) * SUB, SUB), n_rows * SUB), :]
        dst = xs_hbm.at[pl.ds(pl.multiple_of(sorted_row * SUB, SUB), n_rows * SUB), :]
        return pltpu.make_async_copy(src, dst, rsem.at[s])

    def for_rest(rest, s, wait):
        for c, size in enumerate(REST_SIZES):
            first = 8 + 2 * n_exp * c

            def body(q, _, first=first, size=size):
                if wait:
                    rows_copy(s, 0, 0, size).wait()
                else:
                    rows_copy(s, rest[0, 0, first + 2 * q + 1], rest[0, 0, first + 2 * q], size).start()
                return 0
            lax.fori_loop(0, rest[0, 0, c], body, 0)

    def zero_copy(dst_row):
        return pltpu.make_async_copy(zrow, xs_hbm.at[_tile_rows(dst_row), :], zsem.at[0])

    @pl.when(i == 0)
    def _():
        tile_load(0, 0).start()
        zrow[...] = jnp.zeros_like(zrow)
        stage[...] = jnp.zeros_like(stage)
        for wait in (False, True):
            def per_expert(e, _, wait=wait):
                lo, hi = fill_lo[e], fill_hi[e]
                mid = jnp.minimum((lo + RUN_ROWS - 1) // RUN_ROWS * RUN_ROWS, hi)

                def body(row, _):
                    if wait:
                        zero_copy(0).wait()
                    else:
                        zero_copy(row).start()
                    return 0
                lax.fori_loop(lo, mid, body, 0)

                def piece(c, _):
                    cp = pltpu.make_async_copy(stage.at[_run_rows(0), :],
                                               xs_hbm.at[_run_rows(mid + c * RUN_ROWS), :], zsem.at[0])
                    if wait:
                        cp.wait()
                    else:
                        cp.start()
                    return 0
                lax.fori_loop(0, (hi - mid) // RUN_ROWS, piece, 0)
                return 0
            lax.fori_loop(0, n_exp, per_expert, 0)
        tail_lo = fill_hi[n_exp - 1]
        tail_pieces = (xs_hbm.shape[0] // SUB - tail_lo) // RUN_ROWS
        for wait in (False, True):
            def tail(c, _, wait=wait):
                cp = pltpu.make_async_copy(stage.at[_run_rows(0), :], xs_hbm.at[_run_rows(tail_lo + c * RUN_ROWS), :],
                                           zsem.at[0])
                if wait:
                    cp.wait()
                else:
                    cp.start()
                return 0
            lax.fori_loop(0, tail_pieces, tail, 0)

    @pl.when(i + 1 < n)
    def _():
        tile_load(i + 1, 1 - slot).start()

    tile_load(i, slot).wait()

    def pack(t, _):
        row = tbuf[slot, _tile_rows(t), :]
        for k in range(TOP_K):
            stage[_stage_tile(pos[0, 0, TOP_K * t + k]), :] = row
        return 0
    lax.fori_loop(0, tile, pack, 0, unroll=8)

    _for_pieces(meta_cur, lambda off, row: rows_copy(slot, off, row, RUN_ROWS).start())
    for_rest(tail_cur, slot, False)

    def wait_tile(rest, s):
        rows_copy(s, 0, 0, TOP_K * tile).wait()

        def body(q, _):
            rows_copy(s, 0, 0, RUN_ROWS).wait()
            return 0
        lax.fori_loop(0, rest[0, 0, len(REST_SIZES)], body, 0)

    @pl.when(i > 0)
    def _():
        wait_tile(tail_prv, 1 - slot)

    @pl.when(i == n - 1)
    def _():
        wait_tile(tail_cur, slot)


def _experts_kernel(blk_e, n_act, x_ref, wg, wu, bg, bu, wd, bd, y_out, wg_bf, wu_bf, wd_bf):
    i = pl.program_id(0)
    active = i < n_act[0]
    new_expert = (i == 0) | (blk_e[i] != blk_e[jnp.maximum(i - 1, 0)])

    @pl.when(active & new_expert)
    def _():
        wg_bf[...] = wg[...].astype(BF16)
        wu_bf[...] = wu[...].astype(BF16)
        wd_bf[...] = wd[...].astype(BF16)

    @pl.when(active)
    def _():
        xb = _load_rows(x_ref, x_ref.shape[0] // SUB).astype(BF16)
        hid = _clamped_swiglu(_dot(xb, wg_bf[...]) + bg[...], _dot(xb, wu_bf[...]) + bu[...])
        _store_rows(y_out, _dot(hid.astype(BF16), wd_bf[...]) + bd[...])

    @pl.when(i >= n_act[0])
    def _():
        y_out[...] = jnp.zeros_like(y_out)


def _combine_kernel(alpha, n_exp, pos, gate, meta_cur, meta_nxt, tail_cur, tail_nxt, h1, ln_g, ln_b, y_hbm, out,
                    stage, ff_rows, sem):
    i = pl.program_id(0)
    n = pl.num_programs(0)
    tile = out.shape[0]
    slot = i % 2
    stage_rows = stage.shape[0] // (2 * SUB)
    rest_base = stage_rows - n_exp * RUN_ROWS

    def run_copy(s, stage_row, sorted_row):
        return pltpu.make_async_copy(y_hbm.at[_run_rows(sorted_row), :],
                                     stage.at[_run_rows(s * stage_rows + stage_row), :], sem.at[s])

    def fetch(meta, tails, s, wait):
        act = (lambda cp: cp.wait()) if wait else (lambda cp: cp.start())
        _for_pieces(meta, lambda off, row: act(run_copy(s, 0 if wait else off, 0 if wait else row)))

        if wait:
            pltpu.make_async_copy(y_hbm.at[pl.ds(0, n_exp * RUN_ROWS * SUB), :],
                                  stage.at[pl.ds(0, n_exp * RUN_ROWS * SUB), :], sem.at[s]).wait()
            return

        def rest(g, _):
            for u in range(PIECE_GROUP):
                e = g * PIECE_GROUP + u
                run_copy(s, rest_base + e * RUN_ROWS, tails[0, 0, e]).start()
            return 0
        lax.fori_loop(0, n_exp // PIECE_GROUP, rest, 0)

    @pl.when(i == 0)
    def _():
        fetch(meta_cur, tail_cur, 0, False)

    @pl.when(i + 1 < n)
    def _():
        fetch(meta_nxt, tail_nxt, 1 - slot, False)

    fetch(meta_cur, tail_cur, slot, True)

    def gather(t, _):
        acc = gate[0, 0, TOP_K * t] * stage[_stage_tile(pos[0, 0, TOP_K * t]), :]
        for k in range(1, TOP_K):
            acc = acc + gate[0, 0, TOP_K * t + k] * stage[_stage_tile(pos[0, 0, TOP_K * t + k]), :]
        ff_rows[_tile_rows(t), :] = acc
        return 0
    lax.fori_loop(0, tile, gather, 0, unroll=8)

    out[...] = _layer_norm(alpha * _load_rows(h1, tile) + _load_rows(ff_rows, tile), ln_g[...], ln_b[...])


def _block_constants(tile):
    t = np.arange(tile)
    same = (t[:, None] // CHUNK) == (t[None, :] // CHUNK)
    later = same & (t[None, :] > t[:, None])
    chunk_rows = np.zeros((LANES, tile), np.float32)
    chunk_rows[t // CHUNK, t] = 1.0
    uo = np.concatenate([later.astype(np.float32), chunk_rows], axis=0)
    uto = np.concatenate([later.T.astype(np.float32), same.astype(np.float32)], axis=1)
    return jnp.asarray(uo, BF16), jnp.asarray(uto, BF16)


def _expand_constant():
    e = np.zeros((LANES, 2 * SSD_INNER), np.float32)
    for part in range(4):
        for hd in range(SSD_HEADS):
            base = (part // 2) * SSD_INNER + hd * SSD_HEADDIM
            e[part * SSD_HEADS + hd, base:base + SSD_HEADDIM] = 1.0
    return jnp.asarray(e, BF16)


def _full(shape):
    return pl.BlockSpec(shape, lambda *_: (0,) * len(shape))


def kernel(x, meta_tokens, ln_in_g, ln_in_b, w_in, gla_w_a2, gla_b_a, gla_norm_g, ssd_conv_w, ssd_conv_b,
           ssd_dt_bias, ssd_a_log, ssd_d, ssd_norm_g, w_out, ln1_g, ln1_b, router_w, router_b, moe_w_gate,
           moe_w_up, moe_b_gate, moe_b_up, moe_w_down, moe_b_down, ln2_g, ln2_b):
    batch, seq, d = x.shape
    depth = w_in.shape[0]
    assert depth == 1, "single-layer stack"
    n_exp = router_w.shape[-1]
    d_ff = moe_w_gate.shape[-1]
    alpha = (2.0 * depth) ** 0.25
    tile = MIXER_TILE
    assert seq % tile == 0 and d == 1024
    n_tok = batch * seq
    row = lambda a: a.reshape(1, -1).astype(F32)

    wi = w_in[0]
    o_a1 = 1536
    o_z = o_a1 + GLA_RANK
    o_xbc = o_z + SSD_INNER
    o_dt = o_xbc + SSD_CONV_CH
    misc_w = jnp.zeros((d, LANES), F32)
    misc_w = misc_w.at[:, MISC_A1:MISC_A1 + GLA_RANK].set(wi[:, o_a1:o_z])
    misc_w = misc_w.at[:, MISC_DT:MISC_DT + SSD_HEADS].set(wi[:, o_dt:o_dt + SSD_HEADS])
    w_in_r = jnp.concatenate([wi[:, 0:o_a1], wi[:, o_z:o_xbc], wi[:, o_xbc:o_dt], misc_w], axis=1).astype(BF16)
    w_a2 = jnp.zeros((LANES, GLA_KEY), F32).at[MISC_A1:MISC_A1 + GLA_RANK].set(gla_w_a2[0]).astype(BF16)
    mix_params = dict(
        w_in=w_in_r, w_a2=w_a2, b_a=row(gla_b_a[0]), conv_w=ssd_conv_w[0].astype(F32),
        conv_b=row(ssd_conv_b[0]), dt_bias=ssd_dt_bias[0].reshape(-1, 1).astype(F32),
        a_log=ssd_a_log[0].reshape(-1, 1).astype(F32), eexp=_expand_constant())
    out_params = dict(gla_norm_g=row(gla_norm_g[0]), ssd_d=row(jnp.repeat(ssd_d[0], SSD_HEADDIM)),
                      ssd_norm_g=row(ssd_norm_g[0]))

    def mix_args(t):
        uo, uto = _block_constants(t)
        vals = dict(mix_params, uo=uo, uto=uto)
        return [vals[k] for k in _MIX_PARAMS]

    m_tile = PAIR
    x_meta = jnp.concatenate([jnp.zeros((m_tile - N_META, d), F32), meta_tokens.astype(F32)], axis=0)
    meta_in = [x_meta, row(ln_in_g), row(ln_in_b)] + mix_args(m_tile)
    s_gla0, s_ssd0, tail0 = pl.pallas_call(
        _meta_kernel,
        out_shape=(jax.ShapeDtypeStruct((GLA_KEY, GLA_DV), F32),
                   jax.ShapeDtypeStruct((SSD_GROUPS * SSD_STATE, SSD_GROUP_W), F32),
                   jax.ShapeDtypeStruct((CONV_PAD, SSD_CONV_CH), F32)),
        scratch_shapes=[pltpu.VMEM((CONV_PAD + m_tile, SSD_CONV_CH), F32)],
        compiler_params=pltpu.CompilerParams(vmem_limit_bytes=VMEM_LIMIT),
        name="meta_state",
    )(*meta_in)

    n_j = seq // tile
    rw_hi, rw_lo = _split_hi_lo(router_w[0].astype(F32))
    rw = jnp.zeros((d, LANES), BF16).at[:, 0:n_exp].set(rw_hi).at[:, n_exp:2 * n_exp].set(rw_lo)
    rb = router_b[0].reshape(n_exp, 1).astype(F32)
    su = jnp.asarray(np.triu(np.ones((tile, tile), np.float32), 1), BF16)
    args = ([x, row(ln_in_g), row(ln_in_b)] + mix_args(tile) + [out_params[k] for k in _OUT_PARAMS]
            + [w_out[0].astype(BF16), row(ln1_g[0]), row(ln1_b[0]), rw, rb, su, s_gla0, s_ssd0, tail0])
    n_sub = MIXER_SEQS
    assert batch % n_sub == 0
    in_specs = [pl.BlockSpec((n_sub, tile, d), lambda b, j: (b, j, 0))] + [_full(a.shape) for a in args[1:]]
    tok_blk = pl.BlockSpec((n_sub, 8, tile), lambda b, j: (b, 0, j))
    tile_tab = pl.BlockSpec((n_sub, 1, n_exp, LANES), lambda b, j: (b, j, 0, 0))
    h1, topi, gates, rank, cnt, base, tile_cnt = pl.pallas_call(
        functools.partial(_mixer_kernel, alpha, n_exp),
        grid=(batch // n_sub, n_j),
        in_specs=in_specs,
        out_specs=(pl.BlockSpec((n_sub, tile * SUB, LANES), lambda b, j: (b, j, 0)),
                   tok_blk, tok_blk, tok_blk, _full((n_exp, LANES)), tile_tab, tile_tab),
        out_shape=(jax.ShapeDtypeStruct((batch, seq * SUB, LANES), F32),
                   jax.ShapeDtypeStruct((batch, 8, seq), jnp.int32),
                   jax.ShapeDtypeStruct((batch, 8, seq), F32),
                   jax.ShapeDtypeStruct((batch, 8, seq), jnp.int32),
                   jax.ShapeDtypeStruct((n_exp, LANES), F32),
                   jax.ShapeDtypeStruct((batch, n_j, n_exp, LANES), F32),
                   jax.ShapeDtypeStruct((batch, n_j, n_exp, LANES), F32)),
        scratch_shapes=([pltpu.VMEM((GLA_KEY, GLA_DV), F32)] * n_sub
                        + [pltpu.VMEM((SSD_GROUPS * SSD_STATE, SSD_GROUP_W), F32)] * n_sub
                        + [pltpu.VMEM((CONV_PAD + tile, SSD_CONV_CH), F32)] * n_sub
                        + [pltpu.VMEM((tile, d), F32)] * n_sub
                        + [pltpu.VMEM((n_exp, LANES), F32)]),
        compiler_params=pltpu.CompilerParams(dimension_semantics=("arbitrary", "arbitrary"),
                                             vmem_limit_bytes=VMEM_LIMIT),
        name="mixer",
    )(*args)
    h1 = h1.reshape(n_tok * SUB, LANES)
    per_tok = lambda a: a[:, :TOP_K].transpose(1, 0, 2).reshape(TOP_K, n_tok)
    top_e, gates, rank = per_tok(topi), per_tok(gates), per_tok(rank)

    counts = cnt[:, 0].astype(jnp.int32)
    padded = (counts + RUN_ROWS + MOE_BLOCK - 1) // MOE_BLOCK * MOE_BLOCK
    pad_end = jnp.cumsum(padded)
    pad_start = pad_end - padded
    n_blocks = -(-(n_tok * TOP_K) // MOE_BLOCK) + n_exp + 1
    n_rows = n_blocks * MOE_BLOCK
    blk_lo = jnp.arange(n_blocks, dtype=jnp.int32) * MOE_BLOCK
    block_e = jnp.minimum(jnp.sum((blk_lo[:, None] >= pad_end[None, :]).astype(jnp.int32), axis=1), n_exp - 1)
    n_act = (pad_end[-1] // MOE_BLOCK).astype(jnp.int32).reshape(1)
    c_tile = tile
    n_ct = n_tok // c_tile
    run_base = base[..., 0].astype(jnp.int32).reshape(n_ct, n_exp)
    run_len = tile_cnt[..., 0].astype(jnp.int32).reshape(n_ct, n_exp)
    src_row = pad_start[None, :] + run_base
    run_pieces = run_len // RUN_ROWS
    full_rows = run_pieces * RUN_ROWS
    piece_end = jnp.cumsum(run_pieces, axis=1)
    first_piece = piece_end - run_pieces
    stage_row = first_piece * RUN_ROWS
    max_pieces = -(-(TOP_K * c_tile // RUN_ROWS) // PIECE_GROUP) * PIECE_GROUP
    p_ids = jnp.arange(max_pieces, dtype=jnp.int32)
    in_run = (p_ids[None, :, None] >= first_piece[:, None, :]) & (p_ids[None, :, None] < piece_end[:, None, :])
    piece_row = jnp.sum(jnp.where(in_run, (src_row - stage_row)[:, None, :], 0), axis=2)
    piece_row = piece_row + p_ids[None, :] * RUN_ROWS
    n_pieces = piece_end[:, -1:]
    tile_parity = (jnp.arange(n_ct, dtype=jnp.int32) % 2)[:, None]
    spare_row = (n_blocks - 1) * MOE_BLOCK + (tile_parity * PIECE_GROUP + p_ids[None, :] - n_pieces) * RUN_ROWS
    piece_row = jnp.where(p_ids[None, :] < n_pieces, piece_row, jnp.clip(spare_row, 0, n_rows - RUN_ROWS))
    n_groups = (n_pieces + PIECE_GROUP - 1) // PIECE_GROUP
    meta_w = -(-(max_pieces + 1) // LANES) * LANES
    meta = jnp.concatenate([n_groups, piece_row,
                            jnp.zeros((n_ct, meta_w - 1 - max_pieces), jnp.int32)], axis=1).reshape(n_ct, 1, meta_w)
    rest_base = max_pieces * RUN_ROWS
    stage_rows = rest_base + n_exp * RUN_ROWS
    tails = jnp.concatenate([src_row + full_rows, run_len - full_rows,
                             jnp.zeros((n_ct, LANES - 2 * n_exp), jnp.int32)], axis=1).reshape(n_ct, 1, LANES)
    rest_len = run_len - full_rows
    e_row = jnp.arange(n_exp, dtype=jnp.int32)
    rest_cnt, rest_lists = [], []
    for size in REST_SIZES:
        has = (rest_len & size) != 0
        slot_in_list = jnp.cumsum(has, axis=1) - has
        before = rest_len & (7 & ~(2 * size - 1))
        sel = has[:, None, :] & (slot_in_list[:, None, :] == e_row[None, :, None])
        pick_run = lambda v: jnp.sum(jnp.where(sel, v[:, None, :], 0), axis=2)
        rest_lists.append(jnp.stack([pick_run(src_row + full_rows + before),
                                     pick_run(rest_base + e_row[None, :] * RUN_ROWS + before)], axis=2)
                          .reshape(n_ct, 2 * n_exp))
        rest_cnt.append(jnp.sum(has, axis=1, keepdims=True).astype(jnp.int32))
    rest_w = -(-(8 + 2 * n_exp * len(REST_SIZES)) // LANES) * LANES
    rest_cnt.append(n_groups * PIECE_GROUP - n_pieces)
    rest = jnp.concatenate(rest_cnt + [jnp.zeros((n_ct, 8 - len(rest_cnt)), jnp.int32)] + rest_lists
                           + [jnp.zeros((n_ct, rest_w - 8 - 2 * n_exp * len(REST_SIZES)), jnp.int32)],
                           axis=1).reshape(n_ct, 1, rest_w)
    e_ids = e_row[:, None, None]
    hot = top_e[None] == e_ids
    pack = 2 * TOP_K * c_tile
    packed = jnp.repeat((stage_row * pack + full_rows).T, c_tile, axis=1)
    picked = jnp.sum(jnp.where(hot, packed[:, None, :], 0), axis=0)
    run_stage, run_full = picked // pack, picked % pack
    pos = jnp.where(rank < run_full, run_stage + rank, rest_base + top_e * RUN_ROWS + rank - run_full)
    slot_tok = (jnp.arange(n_tok, dtype=jnp.int32) // c_tile) % 2
    pos_addr = (pos + slot_tok[None, :] * stage_rows) * SUB
    per_tile = lambda a: a.reshape(TOP_K, n_ct, c_tile).transpose(1, 2, 0).reshape(n_ct, 1, TOP_K * c_tile)
    pos3, gate3 = per_tile(pos_addr), per_tile(gates)
    smem_blk = lambda width, f: pl.BlockSpec((1, 1, width), f, memory_space=pltpu.SMEM)

    x_sorted = pl.pallas_call(
        _dispatch_kernel,
        grid_spec=pltpu.PrefetchScalarGridSpec(
            num_scalar_prefetch=2,
            grid=(n_ct,),
            in_specs=[smem_blk(TOP_K * c_tile, lambda i, lo, hi: (i, 0, 0)),
                      smem_blk(meta_w, lambda i, lo, hi: (i, 0, 0)),
                      smem_blk(meta_w, lambda i, lo, hi: (jnp.maximum(i - 1, 0), 0, 0)),
                      smem_blk(rest_w, lambda i, lo, hi: (i, 0, 0)),
                      smem_blk(rest_w, lambda i, lo, hi: (jnp.maximum(i - 1, 0), 0, 0)),
                      pl.BlockSpec(memory_space=pl.ANY)],
            out_specs=pl.BlockSpec(memory_space=pl.ANY),
            scratch_shapes=[pltpu.VMEM((2, c_tile * SUB, LANES), F32),
                            pltpu.VMEM((2 * stage_rows * SUB, LANES), F32), pltpu.VMEM((SUB, LANES), F32),
                            pltpu.SemaphoreType.DMA((2,)), pltpu.SemaphoreType.DMA((2,)),
                            pltpu.SemaphoreType.DMA((1,))]),
        out_shape=jax.ShapeDtypeStruct((n_rows * SUB, LANES), F32),
        compiler_params=pltpu.CompilerParams(dimension_semantics=("arbitrary",), vmem_limit_bytes=VMEM_LIMIT),
        name="moe_dispatch",
    )(pad_start + counts, pad_end, pos3, meta, meta, rest, rest, h1)

    last_act = lambda i, na: jnp.minimum(i, na[0] - 1)
    e_mat = lambda shape: pl.BlockSpec((None,) + shape, lambda i, be, na: (be[i], 0, 0))
    y_sorted = pl.pallas_call(
        _experts_kernel,
        grid_spec=pltpu.PrefetchScalarGridSpec(
            num_scalar_prefetch=2,
            grid=(n_blocks,),
            in_specs=[pl.BlockSpec((MOE_BLOCK * SUB, LANES), lambda i, be, na: (last_act(i, na), 0)),
                      e_mat((d, d_ff)), e_mat((d, d_ff)), e_mat((1, d_ff)), e_mat((1, d_ff)),
                      e_mat((d_ff, d)), e_mat((1, d))],
            out_specs=pl.BlockSpec((MOE_BLOCK * SUB, LANES), lambda i, be, na: (i, 0)),
            scratch_shapes=[pltpu.VMEM((d, d_ff), BF16), pltpu.VMEM((d, d_ff), BF16),
                            pltpu.VMEM((d_ff, d), BF16)]),
        out_shape=jax.ShapeDtypeStruct((n_rows * SUB, LANES), F32),
        compiler_params=pltpu.CompilerParams(dimension_semantics=("arbitrary",), vmem_limit_bytes=VMEM_LIMIT),
        name="moe_experts",
    )(block_e, n_act, x_sorted,
      moe_w_gate[0].astype(F32), moe_w_up[0].astype(F32),
      moe_b_gate[0].reshape(n_exp, 1, d_ff).astype(F32), moe_b_up[0].reshape(n_exp, 1, d_ff).astype(F32),
      moe_w_down[0].astype(F32), moe_b_down[0].reshape(n_exp, 1, d).astype(F32))

    out = pl.pallas_call(
        functools.partial(_combine_kernel, alpha, n_exp),
        grid=(n_ct,),
        in_specs=[smem_blk(TOP_K * c_tile, lambda i: (i, 0, 0)),
                  smem_blk(TOP_K * c_tile, lambda i: (i, 0, 0)),
                  smem_blk(meta_w, lambda i: (i, 0, 0)),
                  smem_blk(meta_w, lambda i: (jnp.minimum(i + 1, n_ct - 1), 0, 0)),
                  smem_blk(LANES, lambda i: (i, 0, 0)),
                  smem_blk(LANES, lambda i: (jnp.minimum(i + 1, n_ct - 1), 0, 0)),
                  pl.BlockSpec((c_tile * SUB, LANES), lambda i: (i, 0)),
                  _full((1, d)), _full((1, d)),
                  pl.BlockSpec(memory_space=pl.ANY)],
        out_specs=pl.BlockSpec((c_tile, d), lambda i: (i, 0)),
        out_shape=jax.ShapeDtypeStruct((n_tok, d), F32),
        scratch_shapes=[pltpu.VMEM((2 * stage_rows * SUB, LANES), F32), pltpu.VMEM((c_tile * SUB, LANES), F32),
                        pltpu.SemaphoreType.DMA((2,))],
        compiler_params=pltpu.CompilerParams(dimension_semantics=("arbitrary",), vmem_limit_bytes=VMEM_LIMIT),
        name="moe_combine",
    )(pos3, gate3, meta, meta, tails, tails, h1, row(ln2_g[0]), row(ln2_b[0]), y_sorted)
    return out.reshape(batch, seq, d).astype(x.dtype)
```

```python
import functools

import numpy as np
import jax
import jax.numpy as jnp
from jax import lax
from jax.experimental import pallas as pl
from jax.experimental.pallas import tpu as pltpu

F32 = jnp.float32
BF16 = jnp.bfloat16

CHUNK = 64
N_META = 16
PAIR = 2 * CHUNK
GLA_HEADS = 4
GLA_DK = 64
GLA_DV = 128
GLA_KEY = GLA_HEADS * GLA_DK
GLA_VAL = GLA_HEADS * GLA_DV
GLA_RANK = 16
GLA_TAU = 16.0
SSD_INNER = 512
SSD_HEADS = 8
SSD_HEADDIM = 64
SSD_GROUPS = 2
SSD_STATE = 128
SSD_GROUP_W = SSD_INNER // SSD_GROUPS
SSD_CONV = 4
SSD_CONV_CH = SSD_INNER + 2 * SSD_GROUPS * SSD_STATE
TOP_K = 4
SWIGLU_LIMIT = 7.0
SWIGLU_ALPHA = 1.702
MOE_BLOCK = 768
LN_EPS = 1e-5
RMS_EPS = 1e-6
LANES = 128

C_Q, C_K, C_V, C_OG, C_Z, C_XBC, C_MISC, C_END = 0, 256, 512, 1024, 1536, 2048, 3072, 3200
MISC_A1 = 0
MISC_DT = 16
CONV_PAD = 8

MIXER_TILE = 256
MIXER_SEQS = 2
SEQ_SKEW = 5
VMEM_LIMIT = 56 * 1024 * 1024


def _dot(a, b):
    return jnp.dot(a, b, preferred_element_type=F32)


def _split_hi_lo(x):
    hi = x.astype(BF16)
    lo = (x - hi.astype(F32)).astype(BF16)
    return hi, lo


def _log_sigmoid(x):
    return jnp.minimum(x, 0.0) - jnp.log1p(jnp.exp(-jnp.abs(x)))


def _softplus(x):
    return jnp.maximum(x, 0.0) + jnp.log1p(jnp.exp(-jnp.abs(x)))


def _silu(x):
    return x * jax.nn.sigmoid(x)


def _layer_norm(t, g, b):
    mu = jnp.mean(t, axis=-1, keepdims=True)
    tc = t - mu
    var = jnp.mean(tc * tc, axis=-1, keepdims=True)
    return tc * lax.rsqrt(var + LN_EPS) * g + b


def _group_rms(t, g, width):
    outs = []
    for s in range(0, t.shape[-1], width):
        seg = t[:, s:s + width]
        ms = jnp.mean(seg * seg, axis=-1, keepdims=True)
        outs.append(seg * lax.rsqrt(ms + RMS_EPS) * g[:, s:s + width])
    return jnp.concatenate(outs, axis=-1)


def _run(steps):
    try:
        while True:
            next(steps)
    except StopIteration as done:
        return done.value


def _mixer_tile(h, tile, valid_col, valid_row, p, s_gla, s_ssd, xbc_buf, mix_buf, need_out):
    hb = h.astype(BF16)
    w_in = p["w_in"]

    def proj(lo, hi):
        r = _dot(hb, w_in[:, lo:hi])
        if valid_col is not None:
            r = jnp.where(valid_col, r, 0.0)
        return r

    misc = proj(C_MISC, C_END)

    la = _log_sigmoid(_dot(misc.astype(BF16), p["w_a2"][...]) + p["b_a"][...]) * (1.0 / GLA_TAU)
    if valid_col is not None:
        la = jnp.where(valid_col, la, 0.0)
    la_hi, la_lo = _split_hi_lo(la)
    r = _dot(p["uo"][...], jnp.concatenate([la_hi, la_lo], axis=1))
    dec_exp = r[0:tile, 0:GLA_KEY] + r[0:tile, GLA_KEY:]
    tot_rows = r[tile:, 0:GLA_KEY] + r[tile:, GLA_KEY:]
    gla_dec_t = jnp.exp(tot_rows.T)
    kd = proj(C_K, C_V) * jnp.exp(dec_exp)
    v_bf = proj(C_V, C_OG).astype(BF16)

    yield
    misc_t = misc.T
    dt = _softplus(misc_t[MISC_DT:MISC_DT + SSD_HEADS, :] + p["dt_bias"][...])
    if valid_row is not None:
        dt = jnp.where(valid_row, dt, 0.0)
    dta = dt * (-jnp.exp(p["a_log"][...]))
    d_hi, d_lo = _split_hi_lo(dta)
    r2 = _dot(jnp.concatenate([d_hi, d_lo], axis=0), p["uto"][...])
    w = jnp.exp(r2[0:8, 0:tile] + r2[8:16, 0:tile]) * dt
    ssd_dec = jnp.exp(r2[0:8, tile:] + r2[8:16, tile:])
    w_hi, w_lo = _split_hi_lo(w)
    packed = jnp.concatenate(
        [w_hi.astype(F32), w_lo.astype(F32), jnp.zeros((LANES - 2 * SSD_HEADS, tile), F32)], axis=0)
    w_exp = _dot(packed.T.astype(BF16), p["eexp"][:, 0:SSD_INNER])
    head_lanes = (lax.broadcasted_iota(jnp.int32, (SSD_HEADS, SSD_INNER), 1) // SSD_HEADDIM
                  == lax.broadcasted_iota(jnp.int32, (SSD_HEADS, SSD_INNER), 0)).astype(F32)
    ssd_dec_rows = [jnp.sum(ssd_dec[:, c * CHUNK:c * CHUNK + 1] * head_lanes, axis=0, keepdims=True)
                    for c in range(tile // CHUNK)]

    yield
    xbc_buf[CONV_PAD:CONV_PAD + tile, :] = proj(C_XBC, C_MISC)
    acc = p["conv_b"][...] + p["conv_w"][0:1, :] * xbc_buf[pl.ds(CONV_PAD - 3, tile), :]
    for j in range(1, SSD_CONV):
        acc = acc + p["conv_w"][j:j + 1, :] * xbc_buf[pl.ds(CONV_PAD - 3 + j, tile), :]
    xbc_buf[0:CONV_PAD, :] = xbc_buf[tile:tile + CONV_PAD, :]
    xa = _silu(acc)
    xs = xa[:, 0:SSD_INNER]
    bm = xa[:, SSD_INNER:SSD_INNER + SSD_GROUPS * SSD_STATE]
    cm_bf = xa[:, SSD_INNER + SSD_GROUPS * SSD_STATE:].astype(BF16)
    xw_bf = (xs * w_exp).astype(BF16)
    yield

    if need_out:
        q = proj(C_Q, C_K) * (GLA_DK ** -0.5)

    lane = lax.broadcasted_iota(jnp.int32, (1, PAIR), 1)
    for pr in range(tile // PAIR):
        rows = slice(pr * PAIR, (pr + 1) * PAIR)
        kd_t = kd[rows].T
        bm_t = bm[rows].T
        for half in range(2):
            c = 2 * pr + half
            r0 = c * CHUNK
            sel = (lane >= half * CHUNK) & (lane < (half + 1) * CHUNK)
            kd_m = jnp.where(sel, kd_t, 0.0).astype(BF16)
            upd = jnp.concatenate(
                [_dot(kd_m[hd * GLA_DK:(hd + 1) * GLA_DK], v_bf[rows, hd * GLA_DV:(hd + 1) * GLA_DV])
                 for hd in range(GLA_HEADS)], axis=0)
            s_new = gla_dec_t[:, c:c + 1] * s_gla[...] + upd
            s_gla[...] = s_new
            bm_m = jnp.where(sel, bm_t, 0.0).astype(BF16)
            ssd_new = []
            for g in range(SSD_GROUPS):
                gl = slice(g * SSD_GROUP_W, (g + 1) * SSD_GROUP_W)
                gr = slice(g * SSD_STATE, (g + 1) * SSD_STATE)
                u = _dot(bm_m[gr], xw_bf[rows, gl])
                sg = ssd_dec_rows[c][:, gl] * s_ssd[gr, :] + u
                s_ssd[gr, :] = sg
                ssd_new.append(sg)
            if need_out:
                s_bf = s_new.astype(BF16)
                zero = jnp.zeros((GLA_DK, GLA_DV), BF16)
                s_diag = jnp.concatenate(
                    [jnp.concatenate([s_bf[hd * GLA_DK:(hd + 1) * GLA_DK] if col == hd else zero
                                      for col in range(GLA_HEADS)], axis=1) for hd in range(GLA_HEADS)],
                    axis=0)
                mix_buf[r0:r0 + CHUNK, 0:GLA_VAL] = _dot(q[r0:r0 + CHUNK].astype(BF16), s_diag)
                for g in range(SSD_GROUPS):
                    y = _dot(cm_bf[r0:r0 + CHUNK, g * SSD_STATE:(g + 1) * SSD_STATE],
                             ssd_new[g].astype(BF16))
                    mix_buf[r0:r0 + CHUNK, GLA_VAL + g * SSD_GROUP_W:GLA_VAL + (g + 1) * SSD_GROUP_W] = y
        yield

    if not need_out:
        return None
    o_gla = _group_rms(mix_buf[:, 0:GLA_VAL], p["gla_norm_g"][...], GLA_DV) * _silu(proj(C_OG, C_Z))
    yield
    y = (mix_buf[:, GLA_VAL:] + xs * p["ssd_d"][...]) * _silu(proj(C_Z, C_XBC))
    y = _group_rms(y, p["ssd_norm_g"][...], SSD_GROUP_W)
    return o_gla, y


_MIX_PARAMS = ("w_in", "w_a2", "b_a", "conv_w", "conv_b", "dt_bias", "a_log", "uo", "uto", "eexp")
_OUT_PARAMS = ("gla_norm_g", "ssd_d", "ssd_norm_g")


def _meta_kernel(x_ref, ln_g, ln_b, *rest):
    np_ = len(_MIX_PARAMS)
    p = dict(zip(_MIX_PARAMS, rest[:np_]))
    s_gla_out, s_ssd_out, tail_out, xbc_buf = rest[np_:]
    tile = x_ref.shape[0]
    s_gla_out[...] = jnp.zeros_like(s_gla_out)
    s_ssd_out[...] = jnp.zeros_like(s_ssd_out)
    xbc_buf[0:CONV_PAD, :] = jnp.zeros((CONV_PAD, xbc_buf.shape[1]), F32)
    first_valid = tile - N_META
    valid_col = lax.broadcasted_iota(jnp.int32, (tile, 1), 0) >= first_valid
    valid_row = lax.broadcasted_iota(jnp.int32, (1, tile), 1) >= first_valid
    h = _layer_norm(x_ref[...], ln_g[...], ln_b[...])
    _run(_mixer_tile(h, tile, valid_col, valid_row, p, s_gla_out, s_ssd_out, xbc_buf, None, False))
    tail_out[...] = xbc_buf[0:CONV_PAD, :]


def _mixer_kernel(alpha, n_exp, x_ref, ln_g, ln_b, *rest):
    np_ = len(_MIX_PARAMS)
    p = dict(zip(_MIX_PARAMS, rest[:np_]))
    rest = rest[np_:]
    p.update(zip(_OUT_PARAMS, rest[:3]))
    (w_out, ln1_g, ln1_b, rw, rb, su, s_gla0, s_ssd0, tail0,
     h1_out, topi_out, gate_out, rank_out, cnt_out, base_out, tcnt_out) = rest[3:19]
    n_sub, tile = x_ref.shape[0], x_ref.shape[1]
    scratch = rest[19:]
    s_gla, s_ssd, xbc_buf, mix_buf = (scratch[k * n_sub:(k + 1) * n_sub] for k in range(4))
    run_cnt = scratch[4 * n_sub]
    b, j = pl.program_id(0), pl.program_id(1)

    @pl.when(j == 0)
    def _():
        for s in range(n_sub):
            s_gla[s][...] = s_gla0[...]
            s_ssd[s][...] = s_ssd0[...]
            xbc_buf[s][0:CONV_PAD, :] = tail0[...]

    @pl.when((b == 0) & (j == 0))
    def _():
        run_cnt[...] = jnp.zeros_like(run_cnt)

    count = [run_cnt[...]]
    pad_rows = topi_out.shape[1] - TOP_K

    def sequence(s):
        h = _layer_norm(x_ref[s], ln_g[...], ln_b[...])
        o_gla, y = yield from _mixer_tile(h, tile, None, None, p, s_gla[s], s_ssd[s], xbc_buf[s], mix_buf[s], True)
        yield
        mix = _dot(o_gla.astype(BF16), w_out[0:GLA_VAL, :]) + _dot(y.astype(BF16), w_out[GLA_VAL:, :])
        h1 = _layer_norm(alpha * h + mix, ln1_g[...], ln1_b[...])
        _store_rows(h1_out.at[s], h1)
        yield

        h_hi, h_lo = _split_hi_lo(h1)
        p_hi = _dot(h_hi, rw[...])
        logits = p_hi + (pltpu.roll(p_hi, LANES - n_exp, 1) + _dot(h_lo, rw[...]))
        lt = logits.T[0:n_exp, :] + rb[...]
        e_iota = lax.broadcasted_iota(jnp.int32, (n_exp, tile), 0).astype(F32)
        work = lt
        vals, hots = [], []
        for k in range(TOP_K):
            m = jnp.max(work, axis=0, keepdims=True)
            idx = jnp.min(jnp.where(work == m, e_iota, float(n_exp)), axis=0, keepdims=True)
            hot = e_iota == idx
            work = jnp.where(hot, -jnp.inf, work)
            vals.append(m)
            hots.append(hot)
            topi_out[s, k:k + 1, :] = idx.astype(jnp.int32)
        exps = [jnp.exp(vk - vals[0]) for vk in vals]
        denom = exps[0] + exps[1] + exps[2] + exps[3]
        for k in range(TOP_K):
            gate_out[s, k:k + 1, :] = exps[k] / denom
        member = (hots[0] | hots[1] | hots[2] | hots[3]).astype(F32).astype(BF16)
        yield
        before = _dot(member, su[...])
        for k in range(TOP_K):
            rk = jnp.sum(jnp.where(hots[k], before, 0.0), axis=0, keepdims=True)
            rank_out[s, k:k + 1, :] = rk.astype(jnp.int32)
        topi_out[s, TOP_K:, :] = jnp.zeros((pad_rows, tile), jnp.int32)
        gate_out[s, TOP_K:, :] = jnp.zeros((pad_rows, tile), F32)
        rank_out[s, TOP_K:, :] = jnp.zeros((pad_rows, tile), jnp.int32)
        tile_count = _dot(member, jnp.ones((tile, LANES), BF16))
        base_out[s, 0] = count[0]
        tcnt_out[s, 0] = tile_count
        count[0] = count[0] + tile_count

    pending = [sequence(s) for s in range(n_sub)]
    for lead in range(1, n_sub):
        for _ in range(SEQ_SKEW):
            for steps in pending[:lead]:
                next(steps)
    while pending:
        for steps in list(pending):
            try:
                next(steps)
            except StopIteration:
                pending.remove(steps)
    run_cnt[...] = count[0]
    cnt_out[...] = count[0]


def _clamped_swiglu(a, u):
    a = jnp.minimum(a, SWIGLU_LIMIT)
    u = jnp.clip(u, -SWIGLU_LIMIT, SWIGLU_LIMIT)
    return a * jax.nn.sigmoid(SWIGLU_ALPHA * a) * (u + 1.0)


SUB = 8


def _load_rows(ref, n_rows):
    return jnp.concatenate([ref[pl.ds(j, n_rows, stride=SUB), :] for j in range(SUB)], axis=1)


def _store_rows(ref, val):
    for j in range(SUB):
        ref[pl.ds(j, val.shape[0], stride=SUB), :] = val[:, j * LANES:(j + 1) * LANES]


def _tile_rows(row):
    return pl.ds(pl.multiple_of(row * SUB, SUB), SUB)


RUN_ROWS = 8
PIECE_GROUP = 4
REST_SIZES = (4, 2, 1)


def _run_rows(row):
    return pl.ds(pl.multiple_of(row * SUB, SUB), RUN_ROWS * SUB)


def _stage_tile(addr):
    return pl.ds(pl.multiple_of(addr, SUB), SUB)


def _for_pieces(pieces, fn):
    def body(g, _):
        for u in range(PIECE_GROUP):
            p = g * PIECE_GROUP + u
            fn(p * RUN_ROWS, pieces[0, 0, 1 + p])
        return 0
    lax.fori_loop(0, pieces[0, 0, 0], body, 0)


def _dispatch_kernel(fill_lo, fill_hi, pos, meta_cur, meta_prv, tail_cur, tail_prv, h_hbm, xs_hbm,
                     tbuf, stage, zrow, lsem, rsem, zsem):
    i = pl.program_id(0)
    n = pl.num_programs(0)
    tile = pos.shape[-1] // TOP_K
    slot = i % 2
    n_exp = fill_lo.shape[0]
    stage_rows = stage.shape[0] // (2 * SUB)
    rest_base = stage_rows - n_exp * RUN_ROWS

    def tile_load(j, s):
        return pltpu.make_async_copy(h_hbm.at[pl.ds(j * tile * SUB, tile * SUB), :], tbuf.at[s], lsem.at[s])

    def rows_copy(s, stage_row, sorted_row, n_rows):
        src = stage.at[pl.ds(pl.multiple_of((s * stage_rows + stage_row) * SUB, SUB), n_rows * SUB), :]
        dst = xs_hbm.at[pl.ds(pl.multiple_of(sorted_row * SUB, SUB), n_rows * SUB), :]
        return pltpu.make_async_copy(src, dst, rsem.at[s])

    def for_rest(rest, s, wait):
        for c, size in enumerate(REST_SIZES):
            first = 8 + 2 * n_exp * c

            def body(q, _, first=first, size=size):
                if wait:
                    rows_copy(s, 0, 0, size).wait()
                else:
                    rows_copy(s, rest[0, 0, first + 2 * q + 1], rest[0, 0, first + 2 * q], size).start()
                return 0
            lax.fori_loop(0, rest[0, 0, c], body, 0)

    def zero_copy(dst_row):
        return pltpu.make_async_copy(zrow, xs_hbm.at[_tile_rows(dst_row), :], zsem.at[0])

    @pl.when(i == 0)
    def _():
        tile_load(0, 0).start()
        zrow[...] = jnp.zeros_like(zrow)
        stage[...] = jnp.zeros_like(stage)
        for wait in (False, True):
            def per_expert(e, _, wait=wait):
                lo, hi = fill_lo[e], fill_hi[e]
                mid = jnp.minimum((lo + RUN_ROWS - 1) // RUN_ROWS * RUN_ROWS, hi)

                def body(row, _):
                    if wait:
                        zero_copy(0).wait()
                    else:
                        zero_copy(row).start()
                    return 0
                lax.fori_loop(lo, mid, body, 0)

                def piece(c, _):
                    cp = pltpu.make_async_copy(stage.at[_run_rows(0), :],
                                               xs_hbm.at[_run_rows(mid + c * RUN_ROWS), :], zsem.at[0])
                    if wait:
                        cp.wait()
                    else:
                        cp.start()
                    return 0
                lax.fori_loop(0, (hi - mid) // RUN_ROWS, piece, 0)
                return 0
            lax.fori_loop(0, n_exp, per_expert, 0)
        tail_lo = fill_hi[n_exp - 1]
        tail_pieces = (xs_hbm.shape[0] // SUB - tail_lo) // RUN_ROWS
        for wait in (False, True):
            def tail(c, _, wait=wait):
                cp = pltpu.make_async_copy(stage.at[_run_rows(0), :], xs_hbm.at[_run_rows(tail_lo + c * RUN_ROWS), :],
                                           zsem.at[0])
                if wait:
                    cp.wait()
                else:
                    cp.start()
                return 0
            lax.fori_loop(0, tail_pieces, tail, 0)

    @pl.when(i + 1 < n)
    def _():
        tile_load(i + 1, 1 - slot).start()

    tile_load(i, slot).wait()

    def pack(t, _):
        row = tbuf[slot, _tile_rows(t), :]
        for k in range(TOP_K):
            stage[_stage_tile(pos[0, 0, TOP_K * t + k]), :] = row
        return 0
    lax.fori_loop(0, tile, pack, 0, unroll=8)

    _for_pieces(meta_cur, lambda off, row: rows_copy(slot, off, row, RUN_ROWS).start())
    for_rest(tail_cur, slot, False)

    def wait_tile(rest, s):
        rows_copy(s, 0, 0, TOP_K * tile).wait()

        def body(q, _):
            rows_copy(s, 0, 0, RUN_ROWS).wait()
            return 0
        lax.fori_loop(0, rest[0, 0, len(REST_SIZES)], body, 0)

    @pl.when(i > 0)
    def _():
        wait_tile(tail_prv, 1 - slot)

    @pl.when(i == n - 1)
    def _():
        wait_tile(tail_cur, slot)


def _experts_kernel(blk_e, n_act, x_ref, wg, wu, bg, bu, wd, bd, y_out, wg_bf, wu_bf, wd_bf):
    i = pl.program_id(0)
    active = i < n_act[0]
    new_expert = (i == 0) | (blk_e[i] != blk_e[jnp.maximum(i - 1, 0)])

    @pl.when(active & new_expert)
    def _():
        wg_bf[...] = wg[...].astype(BF16)
        wu_bf[...] = wu[...].astype(BF16)
        wd_bf[...] = wd[...].astype(BF16)

    @pl.when(active)
    def _():
        xb = _load_rows(x_ref, x_ref.shape[0] // SUB).astype(BF16)
        hid = _clamped_swiglu(_dot(xb, wg_bf[...]) + bg[...], _dot(xb, wu_bf[...]) + bu[...])
        _store_rows(y_out, _dot(hid.astype(BF16), wd_bf[...]) + bd[...])

    @pl.when(i >= n_act[0])
    def _():
        y_out[...] = jnp.zeros_like(y_out)


def _combine_kernel(alpha, n_exp, pos, gate, meta_cur, meta_nxt, tail_cur, tail_nxt, h1, ln_g, ln_b, y_hbm, out,
                    stage, ff_rows, sem):
    i = pl.program_id(0)
    n = pl.num_programs(0)
    tile = out.shape[0]
    slot = i % 2
    stage_rows = stage.shape[0] // (2 * SUB)
    rest_base = stage_rows - n_exp * RUN_ROWS

    def run_copy(s, stage_row, sorted_row):
        return pltpu.make_async_copy(y_hbm.at[_run_rows(sorted_row), :],
                                     stage.at[_run_rows(s * stage_rows + stage_row), :], sem.at[s])

    def fetch(meta, tails, s, wait):
        act = (lambda cp: cp.wait()) if wait else (lambda cp: cp.start())
        _for_pieces(meta, lambda off, row: act(run_copy(s, 0 if wait else off, 0 if wait else row)))

        if wait:
            pltpu.make_async_copy(y_hbm.at[pl.ds(0, n_exp * RUN_ROWS * SUB), :],
                                  stage.at[pl.ds(0, n_exp * RUN_ROWS * SUB), :], sem.at[s]).wait()
            return

        def rest(g, _):
            for u in range(PIECE_GROUP):
                e = g * PIECE_GROUP + u
                run_copy(s, rest_base + e * RUN_ROWS, tails[0, 0, e]).start()
            return 0
        lax.fori_loop(0, n_exp // PIECE_GROUP, rest, 0)

    @pl.when(i == 0)
    def _():
        fetch(meta_cur, tail_cur, 0, False)

    @pl.when(i + 1 < n)
    def _():
        fetch(meta_nxt, tail_nxt, 1 - slot, False)

    fetch(meta_cur, tail_cur, slot, True)

    def gather(t, _):
        acc = gate[0, 0, TOP_K * t] * stage[_stage_tile(pos[0, 0, TOP_K * t]), :]
        for k in range(1, TOP_K):
            acc = acc + gate[0, 0, TOP_K * t + k] * stage[_stage_tile(pos[0, 0, TOP_K * t + k]), :]
        ff_rows[_tile_rows(t), :] = acc
        return 0
    lax.fori_loop(0, tile, gather, 0, unroll=8)

    out[...] = _layer_norm(alpha * _load_rows(h1, tile) + _load_rows(ff_rows, tile), ln_g[...], ln_b[...])


def _block_constants(tile):
    t = np.arange(tile)
    same = (t[:, None] // CHUNK) == (t[None, :] // CHUNK)
    later = same & (t[None, :] > t[:, None])
    chunk_rows = np.zeros((LANES, tile), np.float32)
    chunk_rows[t // CHUNK, t] = 1.0
    uo = np.concatenate([later.astype(np.float32), chunk_rows], axis=0)
    uto = np.concatenate([later.T.astype(np.float32), same.astype(np.float32)], axis=1)
    return jnp.asarray(uo, BF16), jnp.asarray(uto, BF16)


def _expand_constant():
    e = np.zeros((LANES, 2 * SSD_INNER), np.float32)
    for part in range(4):
        for hd in range(SSD_HEADS):
            base = (part // 2) * SSD_INNER + hd * SSD_HEADDIM
            e[part * SSD_HEADS + hd, base:base + SSD_HEADDIM] = 1.0
    return jnp.asarray(e, BF16)


def _full(shape):
    return pl.BlockSpec(shape, lambda *_: (0,) * len(shape))


def kernel(x, meta_tokens, ln_in_g, ln_in_b, w_in, gla_w_a2, gla_b_a, gla_norm_g, ssd_conv_w, ssd_conv_b,
           ssd_dt_bias, ssd_a_log, ssd_d, ssd_norm_g, w_out, ln1_g, ln1_b, router_w, router_b, moe_w_gate,
           moe_w_up, moe_b_gate, moe_b_up, moe_w_down, moe_b_down, ln2_g, ln2_b):
    batch, seq, d = x.shape
    depth = w_in.shape[0]
    assert depth == 1, "single-layer stack"
    n_exp = router_w.shape[-1]
    d_ff = moe_w_gate.shape[-1]
    alpha = (2.0 * depth) ** 0.25
    tile = MIXER_TILE
    assert seq % tile == 0 and d == 1024
    n_tok = batch * seq
    row = lambda a: a.reshape(1, -1).astype(F32)

    wi = w_in[0]
    o_a1 = 1536
    o_z = o_a1 + GLA_RANK
    o_xbc = o_z + SSD_INNER
    o_dt = o_xbc + SSD_CONV_CH
    misc_w = jnp.zeros((d, LANES), F32)
    misc_w = misc_w.at[:, MISC_A1:MISC_A1 + GLA_RANK].set(wi[:, o_a1:o_z])
    misc_w = misc_w.at[:, MISC_DT:MISC_DT + SSD_HEADS].set(wi[:, o_dt:o_dt + SSD_HEADS])
    w_in_r = jnp.concatenate([wi[:, 0:o_a1], wi[:, o_z:o_xbc], wi[:, o_xbc:o_dt], misc_w], axis=1).astype(BF16)
    w_a2 = jnp.zeros((LANES, GLA_KEY), F32).at[MISC_A1:MISC_A1 + GLA_RANK].set(gla_w_a2[0]).astype(BF16)
    mix_params = dict(
        w_in=w_in_r, w_a2=w_a2, b_a=row(gla_b_a[0]), conv_w=ssd_conv_w[0].astype(F32),
        conv_b=row(ssd_conv_b[0]), dt_bias=ssd_dt_bias[0].reshape(-1, 1).astype(F32),
        a_log=ssd_a_log[0].reshape(-1, 1).astype(F32), eexp=_expand_constant())
    out_params = dict(gla_norm_g=row(gla_norm_g[0]), ssd_d=row(jnp.repeat(ssd_d[0], SSD_HEADDIM)),
                      ssd_norm_g=row(ssd_norm_g[0]))

    def mix_args(t):
        uo, uto = _block_constants(t)
        vals = dict(mix_params, uo=uo, uto=uto)
        return [vals[k] for k in _MIX_PARAMS]

    m_tile = PAIR
    x_meta = jnp.concatenate([jnp.zeros((m_tile - N_META, d), F32), meta_tokens.astype(F32)], axis=0)
    meta_in = [x_meta, row(ln_in_g), row(ln_in_b)] + mix_args(m_tile)
    s_gla0, s_ssd0, tail0 = pl.pallas_call(
        _meta_kernel,
        out_shape=(jax.ShapeDtypeStruct((GLA_KEY, GLA_DV), F32),
                   jax.ShapeDtypeStruct((SSD_GROUPS * SSD_STATE, SSD_GROUP_W), F32),
                   jax.ShapeDtypeStruct((CONV_PAD, SSD_CONV_CH), F32)),
        scratch_shapes=[pltpu.VMEM((CONV_PAD + m_tile, SSD_CONV_CH), F32)],
        compiler_params=pltpu.CompilerParams(vmem_limit_bytes=VMEM_LIMIT),
        name="meta_state",
    )(*meta_in)

    n_j = seq // tile
    rw_hi, rw_lo = _split_hi_lo(router_w[0].astype(F32))
    rw = jnp.zeros((d, LANES), BF16).at[:, 0:n_exp].set(rw_hi).at[:, n_exp:2 * n_exp].set(rw_lo)
    rb = router_b[0].reshape(n_exp, 1).astype(F32)
    su = jnp.asarray(np.triu(np.ones((tile, tile), np.float32), 1), BF16)
    args = ([x, row(ln_in_g), row(ln_in_b)] + mix_args(tile) + [out_params[k] for k in _OUT_PARAMS]
            + [w_out[0].astype(BF16), row(ln1_g[0]), row(ln1_b[0]), rw, rb, su, s_gla0, s_ssd0, tail0])
    n_sub = MIXER_SEQS
    assert batch % n_sub == 0
    in_specs = [pl.BlockSpec((n_sub, tile, d), lambda b, j: (b, j, 0))] + [_full(a.shape) for a in args[1:]]
    tok_blk = pl.BlockSpec((n_sub, 8, tile), lambda b, j: (b, 0, j))
    tile_tab = pl.BlockSpec((n_sub, 1, n_exp, LANES), lambda b, j: (b, j, 0, 0))
    h1, topi, gates, rank, cnt, base, tile_cnt = pl.pallas_call(
        functools.partial(_mixer_kernel, alpha, n_exp),
        grid=(batch // n_sub, n_j),
        in_specs=in_specs,
        out_specs=(pl.BlockSpec((n_sub, tile * SUB, LANES), lambda b, j: (b, j, 0)),
                   tok_blk, tok_blk, tok_blk, _full((n_exp, LANES)), tile_tab, tile_tab),
        out_shape=(jax.ShapeDtypeStruct((batch, seq * SUB, LANES), F32),
                   jax.ShapeDtypeStruct((batch, 8, seq), jnp.int32),
                   jax.ShapeDtypeStruct((batch, 8, seq), F32),
                   jax.ShapeDtypeStruct((batch, 8, seq), jnp.int32),
                   jax.ShapeDtypeStruct((n_exp, LANES), F32),
                   jax.ShapeDtypeStruct((batch, n_j, n_exp, LANES), F32),
                   jax.ShapeDtypeStruct((batch, n_j, n_exp, LANES), F32)),
        scratch_shapes=([pltpu.VMEM((GLA_KEY, GLA_DV), F32)] * n_sub
                        + [pltpu.VMEM((SSD_GROUPS * SSD_STATE, SSD_GROUP_W), F32)] * n_sub
                        + [pltpu.VMEM((CONV_PAD + tile, SSD_CONV_CH), F32)] * n_sub
                        + [pltpu.VMEM((tile, d), F32)] * n_sub
                        + [pltpu.VMEM((n_exp, LANES), F32)]),
        compiler_params=pltpu.CompilerParams(dimension_semantics=("arbitrary", "arbitrary"),
                                             vmem_limit_bytes=VMEM_LIMIT),
        name="mixer",
    )(*args)
    h1 = h1.reshape(n_tok * SUB, LANES)
    per_tok = lambda a: a[:, :TOP_K].transpose(1, 0, 2).reshape(TOP_K, n_tok)
    top_e, gates, rank = per_tok(topi), per_tok(gates), per_tok(rank)

    counts = cnt[:, 0].astype(jnp.int32)
    padded = (counts + RUN_ROWS + MOE_BLOCK - 1) // MOE_BLOCK * MOE_BLOCK
    pad_end = jnp.cumsum(padded)
    pad_start = pad_end - padded
    n_blocks = -(-(n_tok * TOP_K) // MOE_BLOCK) + n_exp + 1
    n_rows = n_blocks * MOE_BLOCK
    blk_lo = jnp.arange(n_blocks, dtype=jnp.int32) * MOE_BLOCK
    block_e = jnp.minimum(jnp.sum((blk_lo[:, None] >= pad_end[None, :]).astype(jnp.int32), axis=1), n_exp - 1)
    n_act = (pad_end[-1] // MOE_BLOCK).astype(jnp.int32).reshape(1)
    c_tile = tile
    n_ct = n_tok // c_tile
    run_base = base[..., 0].astype(jnp.int32).reshape(n_ct, n_exp)
    run_len = tile_cnt[..., 0].astype(jnp.int32).reshape(n_ct, n_exp)
    src_row = pad_start[None, :] + run_base
    run_pieces = run_len // RUN_ROWS
    full_rows = run_pieces * RUN_ROWS
    piece_end = jnp.cumsum(run_pieces, axis=1)
    first_piece = piece_end - run_pieces
    stage_row = first_piece * RUN_ROWS
    max_pieces = -(-(TOP_K * c_tile // RUN_ROWS) // PIECE_GROUP) * PIECE_GROUP
    p_ids = jnp.arange(max_pieces, dtype=jnp.int32)
    in_run = (p_ids[None, :, None] >= first_piece[:, None, :]) & (p_ids[None, :, None] < piece_end[:, None, :])
    piece_row = jnp.sum(jnp.where(in_run, (src_row - stage_row)[:, None, :], 0), axis=2)
    piece_row = piece_row + p_ids[None, :] * RUN_ROWS
    n_pieces = piece_end[:, -1:]
    tile_parity = (jnp.arange(n_ct, dtype=jnp.int32) % 2)[:, None]
    spare_row = (n_blocks - 1) * MOE_BLOCK + (tile_parity * PIECE_GROUP + p_ids[None, :] - n_pieces) * RUN_ROWS
    piece_row = jnp.where(p_ids[None, :] < n_pieces, piece_row, jnp.clip(spare_row, 0, n_rows - RUN_ROWS))
    n_groups = (n_pieces + PIECE_GROUP - 1) // PIECE_GROUP
    meta_w = -(-(max_pieces + 1) // LANES) * LANES
    meta = jnp.concatenate([n_groups, piece_row,
                            jnp.zeros((n_ct, meta_w - 1 - max_pieces), jnp.int32)], axis=1).reshape(n_ct, 1, meta_w)
    rest_base = max_pieces * RUN_ROWS
    stage_rows = rest_base + n_exp * RUN_ROWS
    tails = jnp.concatenate([src_row + full_rows, run_len - full_rows,
                             jnp.zeros((n_ct, LANES - 2 * n_exp), jnp.int32)], axis=1).reshape(n_ct, 1, LANES)
    rest_len = run_len - full_rows
    e_row = jnp.arange(n_exp, dtype=jnp.int32)
    rest_cnt, rest_lists = [], []
    for size in REST_SIZES:
        has = (rest_len & size) != 0
        slot_in_list = jnp.cumsum(has, axis=1) - has
        before = rest_len & (7 & ~(2 * size - 1))
        sel = has[:, None, :] & (slot_in_list[:, None, :] == e_row[None, :, None])
        pick_run = lambda v: jnp.sum(jnp.where(sel, v[:, None, :], 0), axis=2)
        rest_lists.append(jnp.stack([pick_run(src_row + full_rows + before),
                                     pick_run(rest_base + e_row[None, :] * RUN_ROWS + before)], axis=2)
                          .reshape(n_ct, 2 * n_exp))
        rest_cnt.append(jnp.sum(has, axis=1, keepdims=True).astype(jnp.int32))
    rest_w = -(-(8 + 2 * n_exp * len(REST_SIZES)) // LANES) * LANES
    rest_cnt.append(n_groups * PIECE_GROUP - n_pieces)
    rest = jnp.concatenate(rest_cnt + [jnp.zeros((n_ct, 8 - len(rest_cnt)), jnp.int32)] + rest_lists
                           + [jnp.zeros((n_ct, rest_w - 8 - 2 * n_exp * len(REST_SIZES)), jnp.int32)],
                           axis=1).reshape(n_ct, 1, rest_w)
    e_ids = e_row[:, None, None]
    hot = top_e[None] == e_ids
    pack = 2 * TOP_K * c_tile
    packed = jnp.repeat((stage_row * pack + full_rows).T, c_tile, axis=1)
    picked = jnp.sum(jnp.where(hot, packed[:, None, :], 0), axis=0)
    run_stage, run_full = picked // pack, picked % pack
    pos = jnp.where(rank < run_full, run_stage + rank, rest_base + top_e * RUN_ROWS + rank - run_full)
    slot_tok = (jnp.arange(n_tok, dtype=jnp.int32) // c_tile) % 2
    pos_addr = (pos + slot_tok[None, :] * stage_rows) * SUB
    per_tile = lambda a: a.reshape(TOP_K, n_ct, c_tile).transpose(1, 2, 0).reshape(n_ct, 1, TOP_K * c_tile)
    pos3, gate3 = per_tile(pos_addr), per_tile(gates)
    smem_blk = lambda width, f: pl.BlockSpec((1, 1, width), f, memory_space=pltpu.SMEM)

    x_sorted = pl.pallas_call(
        _dispatch_kernel,
        grid_spec=pltpu.PrefetchScalarGridSpec(
            num_scalar_prefetch=2,
            grid=(n_ct,),
            in_specs=[smem_blk(TOP_K * c_tile, lambda i, lo, hi: (i, 0, 0)),
                      smem_blk(meta_w, lambda i, lo, hi: (i, 0, 0)),
                      smem_blk(meta_w, lambda i, lo, hi: (jnp.maximum(i - 1, 0), 0, 0)),
                      smem_blk(rest_w, lambda i, lo, hi: (i, 0, 0)),
                      smem_blk(rest_w, lambda i, lo, hi: (jnp.maximum(i - 1, 0), 0, 0)),
                      pl.BlockSpec(memory_space=pl.ANY)],
            out_specs=pl.BlockSpec(memory_space=pl.ANY),
            scratch_shapes=[pltpu.VMEM((2, c_tile * SUB, LANES), F32),
                            pltpu.VMEM((2 * stage_rows * SUB, LANES), F32), pltpu.VMEM((SUB, LANES), F32),
                            pltpu.SemaphoreType.DMA((2,)), pltpu.SemaphoreType.DMA((2,)),
                            pltpu.SemaphoreType.DMA((1,))]),
        out_shape=jax.ShapeDtypeStruct((n_rows * SUB, LANES), F32),
        compiler_params=pltpu.CompilerParams(dimension_semantics=("arbitrary",), vmem_limit_bytes=VMEM_LIMIT),
        name="moe_dispatch",
    )(pad_start + counts, pad_end, pos3, meta, meta, rest, rest, h1)

    last_act = lambda i, na: jnp.minimum(i, na[0] - 1)
    e_mat = lambda shape: pl.BlockSpec((None,) + shape, lambda i, be, na: (be[i], 0, 0))
    y_sorted = pl.pallas_call(
        _experts_kernel,
        grid_spec=pltpu.PrefetchScalarGridSpec(
            num_scalar_prefetch=2,
            grid=(n_blocks,),
            in_specs=[pl.BlockSpec((MOE_BLOCK * SUB, LANES), lambda i, be, na: (last_act(i, na), 0)),
                      e_mat((d, d_ff)), e_mat((d, d_ff)), e_mat((1, d_ff)), e_mat((1, d_ff)),
                      e_mat((d_ff, d)), e_mat((1, d))],
            out_specs=pl.BlockSpec((MOE_BLOCK * SUB, LANES), lambda i, be, na: (i, 0)),
            scratch_shapes=[pltpu.VMEM((d, d_ff), BF16), pltpu.VMEM((d, d_ff), BF16),
                            pltpu.VMEM((d_ff, d), BF16)]),
        out_shape=jax.ShapeDtypeStruct((n_rows * SUB, LANES), F32),
        compiler_params=pltpu.CompilerParams(dimension_semantics=("arbitrary",), vmem_limit_bytes=VMEM_LIMIT),
        name="moe_experts",
    )(block_e, n_act, x_sorted,
      moe_w_gate[0].astype(F32), moe_w_up[0].astype(F32),
      moe_b_gate[0].reshape(n_exp, 1, d_ff).astype(F32), moe_b_up[0].reshape(n_exp, 1, d_ff).astype(F32),
      moe_w_down[0].astype(F32), moe_b_down[0].reshape(n_exp, 1, d).astype(F32))

    out = pl.pallas_call(
        functools.partial(_combine_kernel, alpha, n_exp),
        grid=(n_ct,),
        in_specs=[smem_blk(TOP_K * c_tile, lambda i: (i, 0, 0)),
                  smem_blk(TOP_K * c_tile, lambda i: (i, 0, 0)),
                  smem_blk(meta_w, lambda i: (i, 0, 0)),
                  smem_blk(meta_w, lambda i: (jnp.minimum(i + 1, n_ct - 1), 0, 0)),
                  smem_blk(LANES, lambda i: (i, 0, 0)),
                  smem_blk(LANES, lambda i: (jnp.minimum(i + 1, n_ct - 1), 0, 0)),
                  pl.BlockSpec((c_tile * SUB, LANES), lambda i: (i, 0)),
                  _full((1, d)), _full((1, d)),
                  pl.BlockSpec(memory_space=pl.ANY)],
        out_specs=pl.BlockSpec((c_tile, d), lambda i: (i, 0)),
        out_shape=jax.ShapeDtypeStruct((n_tok, d), F32),
        scratch_shapes=[pltpu.VMEM((2 * stage_rows * SUB, LANES), F32), pltpu.VMEM((c_tile * SUB, LANES), F32),
                        pltpu.SemaphoreType.DMA((2,))],
        compiler_params=pltpu.CompilerParams(dimension_semantics=("arbitrary",), vmem_limit_bytes=VMEM_LIMIT),
        name="moe_combine",
    )(pos3, gate3, meta, meta, tails, tails, h1, row(ln2_g[0]), row(ln2_b[0]), y_sorted)
    return out.reshape(batch, seq, d).astype(x.dtype)
```
